```python
import jax, jax.numpy as jnp
from jax import lax
import numpy as np

D_MODEL = 1024
BATCH = 2
SEQ = 8192
DEPTH = 1

MLA_HEADS = 8
MLA_Q_LORA = 384
MLA_KV_LORA = 256
MLA_NOPE = 64
MLA_ROPE = 32
MLA_V = 64
MLA_W = MLA_HEADS * MLA_V
ROPE_THETA = 10000.0
Q_BLOCK = 128
MLSTM_HEADS = 4
MLSTM_DH = 128
MLSTM_W = MLSTM_HEADS * MLSTM_DH
MLSTM_CHUNK = 64
CONV_WIDTH = 5
D_FF = 4 * D_MODEL
NORM_EPS = 1e-6
IN_SPLITS = (MLA_Q_LORA, MLA_KV_LORA, MLA_ROPE, MLSTM_W, MLSTM_W, MLSTM_W, MLSTM_W, 4 * MLSTM_HEADS, D_MODEL, D_MODEL)
IN_COLS = MLA_Q_LORA + MLA_KV_LORA + MLA_ROPE + 4 * MLSTM_W + 4 * MLSTM_HEADS + 2 * D_MODEL

kernel_name = "hybrid_mla_mlstm_gated_block"


def rmsnorm(x, g):
    xf = x.astype(jnp.float32)
    y = xf * lax.rsqrt(jnp.mean(xf * xf, axis=-1, keepdims=True) + NORM_EPS)
    return (y * g.astype(jnp.float32)).astype(x.dtype)


def split_cols(h):
    outs, off = [], 0
    for w in IN_SPLITS:
        outs.append(h[..., off:off + w])
        off += w
    return outs


def rope_tables(positions):
    inv_freq = ROPE_THETA ** (-jnp.arange(0, MLA_ROPE, 2, dtype=jnp.float32) / MLA_ROPE)
    ang = positions.astype(jnp.float32)[..., None] * inv_freq
    return jnp.cos(ang), jnp.sin(ang)


def apply_rope(x, cos, sin):
    xf = x.astype(jnp.float32)
    half = xf.shape[-1] // 2
    x1, x2 = xf[..., :half], xf[..., half:]
    return jnp.concatenate([x1 * cos - x2 * sin, x2 * cos + x1 * sin], axis=-1).astype(x.dtype)


def mla_attention(q_nope, q_rope, k_nope, k_rope, v):
    B, S, H, _ = q_nope.shape
    nb = S // Q_BLOCK
    scale = (MLA_NOPE + MLA_ROPE) ** -0.5
    qn = q_nope.reshape(B, nb, Q_BLOCK, H, MLA_NOPE).swapaxes(0, 1)
    qr = q_rope.reshape(B, nb, Q_BLOCK, H, MLA_ROPE).swapaxes(0, 1)
    kn = k_nope.astype(jnp.float32)
    kr = k_rope.astype(jnp.float32)
    vf = v.astype(jnp.float32)

    def block(args):
        qn_b, qr_b = args
        s = (jnp.einsum('bqhd,bkhd->bhqk', qn_b.astype(jnp.float32), kn)
             + jnp.einsum('bqhd,bkd->bhqk', qr_b.astype(jnp.float32), kr))
        p = jax.nn.softmax(s * scale, axis=-1)
        return jnp.einsum('bhqk,bkhd->bqhd', p, vf)

    o = lax.map(block, (qn, qr))
    return o.swapaxes(0, 1).reshape(B, S, H * MLA_V).astype(v.dtype)


def mla_branch(c_q, c_kv, k_rope_raw, cos, sin, q_norm_g, w_uq, kv_norm_g, w_ukv):
    B, S, _ = c_q.shape
    q = (rmsnorm(c_q, q_norm_g) @ w_uq).reshape(B, S, MLA_HEADS, MLA_NOPE + MLA_ROPE)
    q_nope = q[..., :MLA_NOPE]
    q_rope = apply_rope(q[..., MLA_NOPE:], cos[:, :, None, :], sin[:, :, None, :])
    k_rope = apply_rope(k_rope_raw, cos, sin)
    kv = (rmsnorm(c_kv, kv_norm_g) @ w_ukv).reshape(B, S, MLA_HEADS, MLA_NOPE + MLA_V)
    k_nope, v = kv[..., :MLA_NOPE], kv[..., MLA_NOPE:]
    return mla_attention(q_nope, q_rope, k_nope, k_rope, v)


def mlstm_chunkwise(q, k, v, i_pre, f_pre):
    B, H, S, DK = q.shape
    DV = v.shape[-1]
    L = MLSTM_CHUNK
    NC = S // L
    qc = q.astype(jnp.float32).reshape(B, H, NC, L, DK)
    kc = k.astype(jnp.float32).reshape(B, H, NC, L, DK) * (DK ** -0.5)
    vc = v.astype(jnp.float32).reshape(B, H, NC, L, DV)
    log_f = jax.nn.log_sigmoid(f_pre.astype(jnp.float32)).reshape(B, H, NC, L)
    log_i = i_pre.astype(jnp.float32).reshape(B, H, NC, L)
    b = jnp.cumsum(log_f, axis=-1)
    b_last = b[..., -1]
    w_end = b_last[..., None] - b + log_i

    def step(carry, inp):
        C, n, m = carry
        k_c, v_c, w_c, bl = inp
        m_new = jnp.maximum(bl + m, jnp.max(w_c, axis=-1))
        decay = jnp.exp(bl + m - m_new)
        w = jnp.exp(w_c - m_new[..., None])
        C_new = decay[..., None, None] * C + jnp.einsum('bhl,bhlk,bhlv->bhkv', w, k_c, v_c)
        n_new = decay[..., None] * n + jnp.einsum('bhl,bhlk->bhk', w, k_c)
        return (C_new, n_new, m_new), (C, n, m)

    init = (jnp.zeros((B, H, DK, DV), jnp.float32), jnp.zeros((B, H, DK), jnp.float32),
            jnp.zeros((B, H), jnp.float32))
    xs = (jnp.moveaxis(kc, 2, 0), jnp.moveaxis(vc, 2, 0), jnp.moveaxis(w_end, 2, 0), jnp.moveaxis(b_last, 2, 0))
    _, (C_prev, n_prev, m_prev) = lax.scan(step, init, xs)
    C_prev = jnp.moveaxis(C_prev, 0, 2)
    n_prev = jnp.moveaxis(n_prev, 0, 2)
    m_prev = jnp.moveaxis(m_prev, 0, 2)

    mask = jnp.tril(jnp.ones((L, L), dtype=bool))
    D = jnp.where(mask, b[..., :, None] - b[..., None, :] + log_i[..., None, :], -jnp.inf)
    inter_log = b + m_prev[..., None]
    m_t = jnp.maximum(inter_log, jnp.max(D, axis=-1))
    inter_w = jnp.exp(inter_log - m_t)
    qk = jnp.einsum('bhctd,bhcsd->bhcts', qc, kc) * jnp.exp(D - m_t[..., None])
    num = inter_w[..., None] * jnp.einsum('bhctk,bhckv->bhctv', qc, C_prev) + jnp.einsum('bhcts,bhcsv->bhctv', qk, vc)
    den = inter_w * jnp.einsum('bhctk,bhck->bhct', qc, n_prev) + jnp.sum(qk, axis=-1)
    h = num / jnp.maximum(jnp.abs(den), jnp.exp(-m_t))[..., None]
    return h.reshape(B, H, S, DV)


def mlstm_branch(mq, mk, mv, mo, mgates, conv_w, conv_b, ig_b, fg_b, out_norm_g):
    B, S, _ = mq.shape
    qk = jnp.concatenate([mq, mk], axis=-1)
    qk = lax.conv_general_dilated(qk, conv_w, window_strides=(1,),
                                  padding=[(CONV_WIDTH // 2, CONV_WIDTH // 2)],
                                  dimension_numbers=('NWC', 'WIO', 'NWC'),
                                  feature_group_count=2 * MLSTM_W) + conv_b
    qk = jax.nn.silu(qk)

    def heads(t):
        return t.reshape(B, S, MLSTM_HEADS, MLSTM_DH).transpose(0, 2, 1, 3)

    q, k, v = heads(qk[..., :MLSTM_W]), heads(qk[..., MLSTM_W:]), heads(mv)
    g = mgates.reshape(B, S, 2, 2, MLSTM_HEADS)
    i_pre = (g[:, :, :, 0, :] + ig_b).transpose(2, 0, 3, 1)
    f_pre = (g[:, :, :, 1, :] + fg_b).transpose(2, 0, 3, 1)
    h_fwd = mlstm_chunkwise(q, k, v, i_pre[0], f_pre[0])
    h_bwd = jnp.flip(mlstm_chunkwise(jnp.flip(q, 2), jnp.flip(k, 2), jnp.flip(v, 2),
                                     jnp.flip(i_pre[1], -1), jnp.flip(f_pre[1], -1)), 2)
    h = h_fwd + h_bwd
    h = h * lax.rsqrt(jnp.mean(h * h, axis=-1, keepdims=True) + NORM_EPS)
    h = h.transpose(0, 2, 1, 3).reshape(B, S, MLSTM_W) * out_norm_g.astype(jnp.float32)
    return (h * jax.nn.sigmoid(mo.astype(jnp.float32))).astype(mq.dtype)


def setup_inputs(seed: int = 0) -> dict:
    key = jax.random.key(seed)
    ks = jax.random.split(key, 24)

    def nrm(k, shape, fan_in):
        return jax.random.normal(k, shape, jnp.float32) * (fan_in ** -0.5)

    def gain(k, shape):
        return 1.0 + 0.02 * jax.random.normal(k, shape, jnp.float32)

    x = jax.random.normal(ks[0], (BATCH, SEQ, D_MODEL), jnp.float32)
    offset = jax.random.randint(ks[1], (BATCH, 1), 0, 1024, dtype=jnp.int32)
    positions = jnp.arange(SEQ, dtype=jnp.int32)[None, :] + offset
    fgate_base = jnp.linspace(3.0, 6.0, MLSTM_HEADS, dtype=jnp.float32)
    return {
        "x": x,
        "positions": positions,
        "norm_mix_g": gain(ks[2], (DEPTH, D_MODEL)),
        "w_in": nrm(ks[3], (DEPTH, D_MODEL, IN_COLS), D_MODEL),
        "mla_q_norm_g": gain(ks[4], (DEPTH, MLA_Q_LORA)),
        "mla_w_uq": nrm(ks[5], (DEPTH, MLA_Q_LORA, MLA_HEADS * (MLA_NOPE + MLA_ROPE)), MLA_Q_LORA),
        "mla_kv_norm_g": gain(ks[6], (DEPTH, MLA_KV_LORA)),
        "mla_w_ukv": nrm(ks[7], (DEPTH, MLA_KV_LORA, MLA_HEADS * (MLA_NOPE + MLA_V)), MLA_KV_LORA),
        "mlstm_conv_w": nrm(ks[8], (DEPTH, CONV_WIDTH, 1, 2 * MLSTM_W), CONV_WIDTH),
        "mlstm_conv_b": 0.02 * jax.random.normal(ks[9], (DEPTH, 2 * MLSTM_W), jnp.float32),
        "mlstm_igate_b": 0.1 * jax.random.normal(ks[10], (DEPTH, 2, MLSTM_HEADS), jnp.float32),
        "mlstm_fgate_b": fgate_base + 0.1 * jax.random.normal(ks[11], (DEPTH, 2, MLSTM_HEADS), jnp.float32),
        "mlstm_out_norm_g": gain(ks[12], (DEPTH, MLSTM_W)),
        "w_branch_mla": nrm(ks[13], (DEPTH, MLA_W, D_MODEL), MLA_W),
        "w_branch_mlstm": nrm(ks[14], (DEPTH, MLSTM_W, D_MODEL), MLSTM_W),
        "w_out": nrm(ks[15], (DEPTH, D_MODEL, D_MODEL), D_MODEL),
        "norm_mlp_g": gain(ks[16], (DEPTH, D_MODEL)),
        "w_mlp_up": nrm(ks[17], (DEPTH, D_MODEL, D_FF), D_MODEL),
        "w_mlp_down": nrm(ks[18], (DEPTH, D_FF, D_MODEL), D_FF),
        "norm_final_g": gain(ks[19], (D_MODEL,)),
    }


def reference(x, positions, norm_mix_g, w_in, mla_q_norm_g, mla_w_uq, mla_kv_norm_g, mla_w_ukv,
              mlstm_conv_w, mlstm_conv_b, mlstm_igate_b, mlstm_fgate_b, mlstm_out_norm_g,
              w_branch_mla, w_branch_mlstm, w_out, norm_mlp_g, w_mlp_up, w_mlp_down, norm_final_g):
    cos, sin = rope_tables(positions)
    for l in range(DEPTH):
        h = rmsnorm(x, norm_mix_g[l])
        c_q, c_kv, k_rope, mq, mk, mv, mo, mgates, gate_a, gate_b = split_cols(h @ w_in[l])
        y_attn = mla_branch(c_q, c_kv, k_rope, cos, sin, mla_q_norm_g[l], mla_w_uq[l],
                            mla_kv_norm_g[l], mla_w_ukv[l])
        y_mlstm = mlstm_branch(mq, mk, mv, mo, mgates, mlstm_conv_w[l], mlstm_conv_b[l],
                               mlstm_igate_b[l], mlstm_fgate_b[l], mlstm_out_norm_g[l])
        merged = (jax.nn.sigmoid(gate_a) * (y_attn @ w_branch_mla[l])
                  + jax.nn.sigmoid(gate_b) * (y_mlstm @ w_branch_mlstm[l]))
        x = x + merged @ w_out[l]
        u = rmsnorm(x, norm_mlp_g[l]) @ w_mlp_up[l]
        x = x + jnp.square(jax.nn.relu(u)) @ w_mlp_down[l]
    return rmsnorm(x, norm_final_g)
```

```python
import functools
import math

import jax
import jax.numpy as jnp
from jax import lax
from jax.experimental import pallas as pl
from jax.experimental.pallas import tpu as pltpu

MLA_HEADS = 8
MLA_NOPE = 64
MLA_ROPE = 32
MLA_V = 64
ROPE_THETA = 10000.0
MLSTM_HEADS = 4
MLSTM_DH = 128
CONV_WIDTH = 5
NORM_EPS = 1e-6

LANES = 128
BF16_SUBLANES = 16
VMEM_LIMIT_BYTES = 56 * 1024 * 1024

HEAD_PAD = LANES
MLSTM_CHUNK = 256
HALO = BF16_SUBLANES

F32 = jnp.float32
BF16 = jnp.bfloat16


def _rms(x, g):
    return x * lax.rsqrt(jnp.mean(x * x, axis=-1, keepdims=True) + NORM_EPS) * g


def _dot(a, b):
    return jnp.dot(a, b, preferred_element_type=F32)


def _dot_nt(a, b):
    return lax.dot_general(a, b, (((1,), (1,)), ((), ())), preferred_element_type=F32)


def _dot_tn(a, b):
    return lax.dot_general(a, b, (((0,), (0,)), ((), ())), preferred_element_type=F32)


def _log_sigmoid(x):
    return jnp.minimum(x, 0.0) - jnp.log1p(jnp.exp(-jnp.abs(x)))


def _inproj_kernel(xm_ref, xp_ref, xn_ref, cos_ref, sin_ref, g_ref, w1_ref, w2_ref, w3_ref,
                   wgt_ref, gb_ref, cw_ref, cb_ref, qg_ref, wqa_ref, wqb_ref, kvg_ref,
                   wuk_ref, wuv_ref,
                   q_out, k_out, va_out, qm_out, km_out, vm_out, mo_out, gate_out,
                   pre_scr, *, tm, tiles_per_seq, q_lora, kv_lora, mw, q_scale, k_scale):
    i = pl.program_id(0)
    pos_in_seq = i % tiles_per_seq
    xp = jnp.where(pos_in_seq == 0, 0.0, xp_ref[...])
    xn = jnp.where(pos_in_seq == tiles_per_seq - 1, 0.0, xn_ref[...])
    xe = jnp.concatenate([xp, xm_ref[...], xn], axis=0)
    he = _rms(xe, g_ref[...]).astype(BF16)
    hm = he[HALO:HALO + tm]

    pre_scr[...] = _dot(he, w1_ref[...])
    conv = cb_ref[...]
    for j in range(CONV_WIDTH):
        conv = conv + cw_ref[j:j + 1, :] * pre_scr[pl.ds(HALO - CONV_WIDTH // 2 + j, tm), :]
    qk = conv * jax.nn.sigmoid(conv)
    qm_out[...] = qk[:, :mw].astype(BF16)
    km_out[...] = (qk[:, mw:] * k_scale).astype(BF16)

    vo = _dot(hm, w2_ref[...])
    vm_out[...] = vo[:, :mw].astype(BF16)
    mo_out[...] = jax.nn.sigmoid(vo[:, mw:]).astype(BF16)

    gt = _dot_nt(wgt_ref[...], hm) + gb_ref[...]
    row = lax.broadcasted_iota(jnp.int32, gt.shape, 0)
    is_f = (row % (2 * MLSTM_HEADS)) >= MLSTM_HEADS
    gate_out[...] = jnp.where(is_f, _log_sigmoid(gt), gt)

    c = _dot(hm, w3_ref[...])
    cqn = _rms(c[:, :q_lora], qg_ref[...]).astype(BF16)
    ckvn = _rms(c[:, q_lora:q_lora + kv_lora], kvg_ref[...]).astype(BF16)
    kra = c[:, q_lora + kv_lora:q_lora + kv_lora + HEAD_PAD]
    krb = c[:, q_lora + kv_lora + HEAD_PAD:]
    cos = cos_ref[...]
    sin = sin_ref[...]
    cos_h = jnp.concatenate([cos] * MLA_HEADS, axis=1)
    sin_h = jnp.concatenate([sin] * MLA_HEADS, axis=1)
    q = _dot(cqn, wqa_ref[...]) * cos_h + _dot(cqn, wqb_ref[...]) * sin_h
    q_out[...] = (q * q_scale).astype(BF16)
    kr = kra * cos + krb * sin
    k = _dot(ckvn, wuk_ref[...]) + jnp.concatenate([kr] * MLA_HEADS, axis=1)
    k_out[...] = k.astype(BF16)
    va_out[...] = _dot(ckvn, wuv_ref[...]).astype(BF16)


def _attn_kernel(q_ref, k_ref, v_ref, o_ref, *, tq, tk, nk):
    qs = [q_ref[:, hh * HEAD_PAD:(hh + 1) * HEAD_PAD] for hh in range(2)]

    def body(kc, carry):
        off = pl.multiple_of(kc * tk, tk)
        v_c = v_ref[pl.ds(off, tk), :]
        new = []
        for hh in range(2):
            m, l, acc = carry[hh]
            k_c = k_ref[pl.ds(off, tk), hh * HEAD_PAD:(hh + 1) * HEAD_PAD]
            s = _dot_nt(qs[hh], k_c)
            m_new = jnp.maximum(m, jnp.max(s, axis=-1, keepdims=True))
            alpha = jnp.exp2(m - m_new)
            p = jnp.exp2(s - m_new)
            l = alpha * l + jnp.sum(p, axis=-1, keepdims=True)
            acc = alpha * acc + _dot(p.astype(BF16), v_c)
            new.append((m_new, l, acc))
        return tuple(new)

    init = tuple((jnp.full((tq, 1), -jnp.inf, F32), jnp.zeros((tq, 1), F32),
                  jnp.zeros((tq, 2 * MLA_V), F32)) for _ in range(2))
    res = lax.fori_loop(0, nk, body, init)
    lane = lax.broadcasted_iota(jnp.int32, (tq, 2 * MLA_V), 1)
    o = jnp.where(lane < MLA_V, res[0][2] / res[0][1], res[1][2] / res[1][1])
    o_ref[...] = o.astype(BF16)


def _mlstm_kernel(q_ref, k_ref, v_ref, mo_ref, gate_ref, og_ref, o_ref, hacc, cn_scr, *, L, nc):
    dh = MLSTM_DH
    t_idx = lax.broadcasted_iota(jnp.int32, (L, L), 0)
    u_idx = lax.broadcasted_iota(jnp.int32, (L, L), 1)
    eye = t_idx == u_idx
    ones_col = (lax.broadcasted_iota(jnp.int32, (L, dh), 1) == 0).astype(F32)

    def chunk(c, m, direction):
        tri = (u_idx <= t_idx) if direction == 0 else (u_idx >= t_idx)
        tri_t = (t_idx <= u_idx) if direction == 0 else (t_idx >= u_idx)
        off = pl.multiple_of(c * L, L)
        q_c = q_ref[pl.ds(off, L), :]
        k_c = k_ref[pl.ds(off, L), :]
        v_c = v_ref[pl.ds(off, L), :]
        logi = gate_ref[direction, 0, 0, pl.ds(c, 1), :]
        logf = gate_ref[direction, 1, 0, pl.ds(c, 1), :]

        b_col = jnp.sum(jnp.where(tri, logf, 0.0), axis=1, keepdims=True)
        logf_col = jnp.sum(jnp.where(eye, logf, 0.0), axis=1, keepdims=True)
        logi_col = jnp.sum(jnp.where(eye, logi, 0.0), axis=1, keepdims=True)
        b_row = jnp.sum(jnp.where(tri_t, logf_col, 0.0), axis=0, keepdims=True)
        b_last = jnp.sum(logf, axis=1, keepdims=True)
        a_row = logi - b_row
        a_col = logi_col - b_col

        A = jnp.where(tri, a_row, -jnp.inf)
        g = jnp.maximum(jnp.max(A, axis=1, keepdims=True), m)
        E = jnp.exp(A - g)
        P = (_dot_nt(q_c, k_c) * E).astype(BF16)
        v_aug = jnp.concatenate([v_c.astype(F32), ones_col], axis=1)
        R = _dot(P, v_aug.astype(BF16))
        inter = _dot(q_c, cn_scr[direction].astype(BF16))
        iw = jnp.exp(m - g)
        num = iw * inter[:, :dh] + R[:, :dh]
        den = iw * inter[:, dh:dh + 1] + R[:, dh:dh + 1]
        h = num / jnp.maximum(jnp.abs(den), jnp.exp(-(b_col + g)))

        m_new = jnp.maximum(b_last + m, b_last + jnp.max(a_row, axis=1, keepdims=True))
        decay = jnp.exp(b_last + m - m_new)
        w_col = jnp.exp(b_last + a_col - m_new)
        U = _dot_tn(k_c, (v_aug * w_col).astype(BF16))
        cn_scr[direction] = decay * cn_scr[direction] + U
        return h, m_new

    cn_scr[...] = jnp.zeros_like(cn_scr)
    m0 = jnp.zeros((1, 1), F32)

    def fwd_body(j, m):
        h, m_new = chunk(j, m, 0)
        hacc[pl.ds(pl.multiple_of(j * L, L), L), :] = h
        return m_new

    lax.fori_loop(0, nc, fwd_body, m0)

    def bwd_body(j, m):
        c = nc - 1 - j
        h, m_new = chunk(c, m, 1)
        off = pl.multiple_of(c * L, L)
        hs = hacc[pl.ds(off, L), :] + h
        hn = _rms(hs, og_ref[...])
        o_ref[pl.ds(off, L), :] = (hn * mo_ref[pl.ds(off, L), :].astype(F32)).astype(BF16)
        return m_new

    lax.fori_loop(0, nc, bwd_body, m0)


def _merge_mlp_kernel(x_ref, ya_ref, ym_ref, g1_ref, wgab_ref, wbm_ref, wbl_ref, wout_ref,
                      g2_ref, wup_ref, wdn_ref, gf_ref, o_ref, *, d, final_norm):
    x = x_ref[...]
    hn = _rms(x, g1_ref[...]).astype(BF16)
    gates = jax.nn.sigmoid(_dot(hn, wgab_ref[...]))
    merged = (gates[:, :d] * _dot(ya_ref[...], wbm_ref[...])
              + gates[:, d:] * _dot(ym_ref[...], wbl_ref[...]))
    x1 = x + _dot(merged.astype(BF16), wout_ref[...])
    u = _dot(_rms(x1, g2_ref[...]).astype(BF16), wup_ref[...])
    r = jnp.maximum(u, 0.0)
    x2 = x1 + _dot((r * r).astype(BF16), wdn_ref[...])
    o_ref[...] = _rms(x2, gf_ref[...]) if final_norm else x2


def _const_spec(shape):
    return pl.BlockSpec(shape, lambda *_: (0,) * len(shape))


def _resident_spec(shape):
    return pl.BlockSpec(shape, lambda *_: (0,) * len(shape), pipeline_mode=pl.Buffered(1))


def _params(n_axes):
    return pltpu.CompilerParams(dimension_semantics=("arbitrary",) * n_axes,
                                vmem_limit_bytes=VMEM_LIMIT_BYTES)


def _rope_tables(positions):
    half = MLA_ROPE // 2
    inv_freq = ROPE_THETA ** (-jnp.arange(0, MLA_ROPE, 2, dtype=F32) / MLA_ROPE)
    ang = positions.astype(F32).reshape(-1, 1) * inv_freq
    cos, sin = jnp.cos(ang), jnp.sin(ang)
    t = ang.shape[0]
    pad = HEAD_PAD - MLA_NOPE - MLA_ROPE
    cos_t = jnp.concatenate([jnp.ones((t, MLA_NOPE), F32), cos, cos, jnp.zeros((t, pad), F32)], axis=1)
    sin_t = jnp.concatenate([jnp.zeros((t, MLA_NOPE), F32), -sin, sin, jnp.zeros((t, pad), F32)], axis=1)
    return cos_t, sin_t


def _pad_heads(w, width_in, offset_out):
    k = w.shape[0]
    w = w.reshape(k, MLA_HEADS, width_in)
    w = jnp.pad(w, ((0, 0), (0, 0), (offset_out, HEAD_PAD - width_in - offset_out)))
    return w.reshape(k, MLA_HEADS * HEAD_PAD)


def kernel(x, positions, norm_mix_g, w_in, mla_q_norm_g, mla_w_uq, mla_kv_norm_g, mla_w_ukv, mlstm_conv_w, mlstm_conv_b, mlstm_igate_b, mlstm_fgate_b, mlstm_out_norm_g, w_branch_mla, w_branch_mlstm, w_out, norm_mlp_g, w_mlp_up, w_mlp_down, norm_final_g):
    B, S, D = x.shape
    T = B * S
    depth = w_in.shape[0]
    q_lora = mla_q_norm_g.shape[1]
    kv_lora = mla_kv_norm_g.shape[1]
    mw = MLSTM_HEADS * MLSTM_DH
    half = MLA_ROPE // 2
    n_gate = 4 * MLSTM_HEADS
    d_ff = w_mlp_up.shape[2]
    L = MLSTM_CHUNK
    nc = S // L
    assert S % L == 0 and MLA_HEADS % 2 == 0

    cos_t, sin_t = _rope_tables(positions)
    xf = x.reshape(T, D)

    for l in range(depth):
        offs, o = [], 0
        for w in (q_lora, kv_lora, MLA_ROPE, mw, mw, mw, mw, n_gate, D, D):
            offs.append(o)
            o += w
        wl = w_in[l]
        w_cq = wl[:, offs[0]:offs[0] + q_lora]
        w_ckv = wl[:, offs[1]:offs[1] + kv_lora]
        w_kr = wl[:, offs[2]:offs[2] + MLA_ROPE]
        w_kr_swapped = jnp.concatenate([w_kr[:, half:], w_kr[:, :half]], axis=1)
        pad_kr = ((0, 0), (MLA_NOPE, HEAD_PAD - MLA_NOPE - MLA_ROPE))
        w1 = wl[:, offs[3]:offs[3] + 2 * mw].astype(BF16)
        w2 = wl[:, offs[5]:offs[5] + 2 * mw].astype(BF16)
        w3 = jnp.concatenate([w_cq, w_ckv, jnp.pad(w_kr, pad_kr), jnp.pad(w_kr_swapped, pad_kr)],
                             axis=1).astype(BF16)
        wgt = wl[:, offs[7]:offs[7] + n_gate].T.astype(BF16)
        wgab = wl[:, offs[8]:offs[8] + 2 * D].astype(BF16)
        gate_b = jnp.stack([mlstm_igate_b[l], mlstm_fgate_b[l]], axis=1).reshape(n_gate, 1)

        dqk = MLA_NOPE + MLA_ROPE
        wq = mla_w_uq[l].reshape(q_lora, MLA_HEADS, dqk)
        wq_rope = wq[:, :, MLA_NOPE:]
        wq_swapped = jnp.concatenate([jnp.zeros_like(wq[:, :, :MLA_NOPE]),
                                      wq_rope[:, :, half:], wq_rope[:, :, :half]], axis=2)
        wqa = _pad_heads(wq.reshape(q_lora, -1), dqk, 0).astype(BF16)
        wqb = _pad_heads(wq_swapped.reshape(q_lora, -1), dqk, 0).astype(BF16)
        wkv = mla_w_ukv[l].reshape(kv_lora, MLA_HEADS, MLA_NOPE + MLA_V)
        wuk = _pad_heads(wkv[:, :, :MLA_NOPE].reshape(kv_lora, -1), MLA_NOPE, 0).astype(BF16)
        wuv = wkv[:, :, MLA_NOPE:].reshape(kv_lora, MLA_HEADS * MLA_V).astype(BF16)

        cw = jnp.pad(mlstm_conv_w[l].reshape(CONV_WIDTH, 2 * mw), ((0, 8 - CONV_WIDTH), (0, 0)))
        cb = mlstm_conv_b[l].reshape(1, 2 * mw)

        tm = 256
        tiles_per_seq = S // tm
        hb = tm // HALO
        n_halo_blocks = T // HALO
        row = lambda i: (i, 0)
        kern = functools.partial(
            _inproj_kernel, tm=tm, tiles_per_seq=tiles_per_seq, q_lora=q_lora, kv_lora=kv_lora, mw=mw,
            q_scale=(MLA_NOPE + MLA_ROPE) ** -0.5 * math.log2(math.e), k_scale=MLSTM_DH ** -0.5)
        outs = pl.pallas_call(
            kern,
            grid=(T // tm,),
            in_specs=[
                pl.BlockSpec((tm, D), row),
                pl.BlockSpec((HALO, D), lambda i: (jnp.maximum(i * hb - 1, 0), 0)),
                pl.BlockSpec((HALO, D), lambda i: (jnp.minimum((i + 1) * hb, n_halo_blocks - 1), 0)),
                pl.BlockSpec((tm, HEAD_PAD), row),
                pl.BlockSpec((tm, HEAD_PAD), row),
                _const_spec((1, D)),
                _const_spec(w1.shape), _const_spec(w2.shape), _const_spec(w3.shape),
                _const_spec(wgt.shape), _const_spec(gate_b.shape),
                _const_spec(cw.shape), _const_spec(cb.shape),
                _const_spec((1, q_lora)), _const_spec(wqa.shape), _const_spec(wqb.shape),
                _const_spec((1, kv_lora)), _const_spec(wuk.shape), _const_spec(wuv.shape),
            ],
            out_specs=[
                pl.BlockSpec((tm, MLA_HEADS * HEAD_PAD), row),
                pl.BlockSpec((tm, MLA_HEADS * HEAD_PAD), row),
                pl.BlockSpec((tm, MLA_HEADS * MLA_V), row),
                pl.BlockSpec((tm, mw), row), pl.BlockSpec((tm, mw), row),
                pl.BlockSpec((tm, mw), row), pl.BlockSpec((tm, mw), row),
                pl.BlockSpec((n_gate, tm), lambda i: (0, i)),
            ],
            out_shape=[
                jax.ShapeDtypeStruct((T, MLA_HEADS * HEAD_PAD), BF16),
                jax.ShapeDtypeStruct((T, MLA_HEADS * HEAD_PAD), BF16),
                jax.ShapeDtypeStruct((T, MLA_HEADS * MLA_V), BF16),
                jax.ShapeDtypeStruct((T, mw), BF16), jax.ShapeDtypeStruct((T, mw), BF16),
                jax.ShapeDtypeStruct((T, mw), BF16), jax.ShapeDtypeStruct((T, mw), BF16),
                jax.ShapeDtypeStruct((n_gate, T), F32),
            ],
            scratch_shapes=[pltpu.VMEM((tm + 2 * HALO, 2 * mw), F32)],
            compiler_params=_params(1),
            name="inproj",
        )(xf, xf, xf, cos_t, sin_t, norm_mix_g[l].reshape(1, D), w1, w2, w3, wgt, gate_b, cw, cb,
          mla_q_norm_g[l].reshape(1, q_lora), wqa, wqb, mla_kv_norm_g[l].reshape(1, kv_lora), wuk, wuv)
        q_a, k_a, v_a, q_m, k_m, v_m, mo_s, gates = outs

        tq, tk = 256, 512
        nq = S // tq
        y_attn = pl.pallas_call(
            functools.partial(_attn_kernel, tq=tq, tk=tk, nk=S // tk),
            grid=(B, MLA_HEADS // 2, nq),
            in_specs=[
                pl.BlockSpec((tq, 2 * HEAD_PAD), lambda b, p, i: (b * nq + i, p)),
                pl.BlockSpec((S, 2 * HEAD_PAD), lambda b, p, i: (b, p)),
                pl.BlockSpec((S, 2 * MLA_V), lambda b, p, i: (b, p)),
            ],
            out_specs=pl.BlockSpec((tq, 2 * MLA_V), lambda b, p, i: (b * nq + i, p)),
            out_shape=jax.ShapeDtypeStruct((T, MLA_HEADS * MLA_V), BF16),
            compiler_params=_params(3),
            name="mla_attn",
        )(q_a, k_a, v_a)

        gates5 = gates.reshape(2, 2, MLSTM_HEADS, T // L, L)
        head_blk = lambda b, h: (b, h)
        y_mlstm = pl.pallas_call(
            functools.partial(_mlstm_kernel, L=L, nc=nc),
            grid=(B, MLSTM_HEADS),
            in_specs=[
                pl.BlockSpec((S, MLSTM_DH), head_blk), pl.BlockSpec((S, MLSTM_DH), head_blk),
                pl.BlockSpec((S, MLSTM_DH), head_blk), pl.BlockSpec((S, MLSTM_DH), head_blk),
                pl.BlockSpec((2, 2, 1, nc, L), lambda b, h: (0, 0, h, b, 0)),
                pl.BlockSpec((1, MLSTM_DH), lambda b, h: (0, h)),
            ],
            out_specs=pl.BlockSpec((S, MLSTM_DH), head_blk),
            out_shape=jax.ShapeDtypeStruct((T, mw), BF16),
            scratch_shapes=[pltpu.VMEM((S, MLSTM_DH), F32),
                            pltpu.VMEM((2, MLSTM_DH, 2 * MLSTM_DH), F32)],
            compiler_params=_params(2),
            name="mlstm",
        )(q_m, k_m, v_m, mo_s, gates5, mlstm_out_norm_g[l].reshape(1, mw))

        tm4 = 256
        last = l == depth - 1
        gf = norm_final_g.reshape(1, D)
        weights4 = (wgab, w_branch_mla[l].astype(BF16), w_branch_mlstm[l].astype(BF16),
                    w_out[l].astype(BF16), w_mlp_up[l].astype(BF16), w_mlp_down[l].astype(BF16))
        xf = pl.pallas_call(
            functools.partial(_merge_mlp_kernel, d=D, final_norm=last),
            grid=(T // tm4,),
            in_specs=[
                pl.BlockSpec((tm4, D), row),
                pl.BlockSpec((tm4, MLA_HEADS * MLA_V), row),
                pl.BlockSpec((tm4, mw), row),
                _const_spec((1, D)),
                _resident_spec(weights4[0].shape), _resident_spec(weights4[1].shape),
                _resident_spec(weights4[2].shape), _resident_spec(weights4[3].shape),
                _const_spec((1, D)),
                _resident_spec(weights4[4].shape), _resident_spec(weights4[5].shape),
                _const_spec((1, D)),
            ],
            out_specs=pl.BlockSpec((tm4, D), row),
            out_shape=jax.ShapeDtypeStruct((T, D), F32),
            compiler_params=_params(1),
            name="merge_mlp",
        )(xf, y_attn, y_mlstm, norm_mix_g[l].reshape(1, D), weights4[0], weights4[1], weights4[2],
          weights4[3], norm_mlp_g[l].reshape(1, D), weights4[4], weights4[5], gf)

    return xf.reshape(B, S, D)
```

```python
import functools
import math

import jax
import jax.numpy as jnp
from jax import lax
from jax.experimental import pallas as pl
from jax.experimental.pallas import tpu as pltpu

MLA_HEADS = 8
MLA_NOPE = 64
MLA_ROPE = 32
MLA_V = 64
ROPE_THETA = 10000.0
MLSTM_HEADS = 4
MLSTM_DH = 128
CONV_WIDTH = 5
NORM_EPS = 1e-6

LANES = 128
BF16_SUBLANES = 16
VMEM_LIMIT_BYTES = 56 * 1024 * 1024

HEAD_PAD = LANES
MLSTM_CHUNK = 256
HALO = BF16_SUBLANES

F32 = jnp.float32
BF16 = jnp.bfloat16


def _rms(x, g):
    return x * lax.rsqrt(jnp.mean(x * x, axis=-1, keepdims=True) + NORM_EPS) * g


def _dot(a, b):
    return jnp.dot(a, b, preferred_element_type=F32)


def _dot_nt(a, b):
    return lax.dot_general(a, b, (((1,), (1,)), ((), ())), preferred_element_type=F32)


def _dot_tn(a, b):
    return lax.dot_general(a, b, (((0,), (0,)), ((), ())), preferred_element_type=F32)


def _log_sigmoid(x):
    return jnp.minimum(x, 0.0) - jnp.log1p(jnp.exp(-jnp.abs(x)))


def _inproj_kernel(xm_ref, xp_ref, xn_ref, cos_ref, sin_ref, g_ref, w1_ref, w2_ref, w3_ref,
                   wgt_ref, gb_ref, cw_ref, cb_ref, qg_ref, wqa_ref, wqb_ref, kvg_ref,
                   wuk_ref, wuvt_ref,
                   q_out, k_out, va_out, qm_out, km_out, vm_out, mo_out, gate_out,
                   pre_scr, *, tm, tiles_per_seq, q_lora, kv_lora, mw, q_scale, k_scale):
    i = pl.program_id(0)
    pos_in_seq = i % tiles_per_seq
    xp = jnp.where(pos_in_seq == 0, 0.0, xp_ref[...])
    xn = jnp.where(pos_in_seq == tiles_per_seq - 1, 0.0, xn_ref[...])
    xe = jnp.concatenate([xp, xm_ref[...], xn], axis=0)
    he = _rms(xe, g_ref[...]).astype(BF16)
    hm = he[HALO:HALO + tm]

    pre_scr[...] = _dot(he, w1_ref[...])
    conv = cb_ref[...]
    for j in range(CONV_WIDTH):
        conv = conv + cw_ref[j:j + 1, :] * pre_scr[pl.ds(HALO - CONV_WIDTH // 2 + j, tm), :]
    qk = conv * jax.nn.sigmoid(conv)
    qm_out[...] = qk[:, :mw].astype(BF16)
    km_out[...] = (qk[:, mw:] * k_scale).astype(BF16)

    vo = _dot(hm, w2_ref[...])
    vm_out[...] = vo[:, :mw].astype(BF16)
    mo_out[...] = jax.nn.sigmoid(vo[:, mw:]).astype(BF16)

    gt = _dot_nt(wgt_ref[...], hm) + gb_ref[...]
    row = lax.broadcasted_iota(jnp.int32, gt.shape, 0)
    is_f = (row % (2 * MLSTM_HEADS)) >= MLSTM_HEADS
    gate_out[...] = jnp.where(is_f, _log_sigmoid(gt), gt)

    c = _dot(hm, w3_ref[...])
    cqn = _rms(c[:, :q_lora], qg_ref[...]).astype(BF16)
    ckvn = _rms(c[:, q_lora:q_lora + kv_lora], kvg_ref[...]).astype(BF16)
    kra = c[:, q_lora + kv_lora:q_lora + kv_lora + HEAD_PAD]
    krb = c[:, q_lora + kv_lora + HEAD_PAD:]
    cos = cos_ref[...]
    sin = sin_ref[...]
    cos_h = jnp.concatenate([cos] * MLA_HEADS, axis=1)
    sin_h = jnp.concatenate([sin] * MLA_HEADS, axis=1)
    q = _dot(cqn, wqa_ref[...]) * cos_h + _dot(cqn, wqb_ref[...]) * sin_h
    q_out[...] = (q * q_scale).astype(BF16)
    kr = kra * cos + krb * sin
    k = _dot(ckvn, wuk_ref[...]) + jnp.concatenate([kr] * MLA_HEADS, axis=1)
    k_out[...] = k.astype(BF16)
    va_out[0] = _dot_nt(wuvt_ref[...], ckvn).astype(BF16)


def _attn_kernel(q_ref, k_ref, vt_ref, o_ref, s_a, s_b, acc_scr, *, tq, tk, nk, unroll):
    qs = [q_ref[:, hh * HEAD_PAD:(hh + 1) * HEAD_PAD] for hh in range(2)]
    n_iter = nk // unroll
    assert n_iter % 2 == 0 and n_iter >= 2

    def scores(it, s_scr):
        for u in range(unroll):
            off = pl.multiple_of((it * unroll + u) * tk, tk)
            for hh in range(2):
                s_scr[2 * u + hh] = _dot_nt(
                    k_ref[pl.ds(off, tk), hh * HEAD_PAD:(hh + 1) * HEAD_PAD], qs[hh])

    def consume(it, s_scr, stats):
        stats = list(stats)
        for u in range(unroll):
            vt_c = vt_ref[it * unroll + u]
            for hh in range(2):
                m, l = stats[hh]
                s = s_scr[2 * u + hh]
                m_new = jnp.maximum(m, jnp.max(s, axis=0, keepdims=True))
                alpha = jnp.exp2(m - m_new)
                p = jnp.exp2(s - m_new)
                l = alpha * l + jnp.sum(p, axis=0, keepdims=True)
                acc_scr[hh] = alpha * acc_scr[hh] + _dot(vt_c, p.astype(BF16))
                stats[hh] = (m_new, l)
        return tuple(stats)

    def body(j, stats):
        scores(2 * j + 1, s_b)
        stats = consume(2 * j, s_a, stats)
        scores(2 * j + 2, s_a)
        return consume(2 * j + 1, s_b, stats)

    acc_scr[...] = jnp.zeros_like(acc_scr)
    init = tuple((jnp.full((1, tq), -jnp.inf, F32), jnp.zeros((1, tq), F32)) for _ in range(2))
    scores(0, s_a)
    stats = lax.fori_loop(0, n_iter // 2 - 1, body, init)
    scores(n_iter - 1, s_b)
    stats = consume(n_iter - 2, s_a, stats)
    stats = consume(n_iter - 1, s_b, stats)
    o_t = jnp.concatenate([acc_scr[0, :MLA_V] / stats[0][1], acc_scr[1, MLA_V:] / stats[1][1]], axis=0)
    o_ref[...] = o_t.T.astype(BF16)


def _mlstm_kernel(q_ref, k_ref, v_ref, mo_ref, gate_ref, og_ref, o_ref, hacc, cn_scr, *, L, nc):
    dh = MLSTM_DH
    t_idx = lax.broadcasted_iota(jnp.int32, (L, L), 0)
    u_idx = lax.broadcasted_iota(jnp.int32, (L, L), 1)
    eye = t_idx == u_idx
    ones_col = (lax.broadcasted_iota(jnp.int32, (L, dh), 1) == 0).astype(F32)

    def chunk(c, m, direction):
        tri = (u_idx <= t_idx) if direction == 0 else (u_idx >= t_idx)
        tri_t = (t_idx <= u_idx) if direction == 0 else (t_idx >= u_idx)
        off = pl.multiple_of(c * L, L)
        q_c = q_ref[pl.ds(off, L), :]
        k_c = k_ref[pl.ds(off, L), :]
        v_c = v_ref[pl.ds(off, L), :]
        logi = gate_ref[direction, 0, 0, pl.ds(c, 1), :]
        logf = gate_ref[direction, 1, 0, pl.ds(c, 1), :]

        b_col = jnp.sum(jnp.where(tri, logf, 0.0), axis=1, keepdims=True)
        logf_col = jnp.sum(jnp.where(eye, logf, 0.0), axis=1, keepdims=True)
        logi_col = jnp.sum(jnp.where(eye, logi, 0.0), axis=1, keepdims=True)
        b_row = jnp.sum(jnp.where(tri_t, logf_col, 0.0), axis=0, keepdims=True)
        b_last = jnp.sum(logf, axis=1, keepdims=True)
        a_row = logi - b_row
        a_col = logi_col - b_col

        A = jnp.where(tri, a_row, -jnp.inf)
        g = jnp.maximum(jnp.max(A, axis=1, keepdims=True), m)
        E = jnp.exp(A - g)
        P = (_dot_nt(q_c, k_c) * E).astype(BF16)
        v_aug = jnp.concatenate([v_c.astype(F32), ones_col], axis=1)
        R = _dot(P, v_aug.astype(BF16))
        inter = _dot(q_c, cn_scr[direction].astype(BF16))
        iw = jnp.exp(m - g)
        num = iw * inter[:, :dh] + R[:, :dh]
        den = iw * inter[:, dh:dh + 1] + R[:, dh:dh + 1]
        h = num / jnp.maximum(jnp.abs(den), jnp.exp(-(b_col + g)))

        m_new = jnp.maximum(b_last + m, b_last + jnp.max(a_row, axis=1, keepdims=True))
        decay = jnp.exp(b_last + m - m_new)
        w_col = jnp.exp(b_last + a_col - m_new)
        U = _dot_tn(k_c, (v_aug * w_col).astype(BF16))
        cn_scr[direction] = decay * cn_scr[direction] + U
        return h, m_new

    cn_scr[...] = jnp.zeros_like(cn_scr)
    m0 = jnp.zeros((1, 1), F32)

    def fwd_body(j, m):
        h, m_new = chunk(j, m, 0)
        hacc[pl.ds(pl.multiple_of(j * L, L), L), :] = h
        return m_new

    lax.fori_loop(0, nc, fwd_body, m0)

    def bwd_body(j, m):
        c = nc - 1 - j
        h, m_new = chunk(c, m, 1)
        off = pl.multiple_of(c * L, L)
        hs = hacc[pl.ds(off, L), :] + h
        hn = _rms(hs, og_ref[...])
        o_ref[pl.ds(off, L), :] = (hn * mo_ref[pl.ds(off, L), :].astype(F32)).astype(BF16)
        return m_new

    lax.fori_loop(0, nc, bwd_body, m0)


def _merge_mlp_kernel(x_ref, ya_ref, ym_ref, g1_ref, wgab_ref, wbm_ref, wbl_ref, wout_ref,
                      g2_ref, wup_ref, wdn_ref, gf_ref, o_ref, *, d, final_norm):
    x = x_ref[...]
    hn = _rms(x, g1_ref[...]).astype(BF16)
    gates = jax.nn.sigmoid(_dot(hn, wgab_ref[...]))
    merged = (gates[:, :d] * _dot(ya_ref[...], wbm_ref[...])
              + gates[:, d:] * _dot(ym_ref[...], wbl_ref[...]))
    x1 = x + _dot(merged.astype(BF16), wout_ref[...])
    u = _dot(_rms(x1, g2_ref[...]).astype(BF16), wup_ref[...])
    r = jnp.maximum(u, 0.0)
    x2 = x1 + _dot((r * r).astype(BF16), wdn_ref[...])
    o_ref[...] = _rms(x2, gf_ref[...]) if final_norm else x2


def _const_spec(shape):
    return pl.BlockSpec(shape, lambda *_: (0,) * len(shape))


def _resident_spec(shape):
    return pl.BlockSpec(shape, lambda *_: (0,) * len(shape), pipeline_mode=pl.Buffered(1))


def _params(n_axes):
    return pltpu.CompilerParams(dimension_semantics=("arbitrary",) * n_axes,
                                vmem_limit_bytes=VMEM_LIMIT_BYTES)


def _rope_tables(positions):
    half = MLA_ROPE // 2
    inv_freq = ROPE_THETA ** (-jnp.arange(0, MLA_ROPE, 2, dtype=F32) / MLA_ROPE)
    ang = positions.astype(F32).reshape(-1, 1) * inv_freq
    cos, sin = jnp.cos(ang), jnp.sin(ang)
    t = ang.shape[0]
    pad = HEAD_PAD - MLA_NOPE - MLA_ROPE
    cos_t = jnp.concatenate([jnp.ones((t, MLA_NOPE), F32), cos, cos, jnp.zeros((t, pad), F32)], axis=1)
    sin_t = jnp.concatenate([jnp.zeros((t, MLA_NOPE), F32), -sin, sin, jnp.zeros((t, pad), F32)], axis=1)
    return cos_t, sin_t


def _pad_heads(w, width_in, offset_out):
    k = w.shape[0]
    w = w.reshape(k, MLA_HEADS, width_in)
    w = jnp.pad(w, ((0, 0), (0, 0), (offset_out, HEAD_PAD - width_in - offset_out)))
    return w.reshape(k, MLA_HEADS * HEAD_PAD)


def kernel(x, positions, norm_mix_g, w_in, mla_q_norm_g, mla_w_uq, mla_kv_norm_g, mla_w_ukv, mlstm_conv_w, mlstm_conv_b, mlstm_igate_b, mlstm_fgate_b, mlstm_out_norm_g, w_branch_mla, w_branch_mlstm, w_out, norm_mlp_g, w_mlp_up, w_mlp_down, norm_final_g):
    B, S, D = x.shape
    T = B * S
    depth = w_in.shape[0]
    q_lora = mla_q_norm_g.shape[1]
    kv_lora = mla_kv_norm_g.shape[1]
    mw = MLSTM_HEADS * MLSTM_DH
    half = MLA_ROPE // 2
    n_gate = 4 * MLSTM_HEADS
    d_ff = w_mlp_up.shape[2]
    L = MLSTM_CHUNK
    nc = S // L
    assert S % L == 0 and MLA_HEADS % 2 == 0

    cos_t, sin_t = _rope_tables(positions)
    xf = x.reshape(T, D)

    for l in range(depth):
        offs, o = [], 0
        for w in (q_lora, kv_lora, MLA_ROPE, mw, mw, mw, mw, n_gate, D, D):
            offs.append(o)
            o += w
        wl = w_in[l]
        w_cq = wl[:, offs[0]:offs[0] + q_lora]
        w_ckv = wl[:, offs[1]:offs[1] + kv_lora]
        w_kr = wl[:, offs[2]:offs[2] + MLA_ROPE]
        w_kr_swapped = jnp.concatenate([w_kr[:, half:], w_kr[:, :half]], axis=1)
        pad_kr = ((0, 0), (MLA_NOPE, HEAD_PAD - MLA_NOPE - MLA_ROPE))
        w1 = wl[:, offs[3]:offs[3] + 2 * mw].astype(BF16)
        w2 = wl[:, offs[5]:offs[5] + 2 * mw].astype(BF16)
        w3 = jnp.concatenate([w_cq, w_ckv, jnp.pad(w_kr, pad_kr), jnp.pad(w_kr_swapped, pad_kr)],
                             axis=1).astype(BF16)
        wgt = wl[:, offs[7]:offs[7] + n_gate].T.astype(BF16)
        wgab = wl[:, offs[8]:offs[8] + 2 * D].astype(BF16)
        gate_b = jnp.stack([mlstm_igate_b[l], mlstm_fgate_b[l]], axis=1).reshape(n_gate, 1)

        dqk = MLA_NOPE + MLA_ROPE
        wq = mla_w_uq[l].reshape(q_lora, MLA_HEADS, dqk)
        wq_rope = wq[:, :, MLA_NOPE:]
        wq_swapped = jnp.concatenate([jnp.zeros_like(wq[:, :, :MLA_NOPE]),
                                      wq_rope[:, :, half:], wq_rope[:, :, :half]], axis=2)
        wqa = _pad_heads(wq.reshape(q_lora, -1), dqk, 0).astype(BF16)
        wqb = _pad_heads(wq_swapped.reshape(q_lora, -1), dqk, 0).astype(BF16)
        wkv = mla_w_ukv[l].reshape(kv_lora, MLA_HEADS, MLA_NOPE + MLA_V)
        wuk = _pad_heads(wkv[:, :, :MLA_NOPE].reshape(kv_lora, -1), MLA_NOPE, 0).astype(BF16)
        wuvt = wkv[:, :, MLA_NOPE:].reshape(kv_lora, MLA_HEADS * MLA_V).T.astype(BF16)

        cw = jnp.pad(mlstm_conv_w[l].reshape(CONV_WIDTH, 2 * mw), ((0, 8 - CONV_WIDTH), (0, 0)))
        cb = mlstm_conv_b[l].reshape(1, 2 * mw)

        tm = 256
        tiles_per_seq = S // tm
        hb = tm // HALO
        n_halo_blocks = T // HALO
        row = lambda i: (i, 0)
        kern = functools.partial(
            _inproj_kernel, tm=tm, tiles_per_seq=tiles_per_seq, q_lora=q_lora, kv_lora=kv_lora, mw=mw,
            q_scale=(MLA_NOPE + MLA_ROPE) ** -0.5 * math.log2(math.e), k_scale=MLSTM_DH ** -0.5)
        outs = pl.pallas_call(
            kern,
            grid=(T // tm,),
            in_specs=[
                pl.BlockSpec((tm, D), row),
                pl.BlockSpec((HALO, D), lambda i: (jnp.maximum(i * hb - 1, 0), 0)),
                pl.BlockSpec((HALO, D), lambda i: (jnp.minimum((i + 1) * hb, n_halo_blocks - 1), 0)),
                pl.BlockSpec((tm, HEAD_PAD), row),
                pl.BlockSpec((tm, HEAD_PAD), row),
                _const_spec((1, D)),
                _const_spec(w1.shape), _const_spec(w2.shape), _const_spec(w3.shape),
                _const_spec(wgt.shape), _const_spec(gate_b.shape),
                _const_spec(cw.shape), _const_spec(cb.shape),
                _const_spec((1, q_lora)), _const_spec(wqa.shape), _const_spec(wqb.shape),
                _const_spec((1, kv_lora)), _const_spec(wuk.shape), _const_spec(wuvt.shape),
            ],
            out_specs=[
                pl.BlockSpec((tm, MLA_HEADS * HEAD_PAD), row),
                pl.BlockSpec((tm, MLA_HEADS * HEAD_PAD), row),
                pl.BlockSpec((1, MLA_HEADS * MLA_V, tm), lambda i: (i, 0, 0)),
                pl.BlockSpec((tm, mw), row), pl.BlockSpec((tm, mw), row),
                pl.BlockSpec((tm, mw), row), pl.BlockSpec((tm, mw), row),
                pl.BlockSpec((n_gate, tm), lambda i: (0, i)),
            ],
            out_shape=[
                jax.ShapeDtypeStruct((T, MLA_HEADS * HEAD_PAD), BF16),
                jax.ShapeDtypeStruct((T, MLA_HEADS * HEAD_PAD), BF16),
                jax.ShapeDtypeStruct((T // tm, MLA_HEADS * MLA_V, tm), BF16),
                jax.ShapeDtypeStruct((T, mw), BF16), jax.ShapeDtypeStruct((T, mw), BF16),
                jax.ShapeDtypeStruct((T, mw), BF16), jax.ShapeDtypeStruct((T, mw), BF16),
                jax.ShapeDtypeStruct((n_gate, T), F32),
            ],
            scratch_shapes=[pltpu.VMEM((tm + 2 * HALO, 2 * mw), F32)],
            compiler_params=_params(1),
            name="inproj",
        )(xf, xf, xf, cos_t, sin_t, norm_mix_g[l].reshape(1, D), w1, w2, w3, wgt, gate_b, cw, cb,
          mla_q_norm_g[l].reshape(1, q_lora), wqa, wqb, mla_kv_norm_g[l].reshape(1, kv_lora), wuk, wuvt)
        q_a, k_a, v_a, q_m, k_m, v_m, mo_s, gates = outs

        tq, tk = 256, tm
        nq, nk = S // tq, S // tk
        unroll = 1
        y_attn = pl.pallas_call(
            functools.partial(_attn_kernel, tq=tq, tk=tk, nk=nk, unroll=unroll),
            grid=(B, MLA_HEADS // 2, nq),
            in_specs=[
                pl.BlockSpec((tq, 2 * HEAD_PAD), lambda b, p, i: (b * nq + i, p)),
                pl.BlockSpec((S, 2 * HEAD_PAD), lambda b, p, i: (b, p)),
                pl.BlockSpec((nk, 2 * MLA_V, tk), lambda b, p, i: (b, p, 0)),
            ],
            out_specs=pl.BlockSpec((tq, 2 * MLA_V), lambda b, p, i: (b * nq + i, p)),
            out_shape=jax.ShapeDtypeStruct((T, MLA_HEADS * MLA_V), BF16),
            scratch_shapes=[pltpu.VMEM((2 * unroll, tk, tq), F32),
                            pltpu.VMEM((2 * unroll, tk, tq), F32),
                            pltpu.VMEM((2, 2 * MLA_V, tq), F32)],
            compiler_params=_params(3),
            name="mla_attn",
        )(q_a, k_a, v_a)

        gates5 = gates.reshape(2, 2, MLSTM_HEADS, T // L, L)
        head_blk = lambda b, h: (b, h)
        y_mlstm = pl.pallas_call(
            functools.partial(_mlstm_kernel, L=L, nc=nc),
            grid=(B, MLSTM_HEADS),
            in_specs=[
                pl.BlockSpec((S, MLSTM_DH), head_blk), pl.BlockSpec((S, MLSTM_DH), head_blk),
                pl.BlockSpec((S, MLSTM_DH), head_blk), pl.BlockSpec((S, MLSTM_DH), head_blk),
                pl.BlockSpec((2, 2, 1, nc, L), lambda b, h: (0, 0, h, b, 0)),
                pl.BlockSpec((1, MLSTM_DH), lambda b, h: (0, h)),
            ],
            out_specs=pl.BlockSpec((S, MLSTM_DH), head_blk),
            out_shape=jax.ShapeDtypeStruct((T, mw), BF16),
            scratch_shapes=[pltpu.VMEM((S, MLSTM_DH), F32),
                            pltpu.VMEM((2, MLSTM_DH, 2 * MLSTM_DH), F32)],
            compiler_params=_params(2),
            name="mlstm",
        )(q_m, k_m, v_m, mo_s, gates5, mlstm_out_norm_g[l].reshape(1, mw))

        tm4 = 256
        last = l == depth - 1
        gf = norm_final_g.reshape(1, D)
        weights4 = (wgab, w_branch_mla[l].astype(BF16), w_branch_mlstm[l].astype(BF16),
                    w_out[l].astype(BF16), w_mlp_up[l].astype(BF16), w_mlp_down[l].astype(BF16))
        xf = pl.pallas_call(
            functools.partial(_merge_mlp_kernel, d=D, final_norm=last),
            grid=(T // tm4,),
            in_specs=[
                pl.BlockSpec((tm4, D), row),
                pl.BlockSpec((tm4, MLA_HEADS * MLA_V), row),
                pl.BlockSpec((tm4, mw), row),
                _const_spec((1, D)),
                _resident_spec(weights4[0].shape), _resident_spec(weights4[1].shape),
                _resident_spec(weights4[2].shape), _resident_spec(weights4[3].shape),
                _const_spec((1, D)),
                _resident_spec(weights4[4].shape), _resident_spec(weights4[5].shape),
                _const_spec((1, D)),
            ],
            out_specs=pl.BlockSpec((tm4, D), row),
            out_shape=jax.ShapeDtypeStruct((T, D), F32),
            compiler_params=_params(1),
            name="merge_mlp",
        )(xf, y_attn, y_mlstm, norm_mix_g[l].reshape(1, D), weights4[0], weights4[1], weights4[2],
          weights4[3], norm_mlp_g[l].reshape(1, D), weights4[4], weights4[5], gf)

    return xf.reshape(B, S, D)
```

```python
import functools
import math

import jax
import jax.numpy as jnp
from jax import lax
from jax.experimental import pallas as pl
from jax.experimental.pallas import tpu as pltpu

MLA_HEADS = 8
MLA_NOPE = 64
MLA_ROPE = 32
MLA_V = 64
ROPE_THETA = 10000.0
MLSTM_HEADS = 4
MLSTM_DH = 128
CONV_WIDTH = 5
NORM_EPS = 1e-6

LANES = 128
BF16_SUBLANES = 16
VMEM_LIMIT_BYTES = 56 * 1024 * 1024

HEAD_PAD = LANES
MLSTM_CHUNK = 256
HALO = BF16_SUBLANES

F32 = jnp.float32
BF16 = jnp.bfloat16


def _rms(x, g):
    return x * lax.rsqrt(jnp.mean(x * x, axis=-1, keepdims=True) + NORM_EPS) * g


def _dot(a, b):
    return jnp.dot(a, b, preferred_element_type=F32)


def _dot_nt(a, b):
    return lax.dot_general(a, b, (((1,), (1,)), ((), ())), preferred_element_type=F32)


def _dot_tn(a, b):
    return lax.dot_general(a, b, (((0,), (0,)), ((), ())), preferred_element_type=F32)


def _log_sigmoid(x):
    return jnp.minimum(x, 0.0) - jnp.log1p(jnp.exp(-jnp.abs(x)))


def _inproj_kernel(xm_ref, xp_ref, xn_ref, cos_ref, sin_ref, g_ref, w1_ref, w2_ref, w3_ref,
                   wgt_ref, gb_ref, cw_ref, cb_ref, qg_ref, wqa_ref, wqb_ref, kvg_ref,
                   wuk_ref, wuvt_ref,
                   q_out, k_out, va_out, qm_out, km_out, vm_out, mo_out, gate_out,
                   pre_scr, *, tm, tiles_per_seq, q_lora, kv_lora, mw, q_scale, k_scale):
    i = pl.program_id(0)
    pos_in_seq = i % tiles_per_seq
    xp = jnp.where(pos_in_seq == 0, 0.0, xp_ref[...])
    xn = jnp.where(pos_in_seq == tiles_per_seq - 1, 0.0, xn_ref[...])
    xe = jnp.concatenate([xp, xm_ref[...], xn], axis=0)
    he = _rms(xe, g_ref[...]).astype(BF16)
    hm = he[HALO:HALO + tm]

    pre_scr[...] = _dot(he, w1_ref[...])
    conv = cb_ref[...]
    for j in range(CONV_WIDTH):
        conv = conv + cw_ref[j:j + 1, :] * pre_scr[pl.ds(HALO - CONV_WIDTH // 2 + j, tm), :]
    qk = conv * jax.nn.sigmoid(conv)
    qm_out[...] = qk[:, :mw].astype(BF16)
    km_out[...] = (qk[:, mw:] * k_scale).astype(BF16)

    vo = _dot(hm, w2_ref[...])
    vm_out[...] = vo[:, :mw].astype(BF16)
    mo_out[...] = jax.nn.sigmoid(vo[:, mw:]).astype(BF16)

    gt = _dot_nt(wgt_ref[...], hm) + gb_ref[...]
    row = lax.broadcasted_iota(jnp.int32, gt.shape, 0)
    is_f = (row % (2 * MLSTM_HEADS)) >= MLSTM_HEADS
    gate_out[...] = jnp.where(is_f, _log_sigmoid(gt), gt)

    c = _dot(hm, w3_ref[...])
    cqn = _rms(c[:, :q_lora], qg_ref[...]).astype(BF16)
    ckvn = _rms(c[:, q_lora:q_lora + kv_lora], kvg_ref[...]).astype(BF16)
    kra = c[:, q_lora + kv_lora:q_lora + kv_lora + HEAD_PAD]
    krb = c[:, q_lora + kv_lora + HEAD_PAD:]
    cos = cos_ref[...]
    sin = sin_ref[...]
    cos_h = jnp.concatenate([cos] * MLA_HEADS, axis=1)
    sin_h = jnp.concatenate([sin] * MLA_HEADS, axis=1)
    q = _dot(cqn, wqa_ref[...]) * cos_h + _dot(cqn, wqb_ref[...]) * sin_h
    q_out[...] = (q * q_scale).astype(BF16)
    kr = kra * cos + krb * sin
    k = _dot(ckvn, wuk_ref[...]) + jnp.concatenate([kr] * MLA_HEADS, axis=1)
    k_out[...] = k.astype(BF16)
    vt = _dot_nt(wuvt_ref[...], ckvn)
    ones = jnp.ones((HEAD_PAD - MLA_V, tm), F32)
    pieces = []
    for h in range(MLA_HEADS):
        pieces += [vt[h * MLA_V:(h + 1) * MLA_V], ones]
    va_out[0] = jnp.concatenate(pieces, axis=0).astype(BF16)


def _attn_kernel(q_ref, k_ref, vt_ref, o_ref, s_0, s_1, p_0, p_1, acc_scr, *, tq, tk, vchunk, n_steps):
    qs = [q_ref[:, hh * HEAD_PAD:(hh + 1) * HEAD_PAD] for hh in range(2)]
    s_buf, p_buf = (s_0, s_1), (p_0, p_1)
    sub = tk // vchunk
    assert n_steps % 2 == 0 and n_steps >= 4

    def scores(j, par):
        off = pl.multiple_of(j * tk, tk)
        for hh in range(2):
            s_buf[par][hh] = _dot_nt(k_ref[pl.ds(off, tk), hh * HEAD_PAD:(hh + 1) * HEAD_PAD], qs[hh])

    def pv_issue(j, par):
        out = []
        for hh in range(2):
            vt = jnp.concatenate([vt_ref[j * sub + c, hh * HEAD_PAD:(hh + 1) * HEAD_PAD, :]
                                  for c in range(sub)], axis=1)
            out.append(_dot(vt, p_buf[par][hh]))
        return out

    def pv_accumulate(r, alpha):
        for hh in range(2):
            acc_scr[hh] = alpha[hh] * acc_scr[hh] + r[hh]

    def softmax(par, m):
        m_out, alpha = [], []
        for hh in range(2):
            s = s_buf[par][hh]
            m_new = jnp.maximum(m[hh], jnp.max(s, axis=0, keepdims=True))
            p_buf[par][hh] = jnp.exp2(s - m_new).astype(BF16)
            alpha.append(jnp.exp2(m[hh] - m_new))
            m_out.append(m_new)
        return tuple(m_out), tuple(alpha)

    def body(k, par, m, alpha):
        r = pv_issue(k, par)
        scores(k + 2, par)
        m, alpha_next = softmax(1 - par, m)
        pv_accumulate(r, alpha)
        return m, alpha_next

    def double_body(kk, carry):
        m, alpha = carry
        m, alpha = body(2 * kk, 0, m, alpha)
        return body(2 * kk + 1, 1, m, alpha)

    acc_scr[...] = jnp.zeros_like(acc_scr)
    m = tuple(jnp.full((1, tq), -jnp.inf, F32) for _ in range(2))
    scores(0, 0)
    scores(1, 1)
    m, alpha = softmax(0, m)
    m, alpha = lax.fori_loop(0, (n_steps - 2) // 2, double_body, (m, alpha))
    r = pv_issue(n_steps - 2, 0)
    m, alpha_last = softmax(1, m)
    pv_accumulate(r, alpha)
    pv_accumulate(pv_issue(n_steps - 1, 1), alpha_last)
    o_t = jnp.concatenate([acc_scr[hh, :MLA_V] / acc_scr[hh, MLA_V:MLA_V + 1] for hh in range(2)], axis=0)
    o_ref[...] = o_t.T.astype(BF16)


def _mlstm_kernel(q_ref, k_ref, v_ref, mo_ref, gate_ref, og_ref, o_ref, hacc, cn_scr, *, L, nc):
    dh = MLSTM_DH
    t_idx = lax.broadcasted_iota(jnp.int32, (L, L), 0)
    u_idx = lax.broadcasted_iota(jnp.int32, (L, L), 1)
    eye = t_idx == u_idx
    ones_col = (lax.broadcasted_iota(jnp.int32, (L, dh), 1) == 0).astype(F32)

    def chunk(c, m, direction):
        tri = (u_idx <= t_idx) if direction == 0 else (u_idx >= t_idx)
        tri_t = (t_idx <= u_idx) if direction == 0 else (t_idx >= u_idx)
        off = pl.multiple_of(c * L, L)
        q_c = q_ref[pl.ds(off, L), :]
        k_c = k_ref[pl.ds(off, L), :]
        v_c = v_ref[pl.ds(off, L), :]
        logi = gate_ref[direction, 0, 0, pl.ds(c, 1), :]
        logf = gate_ref[direction, 1, 0, pl.ds(c, 1), :]

        b_col = jnp.sum(jnp.where(tri, logf, 0.0), axis=1, keepdims=True)
        logf_col = jnp.sum(jnp.where(eye, logf, 0.0), axis=1, keepdims=True)
        logi_col = jnp.sum(jnp.where(eye, logi, 0.0), axis=1, keepdims=True)
        b_row = jnp.sum(jnp.where(tri_t, logf_col, 0.0), axis=0, keepdims=True)
        b_last = jnp.sum(logf, axis=1, keepdims=True)
        a_row = logi - b_row
        a_col = logi_col - b_col

        A = jnp.where(tri, a_row, -jnp.inf)
        g = jnp.maximum(jnp.max(A, axis=1, keepdims=True), m)
        E = jnp.exp(A - g)
        P = (_dot_nt(q_c, k_c) * E).astype(BF16)
        v_aug = jnp.concatenate([v_c.astype(F32), ones_col], axis=1)
        R = _dot(P, v_aug.astype(BF16))
        inter = _dot(q_c, cn_scr[direction].astype(BF16))
        iw = jnp.exp(m - g)
        num = iw * inter[:, :dh] + R[:, :dh]
        den = iw * inter[:, dh:dh + 1] + R[:, dh:dh + 1]
        h = num / jnp.maximum(jnp.abs(den), jnp.exp(-(b_col + g)))

        m_new = jnp.maximum(b_last + m, b_last + jnp.max(a_row, axis=1, keepdims=True))
        decay = jnp.exp(b_last + m - m_new)
        w_col = jnp.exp(b_last + a_col - m_new)
        U = _dot_tn(k_c, (v_aug * w_col).astype(BF16))
        cn_scr[direction] = decay * cn_scr[direction] + U
        return h, m_new

    cn_scr[...] = jnp.zeros_like(cn_scr)
    m0 = jnp.zeros((1, 1), F32)

    def fwd_body(j, m):
        h, m_new = chunk(j, m, 0)
        hacc[pl.ds(pl.multiple_of(j * L, L), L), :] = h
        return m_new

    lax.fori_loop(0, nc, fwd_body, m0)

    def bwd_body(j, m):
        c = nc - 1 - j
        h, m_new = chunk(c, m, 1)
        off = pl.multiple_of(c * L, L)
        hs = hacc[pl.ds(off, L), :] + h
        hn = _rms(hs, og_ref[...])
        o_ref[pl.ds(off, L), :] = (hn * mo_ref[pl.ds(off, L), :].astype(F32)).astype(BF16)
        return m_new

    lax.fori_loop(0, nc, bwd_body, m0)


def _merge_mlp_kernel(x_ref, ya_ref, ym_ref, g1_ref, wgab_ref, wbm_ref, wbl_ref, wout_ref,
                      g2_ref, wup_ref, wdn_ref, gf_ref, o_ref, *, d, final_norm):
    x = x_ref[...]
    hn = _rms(x, g1_ref[...]).astype(BF16)
    gates = jax.nn.sigmoid(_dot(hn, wgab_ref[...]))
    merged = (gates[:, :d] * _dot(ya_ref[...], wbm_ref[...])
              + gates[:, d:] * _dot(ym_ref[...], wbl_ref[...]))
    x1 = x + _dot(merged.astype(BF16), wout_ref[...])
    u = _dot(_rms(x1, g2_ref[...]).astype(BF16), wup_ref[...])
    r = jnp.maximum(u, 0.0)
    x2 = x1 + _dot((r * r).astype(BF16), wdn_ref[...])
    o_ref[...] = _rms(x2, gf_ref[...]) if final_norm else x2


def _const_spec(shape):
    return pl.BlockSpec(shape, lambda *_: (0,) * len(shape))


def _resident_spec(shape):
    return pl.BlockSpec(shape, lambda *_: (0,) * len(shape), pipeline_mode=pl.Buffered(1))


def _params(n_axes):
    return pltpu.CompilerParams(dimension_semantics=("arbitrary",) * n_axes,
                                vmem_limit_bytes=VMEM_LIMIT_BYTES)


def _rope_tables(positions):
    half = MLA_ROPE // 2
    inv_freq = ROPE_THETA ** (-jnp.arange(0, MLA_ROPE, 2, dtype=F32) / MLA_ROPE)
    ang = positions.astype(F32).reshape(-1, 1) * inv_freq
    cos, sin = jnp.cos(ang), jnp.sin(ang)
    t = ang.shape[0]
    pad = HEAD_PAD - MLA_NOPE - MLA_ROPE
    cos_t = jnp.concatenate([jnp.ones((t, MLA_NOPE), F32), cos, cos, jnp.zeros((t, pad), F32)], axis=1)
    sin_t = jnp.concatenate([jnp.zeros((t, MLA_NOPE), F32), -sin, sin, jnp.zeros((t, pad), F32)], axis=1)
    return cos_t, sin_t


def _pad_heads(w, width_in, offset_out):
    k = w.shape[0]
    w = w.reshape(k, MLA_HEADS, width_in)
    w = jnp.pad(w, ((0, 0), (0, 0), (offset_out, HEAD_PAD - width_in - offset_out)))
    return w.reshape(k, MLA_HEADS * HEAD_PAD)


def kernel(x, positions, norm_mix_g, w_in, mla_q_norm_g, mla_w_uq, mla_kv_norm_g, mla_w_ukv, mlstm_conv_w, mlstm_conv_b, mlstm_igate_b, mlstm_fgate_b, mlstm_out_norm_g, w_branch_mla, w_branch_mlstm, w_out, norm_mlp_g, w_mlp_up, w_mlp_down, norm_final_g):
    B, S, D = x.shape
    T = B * S
    depth = w_in.shape[0]
    q_lora = mla_q_norm_g.shape[1]
    kv_lora = mla_kv_norm_g.shape[1]
    mw = MLSTM_HEADS * MLSTM_DH
    half = MLA_ROPE // 2
    n_gate = 4 * MLSTM_HEADS
    d_ff = w_mlp_up.shape[2]
    L = MLSTM_CHUNK
    nc = S // L
    assert S % L == 0 and MLA_HEADS % 2 == 0

    cos_t, sin_t = _rope_tables(positions)
    xf = x.reshape(T, D)

    for l in range(depth):
        offs, o = [], 0
        for w in (q_lora, kv_lora, MLA_ROPE, mw, mw, mw, mw, n_gate, D, D):
            offs.append(o)
            o += w
        wl = w_in[l]
        w_cq = wl[:, offs[0]:offs[0] + q_lora]
        w_ckv = wl[:, offs[1]:offs[1] + kv_lora]
        w_kr = wl[:, offs[2]:offs[2] + MLA_ROPE]
        w_kr_swapped = jnp.concatenate([w_kr[:, half:], w_kr[:, :half]], axis=1)
        pad_kr = ((0, 0), (MLA_NOPE, HEAD_PAD - MLA_NOPE - MLA_ROPE))
        w1 = wl[:, offs[3]:offs[3] + 2 * mw].astype(BF16)
        w2 = wl[:, offs[5]:offs[5] + 2 * mw].astype(BF16)
        w3 = jnp.concatenate([w_cq, w_ckv, jnp.pad(w_kr, pad_kr), jnp.pad(w_kr_swapped, pad_kr)],
                             axis=1).astype(BF16)
        wgt = wl[:, offs[7]:offs[7] + n_gate].T.astype(BF16)
        wgab = wl[:, offs[8]:offs[8] + 2 * D].astype(BF16)
        gate_b = jnp.stack([mlstm_igate_b[l], mlstm_fgate_b[l]], axis=1).reshape(n_gate, 1)

        dqk = MLA_NOPE + MLA_ROPE
        wq = mla_w_uq[l].reshape(q_lora, MLA_HEADS, dqk)
        wq_rope = wq[:, :, MLA_NOPE:]
        wq_swapped = jnp.concatenate([jnp.zeros_like(wq[:, :, :MLA_NOPE]),
                                      wq_rope[:, :, half:], wq_rope[:, :, :half]], axis=2)
        wqa = _pad_heads(wq.reshape(q_lora, -1), dqk, 0).astype(BF16)
        wqb = _pad_heads(wq_swapped.reshape(q_lora, -1), dqk, 0).astype(BF16)
        wkv = mla_w_ukv[l].reshape(kv_lora, MLA_HEADS, MLA_NOPE + MLA_V)
        wuk = _pad_heads(wkv[:, :, :MLA_NOPE].reshape(kv_lora, -1), MLA_NOPE, 0).astype(BF16)
        wuvt = wkv[:, :, MLA_NOPE:].reshape(kv_lora, MLA_HEADS * MLA_V).T.astype(BF16)

        cw = jnp.pad(mlstm_conv_w[l].reshape(CONV_WIDTH, 2 * mw), ((0, 8 - CONV_WIDTH), (0, 0)))
        cb = mlstm_conv_b[l].reshape(1, 2 * mw)

        tm = 256
        tiles_per_seq = S // tm
        hb = tm // HALO
        n_halo_blocks = T // HALO
        row = lambda i: (i, 0)
        kern = functools.partial(
            _inproj_kernel, tm=tm, tiles_per_seq=tiles_per_seq, q_lora=q_lora, kv_lora=kv_lora, mw=mw,
            q_scale=(MLA_NOPE + MLA_ROPE) ** -0.5 * math.log2(math.e), k_scale=MLSTM_DH ** -0.5)
        outs = pl.pallas_call(
            kern,
            grid=(T // tm,),
            in_specs=[
                pl.BlockSpec((tm, D), row),
                pl.BlockSpec((HALO, D), lambda i: (jnp.maximum(i * hb - 1, 0), 0)),
                pl.BlockSpec((HALO, D), lambda i: (jnp.minimum((i + 1) * hb, n_halo_blocks - 1), 0)),
                pl.BlockSpec((tm, HEAD_PAD), row),
                pl.BlockSpec((tm, HEAD_PAD), row),
                _const_spec((1, D)),
                _const_spec(w1.shape), _const_spec(w2.shape), _const_spec(w3.shape),
                _const_spec(wgt.shape), _const_spec(gate_b.shape),
                _const_spec(cw.shape), _const_spec(cb.shape),
                _const_spec((1, q_lora)), _const_spec(wqa.shape), _const_spec(wqb.shape),
                _const_spec((1, kv_lora)), _const_spec(wuk.shape), _const_spec(wuvt.shape),
            ],
            out_specs=[
                pl.BlockSpec((tm, MLA_HEADS * HEAD_PAD), row),
                pl.BlockSpec((tm, MLA_HEADS * HEAD_PAD), row),
                pl.BlockSpec((1, MLA_HEADS * HEAD_PAD, tm), lambda i: (i, 0, 0)),
                pl.BlockSpec((tm, mw), row), pl.BlockSpec((tm, mw), row),
                pl.BlockSpec((tm, mw), row), pl.BlockSpec((tm, mw), row),
                pl.BlockSpec((n_gate, tm), lambda i: (0, i)),
            ],
            out_shape=[
                jax.ShapeDtypeStruct((T, MLA_HEADS * HEAD_PAD), BF16),
                jax.ShapeDtypeStruct((T, MLA_HEADS * HEAD_PAD), BF16),
                jax.ShapeDtypeStruct((T // tm, MLA_HEADS * HEAD_PAD, tm), BF16),
                jax.ShapeDtypeStruct((T, mw), BF16), jax.ShapeDtypeStruct((T, mw), BF16),
                jax.ShapeDtypeStruct((T, mw), BF16), jax.ShapeDtypeStruct((T, mw), BF16),
                jax.ShapeDtypeStruct((n_gate, T), F32),
            ],
            scratch_shapes=[pltpu.VMEM((tm + 2 * HALO, 2 * mw), F32)],
            compiler_params=_params(1),
            name="inproj",
        )(xf, xf, xf, cos_t, sin_t, norm_mix_g[l].reshape(1, D), w1, w2, w3, wgt, gate_b, cw, cb,
          mla_q_norm_g[l].reshape(1, q_lora), wqa, wqb, mla_kv_norm_g[l].reshape(1, kv_lora), wuk, wuvt)
        q_a, k_a, v_a, q_m, k_m, v_m, mo_s, gates = outs

        tq, tk = 256, 512
        vchunk = tm
        nq = S // tq
        y_attn = pl.pallas_call(
            functools.partial(_attn_kernel, tq=tq, tk=tk, vchunk=vchunk, n_steps=S // tk),
            grid=(B, MLA_HEADS // 2, nq),
            in_specs=[
                pl.BlockSpec((tq, 2 * HEAD_PAD), lambda b, p, i: (b * nq + i, p)),
                pl.BlockSpec((S, 2 * HEAD_PAD), lambda b, p, i: (b, p)),
                pl.BlockSpec((S // vchunk, 2 * HEAD_PAD, vchunk), lambda b, p, i: (b, p, 0)),
            ],
            out_specs=pl.BlockSpec((tq, 2 * MLA_V), lambda b, p, i: (b * nq + i, p)),
            out_shape=jax.ShapeDtypeStruct((T, MLA_HEADS * MLA_V), BF16),
            scratch_shapes=[pltpu.VMEM((2, tk, tq), F32), pltpu.VMEM((2, tk, tq), F32),
                            pltpu.VMEM((2, tk, tq), BF16), pltpu.VMEM((2, tk, tq), BF16),
                            pltpu.VMEM((2, HEAD_PAD, tq), F32)],
            compiler_params=_params(3),
            name="mla_attn",
        )(q_a, k_a, v_a)

        gates5 = gates.reshape(2, 2, MLSTM_HEADS, T // L, L)
        head_blk = lambda b, h: (b, h)
        y_mlstm = pl.pallas_call(
            functools.partial(_mlstm_kernel, L=L, nc=nc),
            grid=(B, MLSTM_HEADS),
            in_specs=[
                pl.BlockSpec((S, MLSTM_DH), head_blk), pl.BlockSpec((S, MLSTM_DH), head_blk),
                pl.BlockSpec((S, MLSTM_DH), head_blk), pl.BlockSpec((S, MLSTM_DH), head_blk),
                pl.BlockSpec((2, 2, 1, nc, L), lambda b, h: (0, 0, h, b, 0)),
                pl.BlockSpec((1, MLSTM_DH), lambda b, h: (0, h)),
            ],
            out_specs=pl.BlockSpec((S, MLSTM_DH), head_blk),
            out_shape=jax.ShapeDtypeStruct((T, mw), BF16),
            scratch_shapes=[pltpu.VMEM((S, MLSTM_DH), F32),
                            pltpu.VMEM((2, MLSTM_DH, 2 * MLSTM_DH), F32)],
            compiler_params=_params(2),
            name="mlstm",
        )(q_m, k_m, v_m, mo_s, gates5, mlstm_out_norm_g[l].reshape(1, mw))

        tm4 = 256
        last = l == depth - 1
        gf = norm_final_g.reshape(1, D)
        weights4 = (wgab, w_branch_mla[l].astype(BF16), w_branch_mlstm[l].astype(BF16),
                    w_out[l].astype(BF16), w_mlp_up[l].astype(BF16), w_mlp_down[l].astype(BF16))
        xf = pl.pallas_call(
            functools.partial(_merge_mlp_kernel, d=D, final_norm=last),
            grid=(T // tm4,),
            in_specs=[
                pl.BlockSpec((tm4, D), row),
                pl.BlockSpec((tm4, MLA_HEADS * MLA_V), row),
                pl.BlockSpec((tm4, mw), row),
                _const_spec((1, D)),
                _resident_spec(weights4[0].shape), _resident_spec(weights4[1].shape),
                _resident_spec(weights4[2].shape), _resident_spec(weights4[3].shape),
                _const_spec((1, D)),
                _resident_spec(weights4[4].shape), _resident_spec(weights4[5].shape),
                _const_spec((1, D)),
            ],
            out_specs=pl.BlockSpec((tm4, D), row),
            out_shape=jax.ShapeDtypeStruct((T, D), F32),
            compiler_params=_params(1),
            name="merge_mlp",
        )(xf, y_attn, y_mlstm, norm_mix_g[l].reshape(1, D), weights4[0], weights4[1], weights4[2],
          weights4[3], norm_mlp_g[l].reshape(1, D), weights4[4], weights4[5], gf)

    return xf.reshape(B, S, D)
```

```python
import functools
import math

import jax
import jax.numpy as jnp
from jax import lax
from jax.experimental import pallas as pl
from jax.experimental.pallas import tpu as pltpu

MLA_HEADS = 8
MLA_NOPE = 64
MLA_ROPE = 32
MLA_V = 64
ROPE_THETA = 10000.0
MLSTM_HEADS = 4
MLSTM_DH = 128
CONV_WIDTH = 5
NORM_EPS = 1e-6

LANES = 128
BF16_SUBLANES = 16
VMEM_LIMIT_BYTES = 56 * 1024 * 1024

HEAD_PAD = LANES
MLSTM_CHUNK = 256
HALO = BF16_SUBLANES

F32 = jnp.float32
BF16 = jnp.bfloat16


def _rms(x, g):
    return x * lax.rsqrt(jnp.mean(x * x, axis=-1, keepdims=True) + NORM_EPS) * g


def _dot(a, b):
    return jnp.dot(a, b, preferred_element_type=F32)


def _dot_nt(a, b):
    return lax.dot_general(a, b, (((1,), (1,)), ((), ())), preferred_element_type=F32)


def _dot_tn(a, b):
    return lax.dot_general(a, b, (((0,), (0,)), ((), ())), preferred_element_type=F32)


def _log_sigmoid(x):
    return jnp.minimum(x, 0.0) - jnp.log1p(jnp.exp(-jnp.abs(x)))


def _inproj_kernel(xm_ref, xp_ref, xn_ref, cos_ref, sin_ref, g_ref, w1_ref, w2_ref, w3_ref,
                   wgt_ref, gb_ref, cw_ref, cb_ref, qg_ref, wqa_ref, wqb_ref, kvg_ref,
                   wuk_ref, wuvt_ref,
                   q_out, k_out, va_out, qm_out, km_out, vm_out, mo_out, gate_out,
                   pre_scr, *, tm, tiles_per_seq, q_lora, kv_lora, mw, q_scale, k_scale):
    i = pl.program_id(0)
    pos_in_seq = i % tiles_per_seq
    xp = jnp.where(pos_in_seq == 0, 0.0, xp_ref[...])
    xn = jnp.where(pos_in_seq == tiles_per_seq - 1, 0.0, xn_ref[...])
    xe = jnp.concatenate([xp, xm_ref[...], xn], axis=0)
    he = _rms(xe, g_ref[...]).astype(BF16)
    hm = he[HALO:HALO + tm]

    pre_scr[...] = _dot(he, w1_ref[...])
    conv = cb_ref[...]
    for j in range(CONV_WIDTH):
        conv = conv + cw_ref[j:j + 1, :] * pre_scr[pl.ds(HALO - CONV_WIDTH // 2 + j, tm), :]
    qk = conv * jax.nn.sigmoid(conv)
    qm_out[...] = qk[:, :mw].astype(BF16)
    km_out[...] = (qk[:, mw:] * k_scale).astype(BF16)

    vo = _dot(hm, w2_ref[...])
    vm_out[...] = vo[:, :mw].astype(BF16)
    mo_out[...] = jax.nn.sigmoid(vo[:, mw:]).astype(BF16)

    gt = _dot_nt(wgt_ref[...], hm) + gb_ref[...]
    row = lax.broadcasted_iota(jnp.int32, gt.shape, 0)
    is_f = (row % (2 * MLSTM_HEADS)) >= MLSTM_HEADS
    gate_out[...] = jnp.where(is_f, _log_sigmoid(gt), gt)

    c = _dot(hm, w3_ref[...])
    cqn = _rms(c[:, :q_lora], qg_ref[...]).astype(BF16)
    ckvn = _rms(c[:, q_lora:q_lora + kv_lora], kvg_ref[...]).astype(BF16)
    kra = c[:, q_lora + kv_lora:q_lora + kv_lora + HEAD_PAD]
    krb = c[:, q_lora + kv_lora + HEAD_PAD:]
    cos = cos_ref[...]
    sin = sin_ref[...]
    cos_h = jnp.concatenate([cos] * MLA_HEADS, axis=1)
    sin_h = jnp.concatenate([sin] * MLA_HEADS, axis=1)
    q = _dot(cqn, wqa_ref[...]) * cos_h + _dot(cqn, wqb_ref[...]) * sin_h
    q_out[...] = (q * q_scale).astype(BF16)
    kr = kra * cos + krb * sin
    k = _dot(ckvn, wuk_ref[...]) + jnp.concatenate([kr] * MLA_HEADS, axis=1)
    k_out[...] = k.astype(BF16)
    vt = _dot_nt(wuvt_ref[...], ckvn)
    ones = jnp.ones((HEAD_PAD - MLA_V, tm), F32)
    pieces = []
    for h in range(MLA_HEADS):
        pieces += [vt[h * MLA_V:(h + 1) * MLA_V], ones]
    va_out[0] = jnp.concatenate(pieces, axis=0).astype(BF16)


def _aligned(x, m):
    return x if isinstance(x, int) else pl.multiple_of(x, m)


def _attn_kernel(q_ref, k_ref, vt_ref, o_ref, s_0, s_1, p_0, p_1, acc_scr, *, tq, tk, vchunk, n_q, n_k):
    s_buf, p_buf = (s_0, s_1), (p_0, p_1)
    sub = tk // vchunk
    n_total = n_q * n_k
    assert n_k % 2 == 0 and n_total >= 4

    def scores(j, par):
        qoff = _aligned((j // n_k) * tq, tq)
        koff = _aligned((j % n_k) * tk, tk)
        for hh in range(2):
            cols = slice(hh * HEAD_PAD, (hh + 1) * HEAD_PAD)
            s_buf[par][hh] = _dot_nt(k_ref[pl.ds(koff, tk), cols], q_ref[pl.ds(qoff, tq), cols])

    def pv_issue(j, par):
        kb = j % n_k
        out = []
        for hh in range(2):
            vt = jnp.concatenate([vt_ref[kb * sub + c, hh * HEAD_PAD:(hh + 1) * HEAD_PAD, :]
                                  for c in range(sub)], axis=1)
            out.append(_dot(vt, p_buf[par][hh]))
        return out

    def pv_accumulate(r, alpha):
        for hh in range(2):
            acc_scr[hh] = alpha[hh] * acc_scr[hh] + r[hh]

    def softmax(j, par, m):
        first = (j % n_k) == 0
        m_out, alpha = [], []
        for hh in range(2):
            s = s_buf[par][hh]
            m_old = jnp.where(first, -jnp.inf, m[hh])
            m_new = jnp.maximum(m_old, jnp.max(s, axis=0, keepdims=True))
            p_buf[par][hh] = jnp.exp2(s - m_new).astype(BF16)
            alpha.append(jnp.exp2(m_old - m_new))
            m_out.append(m_new)
        return tuple(m_out), tuple(alpha)

    def finalize(j):
        qoff = _aligned((j // n_k) * tq, tq)
        o_t = jnp.concatenate([acc_scr[hh, :MLA_V] / acc_scr[hh, MLA_V:MLA_V + 1] for hh in range(2)],
                              axis=0)
        o_ref[pl.ds(qoff, tq), :] = o_t.T.astype(BF16)

    def body(k, par, m, alpha):
        r = pv_issue(k, par)
        scores(k + 2, par)
        m, alpha_next = softmax(k + 1, 1 - par, m)
        pv_accumulate(r, alpha)
        return m, alpha_next

    def double_body(kk, carry):
        m, alpha = carry
        m, alpha = body(2 * kk, 0, m, alpha)
        m, alpha = body(2 * kk + 1, 1, m, alpha)

        @pl.when((2 * kk + 1) % n_k == n_k - 1)
        def _():
            finalize(2 * kk + 1)

        return m, alpha

    acc_scr[...] = jnp.zeros_like(acc_scr)
    m = tuple(jnp.full((1, tq), -jnp.inf, F32) for _ in range(2))
    scores(0, 0)
    scores(1, 1)
    m, alpha = softmax(0, 0, m)
    m, alpha = lax.fori_loop(0, (n_total - 2) // 2, double_body, (m, alpha))
    r = pv_issue(n_total - 2, 0)
    m, alpha_last = softmax(n_total - 1, 1, m)
    pv_accumulate(r, alpha)
    pv_accumulate(pv_issue(n_total - 1, 1), alpha_last)
    finalize(n_total - 1)


def _mlstm_kernel(q_ref, k_ref, v_ref, mo_ref, gate_ref, og_ref, o_ref, hacc, cn_scr, *, L, nc):
    dh = MLSTM_DH
    t_idx = lax.broadcasted_iota(jnp.int32, (L, L), 0)
    u_idx = lax.broadcasted_iota(jnp.int32, (L, L), 1)
    eye = t_idx == u_idx
    ones_col = (lax.broadcasted_iota(jnp.int32, (L, dh), 1) == 0).astype(F32)

    def chunk(c, m, direction):
        tri = (u_idx <= t_idx) if direction == 0 else (u_idx >= t_idx)
        tri_t = (t_idx <= u_idx) if direction == 0 else (t_idx >= u_idx)
        off = pl.multiple_of(c * L, L)
        q_c = q_ref[pl.ds(off, L), :]
        k_c = k_ref[pl.ds(off, L), :]
        v_c = v_ref[pl.ds(off, L), :]
        logi = gate_ref[direction, 0, 0, pl.ds(c, 1), :]
        logf = gate_ref[direction, 1, 0, pl.ds(c, 1), :]

        b_col = jnp.sum(jnp.where(tri, logf, 0.0), axis=1, keepdims=True)
        logf_col = jnp.sum(jnp.where(eye, logf, 0.0), axis=1, keepdims=True)
        logi_col = jnp.sum(jnp.where(eye, logi, 0.0), axis=1, keepdims=True)
        b_row = jnp.sum(jnp.where(tri_t, logf_col, 0.0), axis=0, keepdims=True)
        b_last = jnp.sum(logf, axis=1, keepdims=True)
        a_row = logi - b_row
        a_col = logi_col - b_col

        A = jnp.where(tri, a_row, -jnp.inf)
        g = jnp.maximum(jnp.max(A, axis=1, keepdims=True), m)
        E = jnp.exp(A - g)
        P = (_dot_nt(q_c, k_c) * E).astype(BF16)
        v_aug = jnp.concatenate([v_c.astype(F32), ones_col], axis=1)
        R = _dot(P, v_aug.astype(BF16))
        inter = _dot(q_c, cn_scr[direction].astype(BF16))
        iw = jnp.exp(m - g)
        num = iw * inter[:, :dh] + R[:, :dh]
        den = iw * inter[:, dh:dh + 1] + R[:, dh:dh + 1]
        h = num / jnp.maximum(jnp.abs(den), jnp.exp(-(b_col + g)))

        m_new = jnp.maximum(b_last + m, b_last + jnp.max(a_row, axis=1, keepdims=True))
        decay = jnp.exp(b_last + m - m_new)
        w_col = jnp.exp(b_last + a_col - m_new)
        U = _dot_tn(k_c, (v_aug * w_col).astype(BF16))
        cn_scr[direction] = decay * cn_scr[direction] + U
        return h, m_new

    cn_scr[...] = jnp.zeros_like(cn_scr)
    m0 = jnp.zeros((1, 1), F32)

    def fwd_body(j, m):
        h, m_new = chunk(j, m, 0)
        hacc[pl.ds(pl.multiple_of(j * L, L), L), :] = h
        return m_new

    lax.fori_loop(0, nc, fwd_body, m0)

    def bwd_body(j, m):
        c = nc - 1 - j
        h, m_new = chunk(c, m, 1)
        off = pl.multiple_of(c * L, L)
        hs = hacc[pl.ds(off, L), :] + h
        hn = _rms(hs, og_ref[...])
        o_ref[pl.ds(off, L), :] = (hn * mo_ref[pl.ds(off, L), :].astype(F32)).astype(BF16)
        return m_new

    lax.fori_loop(0, nc, bwd_body, m0)


def _merge_mlp_kernel(x_ref, ya_ref, ym_ref, g1_ref, wgab_ref, wbm_ref, wbl_ref, wout_ref,
                      g2_ref, wup_ref, wdn_ref, gf_ref, o_ref, *, d, final_norm):
    x = x_ref[...]
    hn = _rms(x, g1_ref[...]).astype(BF16)
    gates = jax.nn.sigmoid(_dot(hn, wgab_ref[...]))
    merged = (gates[:, :d] * _dot(ya_ref[...], wbm_ref[...])
              + gates[:, d:] * _dot(ym_ref[...], wbl_ref[...]))
    x1 = x + _dot(merged.astype(BF16), wout_ref[...])
    u = _dot(_rms(x1, g2_ref[...]).astype(BF16), wup_ref[...])
    r = jnp.maximum(u, 0.0)
    x2 = x1 + _dot((r * r).astype(BF16), wdn_ref[...])
    o_ref[...] = _rms(x2, gf_ref[...]) if final_norm else x2


def _const_spec(shape):
    return pl.BlockSpec(shape, lambda *_: (0,) * len(shape))


def _resident_spec(shape):
    return pl.BlockSpec(shape, lambda *_: (0,) * len(shape), pipeline_mode=pl.Buffered(1))


def _params(n_axes):
    return pltpu.CompilerParams(dimension_semantics=("arbitrary",) * n_axes,
                                vmem_limit_bytes=VMEM_LIMIT_BYTES)


def _rope_tables(positions):
    half = MLA_ROPE // 2
    inv_freq = ROPE_THETA ** (-jnp.arange(0, MLA_ROPE, 2, dtype=F32) / MLA_ROPE)
    ang = positions.astype(F32).reshape(-1, 1) * inv_freq
    cos, sin = jnp.cos(ang), jnp.sin(ang)
    t = ang.shape[0]
    pad = HEAD_PAD - MLA_NOPE - MLA_ROPE
    cos_t = jnp.concatenate([jnp.ones((t, MLA_NOPE), F32), cos, cos, jnp.zeros((t, pad), F32)], axis=1)
    sin_t = jnp.concatenate([jnp.zeros((t, MLA_NOPE), F32), -sin, sin, jnp.zeros((t, pad), F32)], axis=1)
    return cos_t, sin_t


def _pad_heads(w, width_in, offset_out):
    k = w.shape[0]
    w = w.reshape(k, MLA_HEADS, width_in)
    w = jnp.pad(w, ((0, 0), (0, 0), (offset_out, HEAD_PAD - width_in - offset_out)))
    return w.reshape(k, MLA_HEADS * HEAD_PAD)


def kernel(x, positions, norm_mix_g, w_in, mla_q_norm_g, mla_w_uq, mla_kv_norm_g, mla_w_ukv, mlstm_conv_w, mlstm_conv_b, mlstm_igate_b, mlstm_fgate_b, mlstm_out_norm_g, w_branch_mla, w_branch_mlstm, w_out, norm_mlp_g, w_mlp_up, w_mlp_down, norm_final_g):
    B, S, D = x.shape
    T = B * S
    depth = w_in.shape[0]
    q_lora = mla_q_norm_g.shape[1]
    kv_lora = mla_kv_norm_g.shape[1]
    mw = MLSTM_HEADS * MLSTM_DH
    half = MLA_ROPE // 2
    n_gate = 4 * MLSTM_HEADS
    d_ff = w_mlp_up.shape[2]
    L = MLSTM_CHUNK
    nc = S // L
    assert S % L == 0 and MLA_HEADS % 2 == 0

    cos_t, sin_t = _rope_tables(positions)
    xf = x.reshape(T, D)

    for l in range(depth):
        offs, o = [], 0
        for w in (q_lora, kv_lora, MLA_ROPE, mw, mw, mw, mw, n_gate, D, D):
            offs.append(o)
            o += w
        wl = w_in[l]
        w_cq = wl[:, offs[0]:offs[0] + q_lora]
        w_ckv = wl[:, offs[1]:offs[1] + kv_lora]
        w_kr = wl[:, offs[2]:offs[2] + MLA_ROPE]
        w_kr_swapped = jnp.concatenate([w_kr[:, half:], w_kr[:, :half]], axis=1)
        pad_kr = ((0, 0), (MLA_NOPE, HEAD_PAD - MLA_NOPE - MLA_ROPE))
        w1 = wl[:, offs[3]:offs[3] + 2 * mw].astype(BF16)
        w2 = wl[:, offs[5]:offs[5] + 2 * mw].astype(BF16)
        w3 = jnp.concatenate([w_cq, w_ckv, jnp.pad(w_kr, pad_kr), jnp.pad(w_kr_swapped, pad_kr)],
                             axis=1).astype(BF16)
        wgt = wl[:, offs[7]:offs[7] + n_gate].T.astype(BF16)
        wgab = wl[:, offs[8]:offs[8] + 2 * D].astype(BF16)
        gate_b = jnp.stack([mlstm_igate_b[l], mlstm_fgate_b[l]], axis=1).reshape(n_gate, 1)

        dqk = MLA_NOPE + MLA_ROPE
        wq = mla_w_uq[l].reshape(q_lora, MLA_HEADS, dqk)
        wq_rope = wq[:, :, MLA_NOPE:]
        wq_swapped = jnp.concatenate([jnp.zeros_like(wq[:, :, :MLA_NOPE]),
                                      wq_rope[:, :, half:], wq_rope[:, :, :half]], axis=2)
        wqa = _pad_heads(wq.reshape(q_lora, -1), dqk, 0).astype(BF16)
        wqb = _pad_heads(wq_swapped.reshape(q_lora, -1), dqk, 0).astype(BF16)
        wkv = mla_w_ukv[l].reshape(kv_lora, MLA_HEADS, MLA_NOPE + MLA_V)
        wuk = _pad_heads(wkv[:, :, :MLA_NOPE].reshape(kv_lora, -1), MLA_NOPE, 0).astype(BF16)
        wuvt = wkv[:, :, MLA_NOPE:].reshape(kv_lora, MLA_HEADS * MLA_V).T.astype(BF16)

        cw = jnp.pad(mlstm_conv_w[l].reshape(CONV_WIDTH, 2 * mw), ((0, 8 - CONV_WIDTH), (0, 0)))
        cb = mlstm_conv_b[l].reshape(1, 2 * mw)

        tm = 256
        tiles_per_seq = S // tm
        hb = tm // HALO
        n_halo_blocks = T // HALO
        row = lambda i: (i, 0)
        kern = functools.partial(
            _inproj_kernel, tm=tm, tiles_per_seq=tiles_per_seq, q_lora=q_lora, kv_lora=kv_lora, mw=mw,
            q_scale=(MLA_NOPE + MLA_ROPE) ** -0.5 * math.log2(math.e), k_scale=MLSTM_DH ** -0.5)
        outs = pl.pallas_call(
            kern,
            grid=(T // tm,),
            in_specs=[
                pl.BlockSpec((tm, D), row),
                pl.BlockSpec((HALO, D), lambda i: (jnp.maximum(i * hb - 1, 0), 0)),
                pl.BlockSpec((HALO, D), lambda i: (jnp.minimum((i + 1) * hb, n_halo_blocks - 1), 0)),
                pl.BlockSpec((tm, HEAD_PAD), row),
                pl.BlockSpec((tm, HEAD_PAD), row),
                _const_spec((1, D)),
                _const_spec(w1.shape), _const_spec(w2.shape), _const_spec(w3.shape),
                _const_spec(wgt.shape), _const_spec(gate_b.shape),
                _const_spec(cw.shape), _const_spec(cb.shape),
                _const_spec((1, q_lora)), _const_spec(wqa.shape), _const_spec(wqb.shape),
                _const_spec((1, kv_lora)), _const_spec(wuk.shape), _const_spec(wuvt.shape),
            ],
            out_specs=[
                pl.BlockSpec((tm, MLA_HEADS * HEAD_PAD), row),
                pl.BlockSpec((tm, MLA_HEADS * HEAD_PAD), row),
                pl.BlockSpec((1, MLA_HEADS * HEAD_PAD, tm), lambda i: (i, 0, 0)),
                pl.BlockSpec((tm, mw), row), pl.BlockSpec((tm, mw), row),
                pl.BlockSpec((tm, mw), row), pl.BlockSpec((tm, mw), row),
                pl.BlockSpec((n_gate, tm), lambda i: (0, i)),
            ],
            out_shape=[
                jax.ShapeDtypeStruct((T, MLA_HEADS * HEAD_PAD), BF16),
                jax.ShapeDtypeStruct((T, MLA_HEADS * HEAD_PAD), BF16),
                jax.ShapeDtypeStruct((T // tm, MLA_HEADS * HEAD_PAD, tm), BF16),
                jax.ShapeDtypeStruct((T, mw), BF16), jax.ShapeDtypeStruct((T, mw), BF16),
                jax.ShapeDtypeStruct((T, mw), BF16), jax.ShapeDtypeStruct((T, mw), BF16),
                jax.ShapeDtypeStruct((n_gate, T), F32),
            ],
            scratch_shapes=[pltpu.VMEM((tm + 2 * HALO, 2 * mw), F32)],
            compiler_params=_params(1),
            name="inproj",
        )(xf, xf, xf, cos_t, sin_t, norm_mix_g[l].reshape(1, D), w1, w2, w3, wgt, gate_b, cw, cb,
          mla_q_norm_g[l].reshape(1, q_lora), wqa, wqb, mla_kv_norm_g[l].reshape(1, kv_lora), wuk, wuvt)
        q_a, k_a, v_a, q_m, k_m, v_m, mo_s, gates = outs

        tq, tk = 256, 1024
        vchunk = tm
        y_attn = pl.pallas_call(
            functools.partial(_attn_kernel, tq=tq, tk=tk, vchunk=vchunk, n_q=S // tq, n_k=S // tk),
            grid=(B, MLA_HEADS // 2),
            in_specs=[
                pl.BlockSpec((S, 2 * HEAD_PAD), lambda b, p: (b, p)),
                pl.BlockSpec((S, 2 * HEAD_PAD), lambda b, p: (b, p)),
                pl.BlockSpec((S // vchunk, 2 * HEAD_PAD, vchunk), lambda b, p: (b, p, 0)),
            ],
            out_specs=pl.BlockSpec((S, 2 * MLA_V), lambda b, p: (b, p)),
            out_shape=jax.ShapeDtypeStruct((T, MLA_HEADS * MLA_V), BF16),
            scratch_shapes=[pltpu.VMEM((2, tk, tq), F32), pltpu.VMEM((2, tk, tq), F32),
                            pltpu.VMEM((2, tk, tq), BF16), pltpu.VMEM((2, tk, tq), BF16),
                            pltpu.VMEM((2, HEAD_PAD, tq), F32)],
            compiler_params=_params(2),
            name="mla_attn",
        )(q_a, k_a, v_a)

        gates5 = gates.reshape(2, 2, MLSTM_HEADS, T // L, L)
        head_blk = lambda b, h: (b, h)
        y_mlstm = pl.pallas_call(
            functools.partial(_mlstm_kernel, L=L, nc=nc),
            grid=(B, MLSTM_HEADS),
            in_specs=[
                pl.BlockSpec((S, MLSTM_DH), head_blk), pl.BlockSpec((S, MLSTM_DH), head_blk),
                pl.BlockSpec((S, MLSTM_DH), head_blk), pl.BlockSpec((S, MLSTM_DH), head_blk),
                pl.BlockSpec((2, 2, 1, nc, L), lambda b, h: (0, 0, h, b, 0)),
                pl.BlockSpec((1, MLSTM_DH), lambda b, h: (0, h)),
            ],
            out_specs=pl.BlockSpec((S, MLSTM_DH), head_blk),
            out_shape=jax.ShapeDtypeStruct((T, mw), BF16),
            scratch_shapes=[pltpu.VMEM((S, MLSTM_DH), F32),
                            pltpu.VMEM((2, MLSTM_DH, 2 * MLSTM_DH), F32)],
            compiler_params=_params(2),
            name="mlstm",
        )(q_m, k_m, v_m, mo_s, gates5, mlstm_out_norm_g[l].reshape(1, mw))

        tm4 = 256
        last = l == depth - 1
        gf = norm_final_g.reshape(1, D)
        weights4 = (wgab, w_branch_mla[l].astype(BF16), w_branch_mlstm[l].astype(BF16),
                    w_out[l].astype(BF16), w_mlp_up[l].astype(BF16), w_mlp_down[l].astype(BF16))
        xf = pl.pallas_call(
            functools.partial(_merge_mlp_kernel, d=D, final_norm=last),
            grid=(T // tm4,),
            in_specs=[
                pl.BlockSpec((tm4, D), row),
                pl.BlockSpec((tm4, MLA_HEADS * MLA_V), row),
                pl.BlockSpec((tm4, mw), row),
                _const_spec((1, D)),
                _resident_spec(weights4[0].shape), _resident_spec(weights4[1].shape),
                _resident_spec(weights4[2].shape), _resident_spec(weights4[3].shape),
                _const_spec((1, D)),
                _resident_spec(weights4[4].shape), _resident_spec(weights4[5].shape),
                _const_spec((1, D)),
            ],
            out_specs=pl.BlockSpec((tm4, D), row),
            out_shape=jax.ShapeDtypeStruct((T, D), F32),
            compiler_params=_params(1),
            name="merge_mlp",
        )(xf, y_attn, y_mlstm, norm_mix_g[l].reshape(1, D), weights4[0], weights4[1], weights4[2],
          weights4[3], norm_mlp_g[l].reshape(1, D), weights4[4], weights4[5], gf)

    return xf.reshape(B, S, D)
```

```python
import functools
import math

import jax
import jax.numpy as jnp
from jax import lax
from jax.experimental import pallas as pl
from jax.experimental.pallas import tpu as pltpu

MLA_HEADS = 8
MLA_NOPE = 64
MLA_ROPE = 32
MLA_V = 64
ROPE_THETA = 10000.0
MLSTM_HEADS = 4
MLSTM_DH = 128
CONV_WIDTH = 5
NORM_EPS = 1e-6

LANES = 128
BF16_SUBLANES = 16
VMEM_LIMIT_BYTES = 56 * 1024 * 1024

HEAD_PAD = LANES
MLSTM_CHUNK = 256
HALO = BF16_SUBLANES

F32 = jnp.float32
BF16 = jnp.bfloat16


def _rms(x, g):
    return x * lax.rsqrt(jnp.mean(x * x, axis=-1, keepdims=True) + NORM_EPS) * g


def _dot(a, b):
    return jnp.dot(a, b, preferred_element_type=F32)


def _dot_nt(a, b):
    return lax.dot_general(a, b, (((1,), (1,)), ((), ())), preferred_element_type=F32)


def _dot_tn(a, b):
    return lax.dot_general(a, b, (((0,), (0,)), ((), ())), preferred_element_type=F32)


def _log_sigmoid(x):
    return jnp.minimum(x, 0.0) - jnp.log1p(jnp.exp(-jnp.abs(x)))


def _inproj_kernel(xm_ref, xp_ref, xn_ref, cos_ref, sin_ref, g_ref, w1_ref, w2t_ref, w3_ref,
                   wgt_ref, gb_ref, cw_ref, cb_ref, qg_ref, wqa_ref, wqb_ref, kvg_ref,
                   wuk_ref, wuvt_ref,
                   q_out, k_out, va_out, qm_out, km_out, vm_out, mo_out, gate_out,
                   pre_scr, *, tm, tiles_per_seq, q_lora, kv_lora, mw, q_scale, k_scale):
    i = pl.program_id(0)
    pos_in_seq = i % tiles_per_seq
    xp = jnp.where(pos_in_seq == 0, 0.0, xp_ref[...])
    xn = jnp.where(pos_in_seq == tiles_per_seq - 1, 0.0, xn_ref[...])
    xe = jnp.concatenate([xp, xm_ref[...], xn], axis=0)
    he = _rms(xe, g_ref[...]).astype(BF16)
    hm = he[HALO:HALO + tm]

    pre_scr[...] = _dot(he, w1_ref[...])
    conv = cb_ref[...]
    for j in range(CONV_WIDTH):
        conv = conv + cw_ref[j:j + 1, :] * pre_scr[pl.ds(HALO - CONV_WIDTH // 2 + j, tm), :]
    qk = conv * jax.nn.sigmoid(conv)
    qm_out[...] = qk[:, :mw].astype(BF16)
    km_out[...] = (qk[:, mw:] * k_scale).astype(BF16)

    vo_t = _dot_nt(w2t_ref[...], hm)
    vm_out[0] = vo_t[:mw].astype(BF16)
    mo_out[0] = jax.nn.sigmoid(vo_t[mw:]).astype(BF16)

    gt = _dot_nt(wgt_ref[...], hm) + gb_ref[...]
    row = lax.broadcasted_iota(jnp.int32, gt.shape, 0)
    is_f = (row % (2 * MLSTM_HEADS)) >= MLSTM_HEADS
    gate_out[...] = jnp.where(is_f, _log_sigmoid(gt), gt)

    c = _dot(hm, w3_ref[...])
    cqn = _rms(c[:, :q_lora], qg_ref[...]).astype(BF16)
    ckvn = _rms(c[:, q_lora:q_lora + kv_lora], kvg_ref[...]).astype(BF16)
    kra = c[:, q_lora + kv_lora:q_lora + kv_lora + HEAD_PAD]
    krb = c[:, q_lora + kv_lora + HEAD_PAD:]
    cos = cos_ref[...]
    sin = sin_ref[...]
    cos_h = jnp.concatenate([cos] * MLA_HEADS, axis=1)
    sin_h = jnp.concatenate([sin] * MLA_HEADS, axis=1)
    q = _dot(cqn, wqa_ref[...]) * cos_h + _dot(cqn, wqb_ref[...]) * sin_h
    q_out[...] = (q * q_scale).astype(BF16)
    kr = kra * cos + krb * sin
    k = _dot(ckvn, wuk_ref[...]) + jnp.concatenate([kr] * MLA_HEADS, axis=1)
    k_out[...] = k.astype(BF16)
    vt = _dot_nt(wuvt_ref[...], ckvn)
    ones = jnp.ones((HEAD_PAD - MLA_V, tm), F32)
    pieces = []
    for h in range(MLA_HEADS):
        pieces += [vt[h * MLA_V:(h + 1) * MLA_V], ones]
    va_out[0] = jnp.concatenate(pieces, axis=0).astype(BF16)


def _aligned(x, m):
    return x if isinstance(x, int) else pl.multiple_of(x, m)


def _attn_kernel(q_ref, k_ref, vt_ref, o_ref, s_0, s_1, p_0, p_1, acc_scr, *, tq, tk, vchunk, n_q, n_k,
                 bodies_per_iter):
    s_buf, p_buf = (s_0, s_1), (p_0, p_1)
    sub = tk // vchunk
    n_total = n_q * n_k
    assert n_k % 2 == 0 and n_total >= 4 and bodies_per_iter % 2 == 0

    def scores(j, par):
        qoff = _aligned((j // n_k) * tq, tq)
        koff = _aligned((j % n_k) * tk, tk)
        cmax = []
        for hh in range(2):
            cols = slice(hh * HEAD_PAD, (hh + 1) * HEAD_PAD)
            s = _dot_nt(k_ref[pl.ds(koff, tk), cols], q_ref[pl.ds(qoff, tq), cols])
            s_buf[par][hh] = s
            cmax.append(jnp.max(s, axis=0, keepdims=True))
        return tuple(cmax)

    def pv_issue(j, par):
        kb = j % n_k
        out = []
        for hh in range(2):
            vt = jnp.concatenate([vt_ref[kb * sub + c, hh * HEAD_PAD:(hh + 1) * HEAD_PAD, :]
                                  for c in range(sub)], axis=1)
            out.append(_dot(vt, p_buf[par][hh]))
        return out

    def pv_accumulate(r, alpha):
        for hh in range(2):
            acc_scr[hh] = alpha[hh] * acc_scr[hh] + r[hh]

    def softmax(j, par, m, cmax):
        first = (j % n_k) == 0
        m_out, alpha = [], []
        for hh in range(2):
            m_old = jnp.where(first, -jnp.inf, m[hh])
            m_new = jnp.maximum(m_old, cmax[hh])
            p_buf[par][hh] = jnp.exp2(s_buf[par][hh] - m_new).astype(BF16)
            alpha.append(jnp.exp2(m_old - m_new))
            m_out.append(m_new)
        return tuple(m_out), tuple(alpha)

    def finalize(j):
        qoff = _aligned((j // n_k) * tq, tq)
        o_t = jnp.concatenate([acc_scr[hh, :MLA_V] / acc_scr[hh, MLA_V:MLA_V + 1] for hh in range(2)],
                              axis=0)
        o_ref[pl.ds(qoff, tq), :] = o_t.T.astype(BF16)

    def body(k, par, m, alpha, cmax):
        r = pv_issue(k, par)
        cmax_next = scores(k + 2, par)
        m, alpha_next = softmax(k + 1, 1 - par, m, cmax)
        pv_accumulate(r, alpha)
        return m, alpha_next, cmax_next

    def steps(k0, count, m, alpha, cmax):
        for o in range(count):
            k = k0 + o
            m, alpha, cmax = body(k, o % 2, m, alpha, cmax)
            if isinstance(k0, int):
                if k % n_k == n_k - 1:
                    finalize(k)
            elif any((bodies_per_iter * t + o) % n_k == n_k - 1 for t in range(n_k)):
                pl.when(k % n_k == n_k - 1)(functools.partial(finalize, k))
        return m, alpha, cmax

    acc_scr[...] = jnp.zeros_like(acc_scr)
    m = tuple(jnp.full((1, tq), -jnp.inf, F32) for _ in range(2))
    cmax0 = scores(0, 0)
    cmax = scores(1, 1)
    m, alpha = softmax(0, 0, m, cmax0)
    n_iter, n_rest = divmod(n_total - 2, bodies_per_iter)
    m, alpha, cmax = lax.fori_loop(
        0, n_iter, lambda kk, c: steps(bodies_per_iter * kk, bodies_per_iter, *c), (m, alpha, cmax))
    m, alpha, cmax = steps(bodies_per_iter * n_iter, n_rest, m, alpha, cmax)
    r = pv_issue(n_total - 2, 0)
    m, alpha_last = softmax(n_total - 1, 1, m, cmax)
    pv_accumulate(r, alpha)
    pv_accumulate(pv_issue(n_total - 1, 1), alpha_last)
    finalize(n_total - 1)


def _scan_lanes(x, op, fill, reverse):
    n = x.shape[1]
    lane = lax.broadcasted_iota(jnp.int32, x.shape, 1)
    shift = 1
    while shift < n:
        if reverse:
            moved, valid = pltpu.roll(x, n - shift, 1), lane < n - shift
        else:
            moved, valid = pltpu.roll(x, shift, 1), lane >= shift
        x = op(x, jnp.where(valid, moved, fill))
        shift *= 2
    return x


class _Chain:
    pass


def _mlstm_kernel(q_ref, k_ref, vt_ref, mot_ref, gate_ref, og_ref, o_ref,
                  hacc, st_scr, b_scr, a_scr, cm_scr, bl_scr, am_scr, *, L, nc):
    dh = MLSTM_DH
    aug = st_scr.shape[1]
    s_idx = lax.broadcasted_iota(jnp.int32, (L, L), 0)
    t_idx = lax.broadcasted_iota(jnp.int32, (L, L), 1)
    eye = s_idx == t_idx
    visible = (s_idx <= t_idx, s_idx >= t_idx)
    ones_blk = (lax.broadcasted_iota(jnp.int32, (aug - dh, L), 0) == 0).astype(BF16)

    for d in range(2):
        logi = gate_ref[d, 0, 0]
        logf = gate_ref[d, 1, 0]
        b = _scan_lanes(logf, jnp.add, 0.0, reverse=d == 1)
        a = logi - b
        b_scr[d] = b
        a_scr[d] = a
        cm_scr[d] = _scan_lanes(a, jnp.maximum, -jnp.inf, reverse=d == 1)
        bl_scr[d] = jnp.sum(logf, axis=1, keepdims=True)
        am_scr[d] = jnp.max(a, axis=1, keepdims=True)
    st_scr[...] = jnp.zeros_like(st_scr)

    def prepare(d, c, m):
        x = _Chain()
        off = pl.multiple_of(c * L, L)
        x.q = q_ref[pl.ds(off, L), :]
        x.k = k_ref[pl.ds(off, L), :]
        x.vaug = jnp.concatenate([vt_ref[c], ones_blk], axis=0)
        x.b_row = b_scr[d, pl.ds(c, 1), :]
        x.a_row = a_scr[d, pl.ds(c, 1), :]
        b_last = bl_scr[d, pl.ds(c, 1), :]
        x.g_row = jnp.maximum(cm_scr[d, pl.ds(c, 1), :], m)
        x.iw = jnp.exp(m - x.g_row)
        x.m_new = jnp.maximum(b_last + m, b_last + am_scr[d, pl.ds(c, 1), :])
        x.decay = jnp.exp(b_last + m - x.m_new)
        x.w_row = jnp.exp(b_last + x.a_row - x.m_new)
        return x

    def issue_early(d, x):
        x.st = _dot_nt(x.k, x.q)
        x.inter = _dot_nt(st_scr[d].astype(BF16), x.q)
        x.upd = _dot((x.vaug.astype(F32) * x.w_row).astype(BF16), x.k)

    def intra(d, x):
        a_col = jnp.sum(jnp.where(eye, x.a_row, 0.0), axis=1, keepdims=True)
        e_t = jnp.exp(jnp.where(visible[d], a_col - x.g_row, -jnp.inf))
        x.r = _dot(x.vaug, (x.st * e_t).astype(BF16))

    def finish(d, x):
        num = x.iw * x.inter[:dh] + x.r[:dh]
        den = x.iw * x.inter[dh:dh + 1] + x.r[dh:dh + 1]
        floor = jnp.exp(-(x.b_row + x.g_row))
        st_scr[d] = x.decay * st_scr[d] + x.upd
        return num * (1.0 / jnp.maximum(jnp.abs(den), floor))

    def both(j, m_f, m_b):
        cf, cb = j, nc - 1 - j
        xs = (prepare(0, cf, m_f), prepare(1, cb, m_b))
        for d in range(2):
            issue_early(d, xs[d])
        for d in range(2):
            intra(d, xs[d])
        return cf, cb, finish(0, xs[0]), finish(1, xs[1]), xs[0].m_new, xs[1].m_new

    def first_touch(j, carry):
        cf, cb, h_f, h_b, m_f, m_b = both(j, *carry)
        hacc[cf] = h_f
        hacc[cb] = h_b
        return m_f, m_b

    def emit(c, h):
        tot = hacc[c] + h
        y = tot * lax.rsqrt(jnp.mean(tot * tot, axis=0, keepdims=True) + NORM_EPS) * og_ref[...]
        o_ref[c] = (y * mot_ref[c].astype(F32)).astype(BF16)

    def second_touch(j, carry):
        cf, cb, h_f, h_b, m_f, m_b = both(j, *carry)
        emit(cf, h_f)
        emit(cb, h_b)
        return m_f, m_b

    m0 = jnp.zeros((1, 1), F32)
    carry = lax.fori_loop(0, nc // 2, first_touch, (m0, m0))
    lax.fori_loop(nc // 2, nc, second_touch, carry)


def _merge_mlp_kernel(x_ref, ya_ref, ymt_ref, g1_ref, wgab_ref, wbm_ref, wbl_ref, wout_ref,
                      g2_ref, wup_ref, wdn_ref, gf_ref, o_ref, *, d, final_norm):
    x = x_ref[...]
    hn = _rms(x, g1_ref[...]).astype(BF16)
    gates = jax.nn.sigmoid(_dot(hn, wgab_ref[...]))
    merged = (gates[:, :d] * _dot(ya_ref[...], wbm_ref[...])
              + gates[:, d:] * _dot_tn(ymt_ref[0], wbl_ref[...]))
    x1 = x + _dot(merged.astype(BF16), wout_ref[...])
    u = _dot(_rms(x1, g2_ref[...]).astype(BF16), wup_ref[...])
    r = jnp.maximum(u, 0.0)
    x2 = x1 + _dot((r * r).astype(BF16), wdn_ref[...])
    o_ref[...] = _rms(x2, gf_ref[...]) if final_norm else x2


def _const_spec(shape):
    return pl.BlockSpec(shape, lambda *_: (0,) * len(shape))


def _resident_spec(shape):
    return pl.BlockSpec(shape, lambda *_: (0,) * len(shape), pipeline_mode=pl.Buffered(1))


def _params(n_axes):
    return pltpu.CompilerParams(dimension_semantics=("arbitrary",) * n_axes,
                                vmem_limit_bytes=VMEM_LIMIT_BYTES)


def _rope_tables(positions):
    half = MLA_ROPE // 2
    inv_freq = ROPE_THETA ** (-jnp.arange(0, MLA_ROPE, 2, dtype=F32) / MLA_ROPE)
    ang_t = inv_freq.reshape(-1, 1) * positions.astype(F32).reshape(1, -1)
    cos, sin = jnp.cos(ang_t).T, jnp.sin(ang_t).T
    t = cos.shape[0]
    pad = HEAD_PAD - MLA_NOPE - MLA_ROPE
    cos_t = jnp.concatenate([jnp.ones((t, MLA_NOPE), F32), cos, cos, jnp.zeros((t, pad), F32)], axis=1)
    sin_t = jnp.concatenate([jnp.zeros((t, MLA_NOPE), F32), -sin, sin, jnp.zeros((t, pad), F32)], axis=1)
    return cos_t, sin_t


def _pad_heads(w, width_in, offset_out):
    k = w.shape[0]
    w = w.reshape(k, MLA_HEADS, width_in)
    w = jnp.pad(w, ((0, 0), (0, 0), (offset_out, HEAD_PAD - width_in - offset_out)))
    return w.reshape(k, MLA_HEADS * HEAD_PAD)


def kernel(x, positions, norm_mix_g, w_in, mla_q_norm_g, mla_w_uq, mla_kv_norm_g, mla_w_ukv, mlstm_conv_w, mlstm_conv_b, mlstm_igate_b, mlstm_fgate_b, mlstm_out_norm_g, w_branch_mla, w_branch_mlstm, w_out, norm_mlp_g, w_mlp_up, w_mlp_down, norm_final_g):
    B, S, D = x.shape
    T = B * S
    depth = w_in.shape[0]
    q_lora = mla_q_norm_g.shape[1]
    kv_lora = mla_kv_norm_g.shape[1]
    mw = MLSTM_HEADS * MLSTM_DH
    half = MLA_ROPE // 2
    n_gate = 4 * MLSTM_HEADS
    d_ff = w_mlp_up.shape[2]
    L = MLSTM_CHUNK
    nc = S // L
    assert S % L == 0 and MLA_HEADS % 2 == 0

    cos_t, sin_t = _rope_tables(positions)
    xf = x.reshape(T, D)

    for l in range(depth):
        offs, o = [], 0
        for w in (q_lora, kv_lora, MLA_ROPE, mw, mw, mw, mw, n_gate, D, D):
            offs.append(o)
            o += w
        wl = w_in[l]
        w_cq = wl[:, offs[0]:offs[0] + q_lora]
        w_ckv = wl[:, offs[1]:offs[1] + kv_lora]
        w_kr = wl[:, offs[2]:offs[2] + MLA_ROPE]
        w_kr_swapped = jnp.concatenate([w_kr[:, half:], w_kr[:, :half]], axis=1)
        pad_kr = ((0, 0), (MLA_NOPE, HEAD_PAD - MLA_NOPE - MLA_ROPE))
        w1 = wl[:, offs[3]:offs[3] + 2 * mw].astype(BF16)
        w2t = wl[:, offs[5]:offs[5] + 2 * mw].T.astype(BF16)
        w3 = jnp.concatenate([w_cq, w_ckv, jnp.pad(w_kr, pad_kr), jnp.pad(w_kr_swapped, pad_kr)],
                             axis=1).astype(BF16)
        wgt = wl[:, offs[7]:offs[7] + n_gate].T.astype(BF16)
        wgab = wl[:, offs[8]:offs[8] + 2 * D].astype(BF16)
        gate_b = jnp.stack([mlstm_igate_b[l], mlstm_fgate_b[l]], axis=1).reshape(n_gate, 1)

        dqk = MLA_NOPE + MLA_ROPE
        wq = mla_w_uq[l].reshape(q_lora, MLA_HEADS, dqk)
        wq_rope = wq[:, :, MLA_NOPE:]
        wq_swapped = jnp.concatenate([jnp.zeros_like(wq[:, :, :MLA_NOPE]),
                                      wq_rope[:, :, half:], wq_rope[:, :, :half]], axis=2)
        wqa = _pad_heads(wq.reshape(q_lora, -1), dqk, 0).astype(BF16)
        wqb = _pad_heads(wq_swapped.reshape(q_lora, -1), dqk, 0).astype(BF16)
        wkv = mla_w_ukv[l].reshape(kv_lora, MLA_HEADS, MLA_NOPE + MLA_V)
        wuk = _pad_heads(wkv[:, :, :MLA_NOPE].reshape(kv_lora, -1), MLA_NOPE, 0).astype(BF16)
        wuvt = wkv[:, :, MLA_NOPE:].reshape(kv_lora, MLA_HEADS * MLA_V).T.astype(BF16)

        cw = jnp.pad(mlstm_conv_w[l].reshape(CONV_WIDTH, 2 * mw), ((0, 8 - CONV_WIDTH), (0, 0)))
        cb = mlstm_conv_b[l].reshape(1, 2 * mw)

        tm = 256
        tiles_per_seq = S // tm
        hb = tm // HALO
        n_halo_blocks = T // HALO
        row = lambda i: (i, 0)
        kern = functools.partial(
            _inproj_kernel, tm=tm, tiles_per_seq=tiles_per_seq, q_lora=q_lora, kv_lora=kv_lora, mw=mw,
            q_scale=(MLA_NOPE + MLA_ROPE) ** -0.5 * math.log2(math.e), k_scale=MLSTM_DH ** -0.5)
        outs = pl.pallas_call(
            kern,
            grid=(T // tm,),
            in_specs=[
                pl.BlockSpec((tm, D), row),
                pl.BlockSpec((HALO, D), lambda i: (jnp.maximum(i * hb - 1, 0), 0)),
                pl.BlockSpec((HALO, D), lambda i: (jnp.minimum((i + 1) * hb, n_halo_blocks - 1), 0)),
                pl.BlockSpec((tm, HEAD_PAD), row),
                pl.BlockSpec((tm, HEAD_PAD), row),
                _const_spec((1, D)),
                _const_spec(w1.shape), _const_spec(w2t.shape), _const_spec(w3.shape),
                _const_spec(wgt.shape), _const_spec(gate_b.shape),
                _const_spec(cw.shape), _const_spec(cb.shape),
                _const_spec((1, q_lora)), _const_spec(wqa.shape), _const_spec(wqb.shape),
                _const_spec((1, kv_lora)), _const_spec(wuk.shape), _const_spec(wuvt.shape),
            ],
            out_specs=[
                pl.BlockSpec((tm, MLA_HEADS * HEAD_PAD), row),
                pl.BlockSpec((tm, MLA_HEADS * HEAD_PAD), row),
                pl.BlockSpec((1, MLA_HEADS * HEAD_PAD, tm), lambda i: (i, 0, 0)),
                pl.BlockSpec((tm, mw), row), pl.BlockSpec((tm, mw), row),
                pl.BlockSpec((1, mw, tm), lambda i: (i, 0, 0)),
                pl.BlockSpec((1, mw, tm), lambda i: (i, 0, 0)),
                pl.BlockSpec((n_gate, tm), lambda i: (0, i)),
            ],
            out_shape=[
                jax.ShapeDtypeStruct((T, MLA_HEADS * HEAD_PAD), BF16),
                jax.ShapeDtypeStruct((T, MLA_HEADS * HEAD_PAD), BF16),
                jax.ShapeDtypeStruct((T // tm, MLA_HEADS * HEAD_PAD, tm), BF16),
                jax.ShapeDtypeStruct((T, mw), BF16), jax.ShapeDtypeStruct((T, mw), BF16),
                jax.ShapeDtypeStruct((T // tm, mw, tm), BF16),
                jax.ShapeDtypeStruct((T // tm, mw, tm), BF16),
                jax.ShapeDtypeStruct((n_gate, T), F32),
            ],
            scratch_shapes=[pltpu.VMEM((tm + 2 * HALO, 2 * mw), F32)],
            compiler_params=_params(1),
            name="inproj",
        )(xf, xf, xf, cos_t, sin_t, norm_mix_g[l].reshape(1, D), w1, w2t, w3, wgt, gate_b, cw, cb,
          mla_q_norm_g[l].reshape(1, q_lora), wqa, wqb, mla_kv_norm_g[l].reshape(1, kv_lora), wuk, wuvt)
        q_a, k_a, v_a, q_m, k_m, v_m, mo_s, gates = outs

        tq, tk = 256, 1024
        vchunk = tm
        y_attn = pl.pallas_call(
            functools.partial(_attn_kernel, tq=tq, tk=tk, vchunk=vchunk, n_q=S // tq, n_k=S // tk,
                              bodies_per_iter=2),
            grid=(B, MLA_HEADS // 2),
            in_specs=[
                pl.BlockSpec((S, 2 * HEAD_PAD), lambda b, p: (b, p)),
                pl.BlockSpec((S, 2 * HEAD_PAD), lambda b, p: (b, p)),
                pl.BlockSpec((S // vchunk, 2 * HEAD_PAD, vchunk), lambda b, p: (b, p, 0)),
            ],
            out_specs=pl.BlockSpec((S, 2 * MLA_V), lambda b, p: (b, p)),
            out_shape=jax.ShapeDtypeStruct((T, MLA_HEADS * MLA_V), BF16),
            scratch_shapes=[pltpu.VMEM((2, tk, tq), F32), pltpu.VMEM((2, tk, tq), F32),
                            pltpu.VMEM((2, tk, tq), BF16), pltpu.VMEM((2, tk, tq), BF16),
                            pltpu.VMEM((2, HEAD_PAD, tq), F32)],
            compiler_params=_params(2),
            name="mla_attn",
        )(q_a, k_a, v_a)

        gates5 = gates.reshape(2, 2, MLSTM_HEADS, T // L, L)
        assert tm == L
        head_blk = lambda b, h: (b, h)
        head_blk_t = lambda b, h: (b, h, 0)
        aug = MLSTM_DH + BF16_SUBLANES
        row_scratch = pltpu.VMEM((2, nc, L), F32)
        y_mlstm_t = pl.pallas_call(
            functools.partial(_mlstm_kernel, L=L, nc=nc),
            grid=(B, MLSTM_HEADS),
            in_specs=[
                pl.BlockSpec((S, MLSTM_DH), head_blk), pl.BlockSpec((S, MLSTM_DH), head_blk),
                pl.BlockSpec((nc, MLSTM_DH, L), head_blk_t), pl.BlockSpec((nc, MLSTM_DH, L), head_blk_t),
                pl.BlockSpec((2, 2, 1, nc, L), lambda b, h: (0, 0, h, b, 0)),
                pl.BlockSpec((MLSTM_DH, 1), lambda b, h: (h, 0)),
            ],
            out_specs=pl.BlockSpec((nc, MLSTM_DH, L), head_blk_t),
            out_shape=jax.ShapeDtypeStruct((T // L, mw, L), BF16),
            scratch_shapes=[pltpu.VMEM((nc, MLSTM_DH, L), F32),
                            pltpu.VMEM((2, aug, MLSTM_DH), F32),
                            row_scratch, row_scratch, row_scratch,
                            pltpu.VMEM((2, nc, 1), F32), pltpu.VMEM((2, nc, 1), F32)],
            compiler_params=_params(2),
            name="mlstm",
        )(q_m, k_m, v_m, mo_s, gates5, mlstm_out_norm_g[l].reshape(mw, 1))

        tm4 = L
        last = l == depth - 1
        gf = norm_final_g.reshape(1, D)
        weights4 = (wgab, w_branch_mla[l].astype(BF16), w_branch_mlstm[l].astype(BF16),
                    w_out[l].astype(BF16), w_mlp_up[l].astype(BF16), w_mlp_down[l].astype(BF16))
        xf = pl.pallas_call(
            functools.partial(_merge_mlp_kernel, d=D, final_norm=last),
            grid=(T // tm4,),
            in_specs=[
                pl.BlockSpec((tm4, D), row),
                pl.BlockSpec((tm4, MLA_HEADS * MLA_V), row),
                pl.BlockSpec((1, mw, tm4), lambda i: (i, 0, 0)),
                _const_spec((1, D)),
                _resident_spec(weights4[0].shape), _resident_spec(weights4[1].shape),
                _resident_spec(weights4[2].shape), _resident_spec(weights4[3].shape),
                _const_spec((1, D)),
                _resident_spec(weights4[4].shape), _resident_spec(weights4[5].shape),
                _const_spec((1, D)),
            ],
            out_specs=pl.BlockSpec((tm4, D), row),
            out_shape=jax.ShapeDtypeStruct((T, D), F32),
            compiler_params=_params(1),
            name="merge_mlp",
        )(xf, y_attn, y_mlstm_t, norm_mix_g[l].reshape(1, D), weights4[0], weights4[1], weights4[2],
          weights4[3], norm_mlp_g[l].reshape(1, D), weights4[4], weights4[5], gf)

    return xf.reshape(B, S, D)
```

```python
import functools
import math

import jax
import jax.numpy as jnp
from jax import lax
from jax.experimental import pallas as pl
from jax.experimental.pallas import tpu as pltpu

MLA_HEADS = 8
MLA_NOPE = 64
MLA_ROPE = 32
MLA_V = 64
ROPE_THETA = 10000.0
MLSTM_HEADS = 4
MLSTM_DH = 128
CONV_WIDTH = 5
NORM_EPS = 1e-6

LANES = 128
BF16_SUBLANES = 16
VMEM_LIMIT_BYTES = 56 * 1024 * 1024

HEAD_PAD = LANES
MLSTM_CHUNK = 256
HALO = BF16_SUBLANES

F32 = jnp.float32
BF16 = jnp.bfloat16


def _rms(x, g):
    return x * lax.rsqrt(jnp.mean(x * x, axis=-1, keepdims=True) + NORM_EPS) * g


def _dot(a, b):
    return jnp.dot(a, b, preferred_element_type=F32)


def _dot_nt(a, b):
    return lax.dot_general(a, b, (((1,), (1,)), ((), ())), preferred_element_type=F32)


def _dot_tn(a, b):
    return lax.dot_general(a, b, (((0,), (0,)), ((), ())), preferred_element_type=F32)


def _log_sigmoid(x):
    return jnp.minimum(x, 0.0) - jnp.log1p(jnp.exp(-jnp.abs(x)))


def _inproj_kernel(xm_ref, xp_ref, xn_ref, cos_ref, sin_ref, g_ref, w1_ref, w2t_ref, w3_ref,
                   wgt_ref, gb_ref, cw_ref, cb_ref, qg_ref, wqa_ref, wqb_ref, kvg_ref,
                   wuk_ref, wuvt_ref,
                   q_out, k_out, va_out, qm_out, km_out, vm_out, mo_out, gate_out,
                   pre_scr, *, tm, tiles_per_seq, q_lora, kv_lora, mw, q_scale, k_scale):
    i = pl.program_id(0)
    pos_in_seq = i % tiles_per_seq
    xp = jnp.where(pos_in_seq == 0, 0.0, xp_ref[...])
    xn = jnp.where(pos_in_seq == tiles_per_seq - 1, 0.0, xn_ref[...])
    xe = jnp.concatenate([xp, xm_ref[...], xn], axis=0)
    he = _rms(xe, g_ref[...]).astype(BF16)
    hm = he[HALO:HALO + tm]

    pre_scr[...] = _dot(he, w1_ref[...])
    conv = cb_ref[...]
    for j in range(CONV_WIDTH):
        conv = conv + cw_ref[j:j + 1, :] * pre_scr[pl.ds(HALO - CONV_WIDTH // 2 + j, tm), :]
    qk = conv * jax.nn.sigmoid(conv)
    qm_out[...] = qk[:, :mw].astype(BF16)
    km_out[...] = (qk[:, mw:] * k_scale).astype(BF16)

    vo_t = _dot_nt(w2t_ref[...], hm)
    vm_out[0] = vo_t[:mw].astype(BF16)
    mo_out[0] = jax.nn.sigmoid(vo_t[mw:]).astype(BF16)

    gt = _dot_nt(wgt_ref[...], hm) + gb_ref[...]
    row = lax.broadcasted_iota(jnp.int32, gt.shape, 0)
    is_f = (row % (2 * MLSTM_HEADS)) >= MLSTM_HEADS
    gate_out[...] = jnp.where(is_f, _log_sigmoid(gt), gt)

    c = _dot(hm, w3_ref[...])
    cqn = _rms(c[:, :q_lora], qg_ref[...]).astype(BF16)
    ckvn = _rms(c[:, q_lora:q_lora + kv_lora], kvg_ref[...]).astype(BF16)
    kra = c[:, q_lora + kv_lora:q_lora + kv_lora + HEAD_PAD]
    krb = c[:, q_lora + kv_lora + HEAD_PAD:]
    cos = cos_ref[...].T
    sin = sin_ref[...].T
    cos_h = jnp.concatenate([cos] * MLA_HEADS, axis=1)
    sin_h = jnp.concatenate([sin] * MLA_HEADS, axis=1)
    q = _dot(cqn, wqa_ref[...]) * cos_h + _dot(cqn, wqb_ref[...]) * sin_h
    q_out[...] = (q * q_scale).astype(BF16)
    kr = kra * cos + krb * sin
    k = _dot(ckvn, wuk_ref[...]) + jnp.concatenate([kr] * MLA_HEADS, axis=1)
    k_out[...] = k.astype(BF16)
    vt = _dot_nt(wuvt_ref[...], ckvn)
    ones = jnp.ones((HEAD_PAD - MLA_V, tm), F32)
    pieces = []
    for h in range(MLA_HEADS):
        pieces += [vt[h * MLA_V:(h + 1) * MLA_V], ones]
    va_out[0] = jnp.concatenate(pieces, axis=0).astype(BF16)


def _aligned(x, m):
    return x if isinstance(x, int) else pl.multiple_of(x, m)


def _attn_kernel(q_ref, k_ref, vt_ref, o_ref, s_0, s_1, p_0, p_1, acc_scr, *, tq, tk, vchunk, n_q, n_k,
                 bodies_per_iter):
    s_buf, p_buf = (s_0, s_1), (p_0, p_1)
    sub = tk // vchunk
    n_total = n_q * n_k
    assert n_k % 2 == 0 and n_total >= 4 and bodies_per_iter % 2 == 0

    def scores(j, par):
        qoff = _aligned((j // n_k) * tq, tq)
        koff = _aligned((j % n_k) * tk, tk)
        cmax = []
        for hh in range(2):
            cols = slice(hh * HEAD_PAD, (hh + 1) * HEAD_PAD)
            s = _dot_nt(k_ref[pl.ds(koff, tk), cols], q_ref[pl.ds(qoff, tq), cols])
            s_buf[par][hh] = s
            cmax.append(jnp.max(s, axis=0, keepdims=True))
        return tuple(cmax)

    def pv_issue(j, par):
        kb = j % n_k
        out = []
        for hh in range(2):
            vt = jnp.concatenate([vt_ref[kb * sub + c, hh * HEAD_PAD:(hh + 1) * HEAD_PAD, :]
                                  for c in range(sub)], axis=1)
            out.append(_dot(vt, p_buf[par][hh]))
        return out

    def pv_accumulate(r, alpha):
        for hh in range(2):
            acc_scr[hh] = alpha[hh] * acc_scr[hh] + r[hh]

    def softmax(j, par, m, cmax):
        first = (j % n_k) == 0
        m_out, alpha = [], []
        for hh in range(2):
            m_old = jnp.where(first, -jnp.inf, m[hh])
            m_new = jnp.maximum(m_old, cmax[hh])
            p_buf[par][hh] = jnp.exp2(s_buf[par][hh] - m_new).astype(BF16)
            alpha.append(jnp.exp2(m_old - m_new))
            m_out.append(m_new)
        return tuple(m_out), tuple(alpha)

    def finalize(j):
        qoff = _aligned((j // n_k) * tq, tq)
        o_t = jnp.concatenate([acc_scr[hh, :MLA_V] / acc_scr[hh, MLA_V:MLA_V + 1] for hh in range(2)],
                              axis=0)
        o_ref[pl.ds(qoff, tq), :] = o_t.T.astype(BF16)

    def body(k, par, m, alpha, cmax):
        r = pv_issue(k, par)
        cmax_next = scores(k + 2, par)
        m, alpha_next = softmax(k + 1, 1 - par, m, cmax)
        pv_accumulate(r, alpha)
        return m, alpha_next, cmax_next

    def steps(k0, count, m, alpha, cmax):
        for o in range(count):
            k = k0 + o
            m, alpha, cmax = body(k, o % 2, m, alpha, cmax)
            if isinstance(k0, int):
                if k % n_k == n_k - 1:
                    finalize(k)
            elif any((bodies_per_iter * t + o) % n_k == n_k - 1 for t in range(n_k)):
                pl.when(k % n_k == n_k - 1)(functools.partial(finalize, k))
        return m, alpha, cmax

    acc_scr[...] = jnp.zeros_like(acc_scr)
    m = tuple(jnp.full((1, tq), -jnp.inf, F32) for _ in range(2))
    cmax0 = scores(0, 0)
    cmax = scores(1, 1)
    m, alpha = softmax(0, 0, m, cmax0)
    n_iter, n_rest = divmod(n_total - 2, bodies_per_iter)
    m, alpha, cmax = lax.fori_loop(
        0, n_iter, lambda kk, c: steps(bodies_per_iter * kk, bodies_per_iter, *c), (m, alpha, cmax))
    m, alpha, cmax = steps(bodies_per_iter * n_iter, n_rest, m, alpha, cmax)
    r = pv_issue(n_total - 2, 0)
    m, alpha_last = softmax(n_total - 1, 1, m, cmax)
    pv_accumulate(r, alpha)
    pv_accumulate(pv_issue(n_total - 1, 1), alpha_last)
    finalize(n_total - 1)


def _scan_lanes(x, op, fill, reverse):
    n = x.shape[1]
    lane = lax.broadcasted_iota(jnp.int32, x.shape, 1)
    shift = 1
    while shift < n:
        if reverse:
            moved, valid = pltpu.roll(x, n - shift, 1), lane < n - shift
        else:
            moved, valid = pltpu.roll(x, shift, 1), lane >= shift
        x = op(x, jnp.where(valid, moved, fill))
        shift *= 2
    return x


class _Chain:
    pass


def _mlstm_kernel(q_ref, k_ref, vt_ref, mot_ref, gate_ref, og_ref, o_ref,
                  hacc, st_scr, b_scr, a_scr, cm_scr, bl_scr, am_scr, *, L, nc):
    dh = MLSTM_DH
    aug = st_scr.shape[1]
    s_idx = lax.broadcasted_iota(jnp.int32, (L, L), 0)
    t_idx = lax.broadcasted_iota(jnp.int32, (L, L), 1)
    eye = s_idx == t_idx
    visible = (s_idx <= t_idx, s_idx >= t_idx)
    ones_blk = (lax.broadcasted_iota(jnp.int32, (aug - dh, L), 0) == 0).astype(BF16)

    for d in range(2):
        logi = gate_ref[d, 0, 0]
        logf = gate_ref[d, 1, 0]
        b = _scan_lanes(logf, jnp.add, 0.0, reverse=d == 1)
        a = logi - b
        b_scr[d] = b
        a_scr[d] = a
        cm_scr[d] = _scan_lanes(a, jnp.maximum, -jnp.inf, reverse=d == 1)
        bl_scr[d] = jnp.sum(logf, axis=1, keepdims=True)
        am_scr[d] = jnp.max(a, axis=1, keepdims=True)
    st_scr[...] = jnp.zeros_like(st_scr)

    def prepare(d, c, m):
        x = _Chain()
        off = pl.multiple_of(c * L, L)
        x.q = q_ref[pl.ds(off, L), :]
        x.k = k_ref[pl.ds(off, L), :]
        x.vaug = jnp.concatenate([vt_ref[c], ones_blk], axis=0)
        x.b_row = b_scr[d, pl.ds(c, 1), :]
        x.a_row = a_scr[d, pl.ds(c, 1), :]
        b_last = bl_scr[d, pl.ds(c, 1), :]
        x.g_row = jnp.maximum(cm_scr[d, pl.ds(c, 1), :], m)
        x.iw = jnp.exp(m - x.g_row)
        x.m_new = jnp.maximum(b_last + m, b_last + am_scr[d, pl.ds(c, 1), :])
        x.decay = jnp.exp(b_last + m - x.m_new)
        x.w_row = jnp.exp(b_last + x.a_row - x.m_new)
        return x

    def issue_early(d, x):
        x.st = _dot_nt(x.k, x.q)
        x.inter = _dot_nt(st_scr[d].astype(BF16), x.q)
        x.upd = _dot((x.vaug.astype(F32) * x.w_row).astype(BF16), x.k)

    def intra(d, x):
        a_col = jnp.sum(jnp.where(eye, x.a_row, 0.0), axis=1, keepdims=True)
        e_t = jnp.exp(jnp.where(visible[d], a_col - x.g_row, -jnp.inf))
        x.r = _dot(x.vaug, (x.st * e_t).astype(BF16))

    def finish(d, x):
        num = x.iw * x.inter[:dh] + x.r[:dh]
        den = x.iw * x.inter[dh:dh + 1] + x.r[dh:dh + 1]
        floor = jnp.exp(-(x.b_row + x.g_row))
        st_scr[d] = x.decay * st_scr[d] + x.upd
        return num * (1.0 / jnp.maximum(jnp.abs(den), floor))

    def both(j, m_f, m_b):
        cf, cb = j, nc - 1 - j
        xs = (prepare(0, cf, m_f), prepare(1, cb, m_b))
        for d in range(2):
            issue_early(d, xs[d])
        for d in range(2):
            intra(d, xs[d])
        return cf, cb, finish(0, xs[0]), finish(1, xs[1]), xs[0].m_new, xs[1].m_new

    def first_touch(j, carry):
        cf, cb, h_f, h_b, m_f, m_b = both(j, *carry)
        hacc[cf] = h_f
        hacc[cb] = h_b
        return m_f, m_b

    def emit(c, h):
        tot = hacc[c] + h
        y = tot * lax.rsqrt(jnp.mean(tot * tot, axis=0, keepdims=True) + NORM_EPS) * og_ref[...]
        o_ref[c] = (y * mot_ref[c].astype(F32)).astype(BF16)

    def second_touch(j, carry):
        cf, cb, h_f, h_b, m_f, m_b = both(j, *carry)
        emit(cf, h_f)
        emit(cb, h_b)
        return m_f, m_b

    m0 = jnp.zeros((1, 1), F32)
    carry = lax.fori_loop(0, nc // 2, first_touch, (m0, m0))
    lax.fori_loop(nc // 2, nc, second_touch, carry)


def _merge_mlp_kernel(x_ref, ya_ref, ymt_ref, g1_ref, wgab_ref, wbm_ref, wbl_ref, wout_ref,
                      g2_ref, wup_ref, wdn_ref, gf_ref, o_ref, *, d, final_norm):
    x = x_ref[...]
    hn = _rms(x, g1_ref[...]).astype(BF16)
    gates = jax.nn.sigmoid(_dot(hn, wgab_ref[...]))
    merged = (gates[:, :d] * _dot(ya_ref[...], wbm_ref[...])
              + gates[:, d:] * _dot_tn(ymt_ref[0], wbl_ref[...]))
    x1 = x + _dot(merged.astype(BF16), wout_ref[...])
    u = _dot(_rms(x1, g2_ref[...]).astype(BF16), wup_ref[...])
    r = jnp.maximum(u, 0.0)
    x2 = x1 + _dot((r * r).astype(BF16), wdn_ref[...])
    o_ref[...] = _rms(x2, gf_ref[...]) if final_norm else x2


def _const_spec(shape):
    return pl.BlockSpec(shape, lambda *_: (0,) * len(shape))


def _resident_spec(shape):
    return pl.BlockSpec(shape, lambda *_: (0,) * len(shape), pipeline_mode=pl.Buffered(1))


def _params(n_axes):
    return pltpu.CompilerParams(dimension_semantics=("arbitrary",) * n_axes,
                                vmem_limit_bytes=VMEM_LIMIT_BYTES)


def _rope_kernel(pos_ref, freq_ref, cos_out, sin_out):
    t = pos_ref.shape[1]
    ang = freq_ref[...] * pos_ref[...]
    cos, sin = jnp.cos(ang), jnp.sin(ang)
    pad = jnp.zeros((HEAD_PAD - MLA_NOPE - MLA_ROPE, t), F32)
    cos_out[...] = jnp.concatenate([jnp.ones((MLA_NOPE, t), F32), cos, cos, pad], axis=0)
    sin_out[...] = jnp.concatenate([jnp.zeros((MLA_NOPE, t), F32), -sin, sin, pad], axis=0)


def _rope_tables(positions):
    t = positions.size
    inv_freq = ROPE_THETA ** (-jnp.arange(0, MLA_ROPE, 2, dtype=F32) / MLA_ROPE)
    table = jax.ShapeDtypeStruct((HEAD_PAD, t), F32)
    tile = min(t, 16 * LANES)
    return pl.pallas_call(
        _rope_kernel,
        grid=(t // tile,),
        in_specs=[pl.BlockSpec((1, tile), lambda i: (0, i)), _const_spec((MLA_ROPE // 2, 1))],
        out_specs=[pl.BlockSpec((HEAD_PAD, tile), lambda i: (0, i))] * 2,
        out_shape=[table, table], compiler_params=_params(1), name="rope_table",
    )(positions.astype(F32).reshape(1, t), inv_freq.reshape(-1, 1))


def _pad_heads(w, width_in, offset_out):
    k = w.shape[0]
    w = w.reshape(k, MLA_HEADS, width_in)
    w = jnp.pad(w, ((0, 0), (0, 0), (offset_out, HEAD_PAD - width_in - offset_out)))
    return w.reshape(k, MLA_HEADS * HEAD_PAD)


def kernel(x, positions, norm_mix_g, w_in, mla_q_norm_g, mla_w_uq, mla_kv_norm_g, mla_w_ukv, mlstm_conv_w, mlstm_conv_b, mlstm_igate_b, mlstm_fgate_b, mlstm_out_norm_g, w_branch_mla, w_branch_mlstm, w_out, norm_mlp_g, w_mlp_up, w_mlp_down, norm_final_g):
    B, S, D = x.shape
    T = B * S
    depth = w_in.shape[0]
    q_lora = mla_q_norm_g.shape[1]
    kv_lora = mla_kv_norm_g.shape[1]
    mw = MLSTM_HEADS * MLSTM_DH
    half = MLA_ROPE // 2
    n_gate = 4 * MLSTM_HEADS
    d_ff = w_mlp_up.shape[2]
    L = MLSTM_CHUNK
    nc = S // L
    assert S % L == 0 and MLA_HEADS % 2 == 0

    cos_t, sin_t = _rope_tables(positions)
    xf = x.reshape(T, D)

    for l in range(depth):
        offs, o = [], 0
        for w in (q_lora, kv_lora, MLA_ROPE, mw, mw, mw, mw, n_gate, D, D):
            offs.append(o)
            o += w
        wl = w_in[l]
        w_cq = wl[:, offs[0]:offs[0] + q_lora]
        w_ckv = wl[:, offs[1]:offs[1] + kv_lora]
        w_kr = wl[:, offs[2]:offs[2] + MLA_ROPE]
        w_kr_swapped = jnp.concatenate([w_kr[:, half:], w_kr[:, :half]], axis=1)
        pad_kr = ((0, 0), (MLA_NOPE, HEAD_PAD - MLA_NOPE - MLA_ROPE))
        w1 = wl[:, offs[3]:offs[3] + 2 * mw].astype(BF16)
        w2t = wl[:, offs[5]:offs[5] + 2 * mw].T.astype(BF16)
        w3 = jnp.concatenate([w_cq, w_ckv, jnp.pad(w_kr, pad_kr), jnp.pad(w_kr_swapped, pad_kr)],
                             axis=1).astype(BF16)
        wgt = wl[:, offs[7]:offs[7] + n_gate].T.astype(BF16)
        wgab = wl[:, offs[8]:offs[8] + 2 * D].astype(BF16)
        gate_b = jnp.stack([mlstm_igate_b[l], mlstm_fgate_b[l]], axis=1).reshape(n_gate, 1)

        dqk = MLA_NOPE + MLA_ROPE
        wq = mla_w_uq[l].reshape(q_lora, MLA_HEADS, dqk)
        wq_rope = wq[:, :, MLA_NOPE:]
        wq_swapped = jnp.concatenate([jnp.zeros_like(wq[:, :, :MLA_NOPE]),
                                      wq_rope[:, :, half:], wq_rope[:, :, :half]], axis=2)
        wqa = _pad_heads(wq.reshape(q_lora, -1), dqk, 0).astype(BF16)
        wqb = _pad_heads(wq_swapped.reshape(q_lora, -1), dqk, 0).astype(BF16)
        wkv = mla_w_ukv[l].reshape(kv_lora, MLA_HEADS, MLA_NOPE + MLA_V)
        wuk = _pad_heads(wkv[:, :, :MLA_NOPE].reshape(kv_lora, -1), MLA_NOPE, 0).astype(BF16)
        wuvt = wkv[:, :, MLA_NOPE:].reshape(kv_lora, MLA_HEADS * MLA_V).T.astype(BF16)

        cw = jnp.pad(mlstm_conv_w[l].reshape(CONV_WIDTH, 2 * mw), ((0, 8 - CONV_WIDTH), (0, 0)))
        cb = mlstm_conv_b[l].reshape(1, 2 * mw)

        tm = 256
        tiles_per_seq = S // tm
        hb = tm // HALO
        n_halo_blocks = T // HALO
        row = lambda i: (i, 0)
        col = lambda i: (0, i)
        chunk = lambda i: (i, 0, 0)
        kern = functools.partial(
            _inproj_kernel, tm=tm, tiles_per_seq=tiles_per_seq, q_lora=q_lora, kv_lora=kv_lora, mw=mw,
            q_scale=(MLA_NOPE + MLA_ROPE) ** -0.5 * math.log2(math.e), k_scale=MLSTM_DH ** -0.5)
        outs = pl.pallas_call(
            kern,
            grid=(T // tm,),
            in_specs=[
                pl.BlockSpec((tm, D), row),
                pl.BlockSpec((HALO, D), lambda i: (jnp.maximum(i * hb - 1, 0), 0)),
                pl.BlockSpec((HALO, D), lambda i: (jnp.minimum((i + 1) * hb, n_halo_blocks - 1), 0)),
                pl.BlockSpec((HEAD_PAD, tm), col),
                pl.BlockSpec((HEAD_PAD, tm), col),
                _const_spec((1, D)),
                _const_spec(w1.shape), _const_spec(w2t.shape), _const_spec(w3.shape),
                _const_spec(wgt.shape), _const_spec(gate_b.shape),
                _const_spec(cw.shape), _const_spec(cb.shape),
                _const_spec((1, q_lora)), _const_spec(wqa.shape), _const_spec(wqb.shape),
                _const_spec((1, kv_lora)), _const_spec(wuk.shape), _const_spec(wuvt.shape),
            ],
            out_specs=[
                pl.BlockSpec((tm, MLA_HEADS * HEAD_PAD), row),
                pl.BlockSpec((tm, MLA_HEADS * HEAD_PAD), row),
                pl.BlockSpec((1, MLA_HEADS * HEAD_PAD, tm), chunk),
                pl.BlockSpec((tm, mw), row), pl.BlockSpec((tm, mw), row),
                pl.BlockSpec((1, mw, tm), chunk),
                pl.BlockSpec((1, mw, tm), chunk),
                pl.BlockSpec((n_gate, tm), col),
            ],
            out_shape=[
                jax.ShapeDtypeStruct((T, MLA_HEADS * HEAD_PAD), BF16),
                jax.ShapeDtypeStruct((T, MLA_HEADS * HEAD_PAD), BF16),
                jax.ShapeDtypeStruct((T // tm, MLA_HEADS * HEAD_PAD, tm), BF16),
                jax.ShapeDtypeStruct((T, mw), BF16), jax.ShapeDtypeStruct((T, mw), BF16),
                jax.ShapeDtypeStruct((T // tm, mw, tm), BF16),
                jax.ShapeDtypeStruct((T // tm, mw, tm), BF16),
                jax.ShapeDtypeStruct((n_gate, T), F32),
            ],
            scratch_shapes=[pltpu.VMEM((tm + 2 * HALO, 2 * mw), F32)],
            compiler_params=_params(1),
            name="inproj",
        )(xf, xf, xf, cos_t, sin_t, norm_mix_g[l].reshape(1, D), w1, w2t, w3, wgt, gate_b, cw, cb,
          mla_q_norm_g[l].reshape(1, q_lora), wqa, wqb, mla_kv_norm_g[l].reshape(1, kv_lora), wuk, wuvt)
        q_a, k_a, v_a, q_m, k_m, v_m, mo_s, gates = outs

        tq, tk = 256, 1024
        vchunk = tm
        y_attn = pl.pallas_call(
            functools.partial(_attn_kernel, tq=tq, tk=tk, vchunk=vchunk, n_q=S // tq, n_k=S // tk,
                              bodies_per_iter=2),
            grid=(B, MLA_HEADS // 2),
            in_specs=[
                pl.BlockSpec((S, 2 * HEAD_PAD), lambda b, p: (b, p)),
                pl.BlockSpec((S, 2 * HEAD_PAD), lambda b, p: (b, p)),
                pl.BlockSpec((S // vchunk, 2 * HEAD_PAD, vchunk), lambda b, p: (b, p, 0)),
            ],
            out_specs=pl.BlockSpec((S, 2 * MLA_V), lambda b, p: (b, p)),
            out_shape=jax.ShapeDtypeStruct((T, MLA_HEADS * MLA_V), BF16),
            scratch_shapes=[pltpu.VMEM((2, tk, tq), F32), pltpu.VMEM((2, tk, tq), F32),
                            pltpu.VMEM((2, tk, tq), BF16), pltpu.VMEM((2, tk, tq), BF16),
                            pltpu.VMEM((2, HEAD_PAD, tq), F32)],
            compiler_params=_params(2),
            name="mla_attn",
        )(q_a, k_a, v_a)

        gates5 = gates.reshape(2, 2, MLSTM_HEADS, T // L, L)
        assert tm == L
        head_blk = lambda b, h: (b, h)
        head_blk_t = lambda b, h: (b, h, 0)
        aug = MLSTM_DH + BF16_SUBLANES
        row_scratch = pltpu.VMEM((2, nc, L), F32)
        y_mlstm_t = pl.pallas_call(
            functools.partial(_mlstm_kernel, L=L, nc=nc),
            grid=(B, MLSTM_HEADS),
            in_specs=[
                pl.BlockSpec((S, MLSTM_DH), head_blk), pl.BlockSpec((S, MLSTM_DH), head_blk),
                pl.BlockSpec((nc, MLSTM_DH, L), head_blk_t), pl.BlockSpec((nc, MLSTM_DH, L), head_blk_t),
                pl.BlockSpec((2, 2, 1, nc, L), lambda b, h: (0, 0, h, b, 0)),
                pl.BlockSpec((MLSTM_DH, 1), lambda b, h: (h, 0)),
            ],
            out_specs=pl.BlockSpec((nc, MLSTM_DH, L), head_blk_t),
            out_shape=jax.ShapeDtypeStruct((T // L, mw, L), BF16),
            scratch_shapes=[pltpu.VMEM((nc, MLSTM_DH, L), F32),
                            pltpu.VMEM((2, aug, MLSTM_DH), F32),
                            row_scratch, row_scratch, row_scratch,
                            pltpu.VMEM((2, nc, 1), F32), pltpu.VMEM((2, nc, 1), F32)],
            compiler_params=_params(2),
            name="mlstm",
        )(q_m, k_m, v_m, mo_s, gates5, mlstm_out_norm_g[l].reshape(mw, 1))

        tm4 = L
        last = l == depth - 1
        gf = norm_final_g.reshape(1, D)
        weights4 = (wgab, w_branch_mla[l].astype(BF16), w_branch_mlstm[l].astype(BF16),
                    w_out[l].astype(BF16), w_mlp_up[l].astype(BF16), w_mlp_down[l].astype(BF16))
        xf = pl.pallas_call(
            functools.partial(_merge_mlp_kernel, d=D, final_norm=last),
            grid=(T // tm4,),
            in_specs=[
                pl.BlockSpec((tm4, D), row),
                pl.BlockSpec((tm4, MLA_HEADS * MLA_V), row),
                pl.BlockSpec((1, mw, tm4), lambda i: (i, 0, 0)),
                _const_spec((1, D)),
                _resident_spec(weights4[0].shape), _resident_spec(weights4[1].shape),
                _resident_spec(weights4[2].shape), _resident_spec(weights4[3].shape),
                _const_spec((1, D)),
                _resident_spec(weights4[4].shape), _resident_spec(weights4[5].shape),
                _const_spec((1, D)),
            ],
            out_specs=pl.BlockSpec((tm4, D), row),
            out_shape=jax.ShapeDtypeStruct((T, D), F32),
            compiler_params=_params(1),
            name="merge_mlp",
        )(xf, y_attn, y_mlstm_t, norm_mix_g[l].reshape(1, D), weights4[0], weights4[1], weights4[2],
          weights4[3], norm_mlp_g[l].reshape(1, D), weights4[4], weights4[5], gf)

    return xf.reshape(B, S, D)
```

```python
import functools
import math

import jax
import jax.numpy as jnp
from jax import lax
from jax.experimental import pallas as pl
from jax.experimental.pallas import tpu as pltpu

MLA_HEADS = 8
MLA_NOPE = 64
MLA_ROPE = 32
MLA_V = 64
ROPE_THETA = 10000.0
MLSTM_HEADS = 4
MLSTM_DH = 128
CONV_WIDTH = 5
NORM_EPS = 1e-6

LANES = 128
BF16_SUBLANES = 16
VMEM_LIMIT_BYTES = 56 * 1024 * 1024

HEAD_PAD = LANES
MLSTM_CHUNK = 256
HALO = BF16_SUBLANES

F32 = jnp.float32
BF16 = jnp.bfloat16


def _rms(x, g):
    return x * lax.rsqrt(jnp.mean(x * x, axis=-1, keepdims=True) + NORM_EPS) * g


def _dot(a, b):
    return jnp.dot(a, b, preferred_element_type=F32)


def _dot_nt(a, b):
    return lax.dot_general(a, b, (((1,), (1,)), ((), ())), preferred_element_type=F32)


def _dot_tn(a, b):
    return lax.dot_general(a, b, (((0,), (0,)), ((), ())), preferred_element_type=F32)


def _log_sigmoid(x):
    return jnp.minimum(x, 0.0) - jnp.log1p(jnp.exp(-jnp.abs(x)))


def _inproj_kernel(xm_ref, xp_ref, xn_ref, cos_ref, sin_ref, g_ref, w1_ref, w2t_ref, w3_ref,
                   wgt_ref, gb_ref, cw_ref, cb_ref, qg_ref, wqa_ref, wqb_ref, kvg_ref,
                   wuk_ref, wuvt_ref,
                   q_out, k_out, va_out, qm_out, km_out, vm_out, mo_out, gate_out,
                   pre_scr, *, tm, tiles_per_seq, q_lora, kv_lora, mw, q_scale, k_scale):
    i = pl.program_id(0)
    pos_in_seq = i % tiles_per_seq
    xp = jnp.where(pos_in_seq == 0, 0.0, xp_ref[...])
    xn = jnp.where(pos_in_seq == tiles_per_seq - 1, 0.0, xn_ref[...])
    xe = jnp.concatenate([xp, xm_ref[...], xn], axis=0)
    he = _rms(xe, g_ref[...]).astype(BF16)
    hm = he[HALO:HALO + tm]

    pre_scr[...] = _dot(he, w1_ref[...])
    conv = cb_ref[...]
    for j in range(CONV_WIDTH):
        conv = conv + cw_ref[j:j + 1, :] * pre_scr[pl.ds(HALO - CONV_WIDTH // 2 + j, tm), :]
    qk = conv * jax.nn.sigmoid(conv)
    qm_out[...] = qk[:, :mw].astype(BF16)
    km_out[...] = (qk[:, mw:] * k_scale).astype(BF16)

    vo_t = _dot_nt(w2t_ref[...], hm)
    vm_out[0] = vo_t[:mw].astype(BF16)
    mo_out[0] = jax.nn.sigmoid(vo_t[mw:]).astype(BF16)

    gt = _dot_nt(wgt_ref[...], hm) + gb_ref[...]
    row = lax.broadcasted_iota(jnp.int32, gt.shape, 0)
    is_f = (row % (2 * MLSTM_HEADS)) >= MLSTM_HEADS
    gate_out[...] = jnp.where(is_f, _log_sigmoid(gt), gt)

    c = _dot(hm, w3_ref[...])
    cqn = _rms(c[:, :q_lora], qg_ref[...]).astype(BF16)
    ckvn = _rms(c[:, q_lora:q_lora + kv_lora], kvg_ref[...]).astype(BF16)
    kra = c[:, q_lora + kv_lora:q_lora + kv_lora + HEAD_PAD]
    krb = c[:, q_lora + kv_lora + HEAD_PAD:]
    cos = cos_ref[...].T
    sin = sin_ref[...].T
    cos_h = jnp.concatenate([cos] * MLA_HEADS, axis=1)
    sin_h = jnp.concatenate([sin] * MLA_HEADS, axis=1)
    q = _dot(cqn, wqa_ref[...]) * cos_h + _dot(cqn, wqb_ref[...]) * sin_h
    q_out[...] = (q * q_scale).astype(BF16)
    kr = kra * cos + krb * sin
    k = _dot(ckvn, wuk_ref[...]) + jnp.concatenate([kr] * MLA_HEADS, axis=1)
    k_out[...] = k.astype(BF16)
    vt = _dot_nt(wuvt_ref[...], ckvn)
    ones = jnp.ones((HEAD_PAD - MLA_V, tm), F32)
    pieces = []
    for h in range(MLA_HEADS):
        pieces += [vt[h * MLA_V:(h + 1) * MLA_V], ones]
    va_out[0] = jnp.concatenate(pieces, axis=0).astype(BF16)


def _aligned(x, m):
    return x if isinstance(x, int) else pl.multiple_of(x, m)


def _attn_kernel(q_ref, k_ref, vt_ref, o_ref, s_0, s_1, p_0, p_1, p_2, p_3, acc_scr, *,
                 tq, tk, vchunk, n_q, n_k):
    s_buf, p_buf = (s_0, s_1), (p_0, p_1, p_2, p_3)
    n_slots = len(p_buf)
    sub = tk // vchunk
    n_total = n_q * n_k
    assert n_k % n_slots == 0 and n_slots % 2 == 0 and n_total >= 6

    def scores(j, par):
        qoff = _aligned((j // n_k) * tq, tq)
        koff = _aligned((j % n_k) * tk, tk)
        cmax = []
        for hh in range(2):
            cols = slice(hh * HEAD_PAD, (hh + 1) * HEAD_PAD)
            s = _dot_nt(k_ref[pl.ds(koff, tk), cols], q_ref[pl.ds(qoff, tq), cols])
            s_buf[par][hh] = s
            cmax.append(jnp.max(s, axis=0, keepdims=True))
        return tuple(cmax)

    def pv_issue(j, slot):
        kb = j % n_k
        out = []
        for hh in range(2):
            vt = jnp.concatenate([vt_ref[kb * sub + c, hh * HEAD_PAD:(hh + 1) * HEAD_PAD, :]
                                  for c in range(sub)], axis=1)
            out.append(_dot(vt, p_buf[slot][hh]))
        return out

    def pv_accumulate(r, alpha):
        for hh in range(2):
            acc_scr[hh] = alpha[hh] * acc_scr[hh] + r[hh]

    def softmax(j, par, slot, m, cmax):
        first = (j % n_k) == 0
        m_out, alpha = [], []
        for hh in range(2):
            m_old = jnp.where(first, -jnp.inf, m[hh])
            m_new = jnp.maximum(m_old, cmax[hh])
            p_buf[slot][hh] = jnp.exp2(s_buf[par][hh] - m_new).astype(BF16)
            alpha.append(jnp.exp2(m_old - m_new))
            m_out.append(m_new)
        return tuple(m_out), tuple(alpha)

    def finalize(j):
        qoff = _aligned((j // n_k) * tq, tq)
        o_t = jnp.concatenate([acc_scr[hh, :MLA_V] / acc_scr[hh, MLA_V:MLA_V + 1] for hh in range(2)],
                              axis=0)
        o_ref[pl.ds(qoff, tq), :] = o_t.T.astype(BF16)

    def pv(j, alpha, r=None):
        pv_accumulate(pv_issue(j, j % n_slots) if r is None else r, alpha)
        if isinstance(j, int) and j % n_k == n_k - 1:
            finalize(j)

    def body(k, kmod, m, alpha_prev, alpha_cur, cmax):
        par = kmod % 2
        r = pv_issue(k - 1, (kmod - 1) % n_slots)
        cmax_next = scores(k + 2, par)
        m, alpha_next = softmax(k + 1, 1 - par, (kmod + 1) % n_slots, m, cmax)
        pv(k - 1, alpha_prev, r)
        return m, alpha_cur, alpha_next, cmax_next

    def loop_body(kk, carry):
        k0 = n_slots * kk + 1
        for o in range(n_slots):
            carry = body(k0 + o, (1 + o) % n_slots, *carry)
        last = k0 + n_slots - 2
        pl.when(last % n_k == n_k - 1)(functools.partial(finalize, last))
        return carry

    acc_scr[...] = jnp.zeros_like(acc_scr)
    m = tuple(jnp.full((1, tq), -jnp.inf, F32) for _ in range(2))
    cmax0 = scores(0, 0)
    cmax1 = scores(1, 1)
    m, alpha0 = softmax(0, 0, 0, m, cmax0)
    cmax2 = scores(2, 0)
    m, alpha1 = softmax(1, 1, 1, m, cmax1)
    n_iter = (n_total - 3) // n_slots
    carry = lax.fori_loop(0, n_iter, loop_body, (m, alpha0, alpha1, cmax2))
    k = n_slots * n_iter + 1
    while k <= n_total - 3:
        carry = body(k, k % n_slots, *carry)
        k += 1
    m, alpha_prev, alpha_cur, cmax = carry
    r = pv_issue(k - 1, (k - 1) % n_slots)
    m, alpha_last = softmax(k + 1, (k + 1) % 2, (k + 1) % n_slots, m, cmax)
    pv(k - 1, alpha_prev, r)
    pv(n_total - 2, alpha_cur)
    pv(n_total - 1, alpha_last)


def _scan_lanes(x, op, fill, reverse):
    n = x.shape[1]
    lane = lax.broadcasted_iota(jnp.int32, x.shape, 1)
    shift = 1
    while shift < n:
        if reverse:
            moved, valid = pltpu.roll(x, n - shift, 1), lane < n - shift
        else:
            moved, valid = pltpu.roll(x, shift, 1), lane >= shift
        x = op(x, jnp.where(valid, moved, fill))
        shift *= 2
    return x


class _Chain:
    pass


def _mlstm_kernel(q_ref, k_ref, vt_ref, mot_ref, gate_ref, og_ref, o_ref,
                  hacc, st_scr, b_scr, a_scr, cm_scr, bl_scr, am_scr, *, L, nc):
    dh = MLSTM_DH
    aug = st_scr.shape[1]
    s_idx = lax.broadcasted_iota(jnp.int32, (L, L), 0)
    t_idx = lax.broadcasted_iota(jnp.int32, (L, L), 1)
    eye = s_idx == t_idx
    visible = (s_idx <= t_idx, s_idx >= t_idx)
    ones_blk = (lax.broadcasted_iota(jnp.int32, (aug - dh, L), 0) == 0).astype(BF16)

    for d in range(2):
        logi = gate_ref[d, 0, 0]
        logf = gate_ref[d, 1, 0]
        b = _scan_lanes(logf, jnp.add, 0.0, reverse=d == 1)
        a = logi - b
        b_scr[d] = b
        a_scr[d] = a
        cm_scr[d] = _scan_lanes(a, jnp.maximum, -jnp.inf, reverse=d == 1)
        bl_scr[d] = jnp.sum(logf, axis=1, keepdims=True)
        am_scr[d] = jnp.max(a, axis=1, keepdims=True)
    st_scr[...] = jnp.zeros_like(st_scr)

    def prepare(d, c, m):
        x = _Chain()
        off = pl.multiple_of(c * L, L)
        x.q = q_ref[pl.ds(off, L), :]
        x.k = k_ref[pl.ds(off, L), :]
        x.vaug = jnp.concatenate([vt_ref[c], ones_blk], axis=0)
        x.b_row = b_scr[d, pl.ds(c, 1), :]
        x.a_row = a_scr[d, pl.ds(c, 1), :]
        b_last = bl_scr[d, pl.ds(c, 1), :]
        x.g_row = jnp.maximum(cm_scr[d, pl.ds(c, 1), :], m)
        x.iw = jnp.exp(m - x.g_row)
        x.m_new = jnp.maximum(b_last + m, b_last + am_scr[d, pl.ds(c, 1), :])
        x.decay = jnp.exp(b_last + m - x.m_new)
        x.w_row = jnp.exp(b_last + x.a_row - x.m_new)
        return x

    def issue_early(d, x):
        x.st = _dot_nt(x.k, x.q)
        x.inter = _dot_nt(st_scr[d].astype(BF16), x.q)
        x.upd = _dot((x.vaug.astype(F32) * x.w_row).astype(BF16), x.k)

    def intra(d, x):
        a_col = jnp.sum(jnp.where(eye, x.a_row, 0.0), axis=1, keepdims=True)
        e_t = jnp.exp(jnp.where(visible[d], a_col - x.g_row, -jnp.inf))
        x.r = _dot(x.vaug, (x.st * e_t).astype(BF16))

    def finish(d, x):
        num = x.iw * x.inter[:dh] + x.r[:dh]
        den = x.iw * x.inter[dh:dh + 1] + x.r[dh:dh + 1]
        floor = jnp.exp(-(x.b_row + x.g_row))
        st_scr[d] = x.decay * st_scr[d] + x.upd
        return num * (1.0 / jnp.maximum(jnp.abs(den), floor))

    def both(j, m_f, m_b):
        cf, cb = j, nc - 1 - j
        xs = (prepare(0, cf, m_f), prepare(1, cb, m_b))
        for d in range(2):
            issue_early(d, xs[d])
        for d in range(2):
            intra(d, xs[d])
        return cf, cb, finish(0, xs[0]), finish(1, xs[1]), xs[0].m_new, xs[1].m_new

    def first_touch(j, carry):
        cf, cb, h_f, h_b, m_f, m_b = both(j, *carry)
        hacc[cf] = h_f
        hacc[cb] = h_b
        return m_f, m_b

    def emit(c, h):
        tot = hacc[c] + h
        y = tot * lax.rsqrt(jnp.mean(tot * tot, axis=0, keepdims=True) + NORM_EPS) * og_ref[...]
        o_ref[c] = (y * mot_ref[c].astype(F32)).astype(BF16)

    def second_touch(j, carry):
        cf, cb, h_f, h_b, m_f, m_b = both(j, *carry)
        emit(cf, h_f)
        emit(cb, h_b)
        return m_f, m_b

    m0 = jnp.zeros((1, 1), F32)
    carry = lax.fori_loop(0, nc // 2, first_touch, (m0, m0))
    lax.fori_loop(nc // 2, nc, second_touch, carry)


def _merge_mlp_kernel(x_ref, ya_ref, ymt_ref, g1_ref, wgab_ref, wbm_ref, wbl_ref, wout_ref,
                      g2_ref, wup_ref, wdn_ref, gf_ref, o_ref, *, d, final_norm):
    x = x_ref[...]
    hn = _rms(x, g1_ref[...]).astype(BF16)
    gates = jax.nn.sigmoid(_dot(hn, wgab_ref[...]))
    merged = (gates[:, :d] * _dot(ya_ref[...], wbm_ref[...])
              + gates[:, d:] * _dot_tn(ymt_ref[0], wbl_ref[...]))
    x1 = x + _dot(merged.astype(BF16), wout_ref[...])
    u = _dot(_rms(x1, g2_ref[...]).astype(BF16), wup_ref[...])
    r = jnp.maximum(u, 0.0)
    x2 = x1 + _dot((r * r).astype(BF16), wdn_ref[...])
    o_ref[...] = _rms(x2, gf_ref[...]) if final_norm else x2


def _const_spec(shape):
    return pl.BlockSpec(shape, lambda *_: (0,) * len(shape))


def _resident_spec(shape):
    return pl.BlockSpec(shape, lambda *_: (0,) * len(shape), pipeline_mode=pl.Buffered(1))


def _params(n_axes):
    return pltpu.CompilerParams(dimension_semantics=("arbitrary",) * n_axes,
                                vmem_limit_bytes=VMEM_LIMIT_BYTES)


def _rope_kernel(pos_ref, freq_ref, cos_out, sin_out):
    t = pos_ref.shape[1]
    ang = freq_ref[...] * pos_ref[...]
    cos, sin = jnp.cos(ang), jnp.sin(ang)
    pad = jnp.zeros((HEAD_PAD - MLA_NOPE - MLA_ROPE, t), F32)
    cos_out[...] = jnp.concatenate([jnp.ones((MLA_NOPE, t), F32), cos, cos, pad], axis=0)
    sin_out[...] = jnp.concatenate([jnp.zeros((MLA_NOPE, t), F32), -sin, sin, pad], axis=0)


def _rope_tables(positions):
    t = positions.size
    inv_freq = ROPE_THETA ** (-jnp.arange(0, MLA_ROPE, 2, dtype=F32) / MLA_ROPE)
    table = jax.ShapeDtypeStruct((HEAD_PAD, t), F32)
    tile = min(t, 16 * LANES)
    return pl.pallas_call(
        _rope_kernel,
        grid=(t // tile,),
        in_specs=[pl.BlockSpec((1, tile), lambda i: (0, i)), _const_spec((MLA_ROPE // 2, 1))],
        out_specs=[pl.BlockSpec((HEAD_PAD, tile), lambda i: (0, i))] * 2,
        out_shape=[table, table], compiler_params=_params(1), name="rope_table",
    )(positions.astype(F32).reshape(1, t), inv_freq.reshape(-1, 1))


def _pad_heads(w, width_in, offset_out):
    k = w.shape[0]
    w = w.reshape(k, MLA_HEADS, width_in)
    w = jnp.pad(w, ((0, 0), (0, 0), (offset_out, HEAD_PAD - width_in - offset_out)))
    return w.reshape(k, MLA_HEADS * HEAD_PAD)


def kernel(x, positions, norm_mix_g, w_in, mla_q_norm_g, mla_w_uq, mla_kv_norm_g, mla_w_ukv, mlstm_conv_w, mlstm_conv_b, mlstm_igate_b, mlstm_fgate_b, mlstm_out_norm_g, w_branch_mla, w_branch_mlstm, w_out, norm_mlp_g, w_mlp_up, w_mlp_down, norm_final_g):
    B, S, D = x.shape
    T = B * S
    depth = w_in.shape[0]
    q_lora = mla_q_norm_g.shape[1]
    kv_lora = mla_kv_norm_g.shape[1]
    mw = MLSTM_HEADS * MLSTM_DH
    half = MLA_ROPE // 2
    n_gate = 4 * MLSTM_HEADS
    d_ff = w_mlp_up.shape[2]
    L = MLSTM_CHUNK
    nc = S // L
    assert S % L == 0 and MLA_HEADS % 2 == 0

    cos_t, sin_t = _rope_tables(positions)
    xf = x.reshape(T, D)

    for l in range(depth):
        offs, o = [], 0
        for w in (q_lora, kv_lora, MLA_ROPE, mw, mw, mw, mw, n_gate, D, D):
            offs.append(o)
            o += w
        wl = w_in[l]
        w_cq = wl[:, offs[0]:offs[0] + q_lora]
        w_ckv = wl[:, offs[1]:offs[1] + kv_lora]
        w_kr = wl[:, offs[2]:offs[2] + MLA_ROPE]
        w_kr_swapped = jnp.concatenate([w_kr[:, half:], w_kr[:, :half]], axis=1)
        pad_kr = ((0, 0), (MLA_NOPE, HEAD_PAD - MLA_NOPE - MLA_ROPE))
        w1 = wl[:, offs[3]:offs[3] + 2 * mw].astype(BF16)
        w2t = wl[:, offs[5]:offs[5] + 2 * mw].T.astype(BF16)
        w3 = jnp.concatenate([w_cq, w_ckv, jnp.pad(w_kr, pad_kr), jnp.pad(w_kr_swapped, pad_kr)],
                             axis=1).astype(BF16)
        wgt = wl[:, offs[7]:offs[7] + n_gate].T.astype(BF16)
        wgab = wl[:, offs[8]:offs[8] + 2 * D].astype(BF16)
        gate_b = jnp.stack([mlstm_igate_b[l], mlstm_fgate_b[l]], axis=1).reshape(n_gate, 1)

        dqk = MLA_NOPE + MLA_ROPE
        wq = mla_w_uq[l].reshape(q_lora, MLA_HEADS, dqk)
        wq_rope = wq[:, :, MLA_NOPE:]
        wq_swapped = jnp.concatenate([jnp.zeros_like(wq[:, :, :MLA_NOPE]),
                                      wq_rope[:, :, half:], wq_rope[:, :, :half]], axis=2)
        wqa = _pad_heads(wq.reshape(q_lora, -1), dqk, 0).astype(BF16)
        wqb = _pad_heads(wq_swapped.reshape(q_lora, -1), dqk, 0).astype(BF16)
        wkv = mla_w_ukv[l].reshape(kv_lora, MLA_HEADS, MLA_NOPE + MLA_V)
        wuk = _pad_heads(wkv[:, :, :MLA_NOPE].reshape(kv_lora, -1), MLA_NOPE, 0).astype(BF16)
        wuvt = wkv[:, :, MLA_NOPE:].reshape(kv_lora, MLA_HEADS * MLA_V).T.astype(BF16)

        cw = jnp.pad(mlstm_conv_w[l].reshape(CONV_WIDTH, 2 * mw), ((0, 8 - CONV_WIDTH), (0, 0)))
        cb = mlstm_conv_b[l].reshape(1, 2 * mw)

        tm = 256
        tiles_per_seq = S // tm
        hb = tm // HALO
        n_halo_blocks = T // HALO
        row = lambda i: (i, 0)
        col = lambda i: (0, i)
        chunk = lambda i: (i, 0, 0)
        kern = functools.partial(
            _inproj_kernel, tm=tm, tiles_per_seq=tiles_per_seq, q_lora=q_lora, kv_lora=kv_lora, mw=mw,
            q_scale=(MLA_NOPE + MLA_ROPE) ** -0.5 * math.log2(math.e), k_scale=MLSTM_DH ** -0.5)
        outs = pl.pallas_call(
            kern,
            grid=(T // tm,),
            in_specs=[
                pl.BlockSpec((tm, D), row),
                pl.BlockSpec((HALO, D), lambda i: (jnp.maximum(i * hb - 1, 0), 0)),
                pl.BlockSpec((HALO, D), lambda i: (jnp.minimum((i + 1) * hb, n_halo_blocks - 1), 0)),
                pl.BlockSpec((HEAD_PAD, tm), col),
                pl.BlockSpec((HEAD_PAD, tm), col),
                _const_spec((1, D)),
                _const_spec(w1.shape), _const_spec(w2t.shape), _const_spec(w3.shape),
                _const_spec(wgt.shape), _const_spec(gate_b.shape),
                _const_spec(cw.shape), _const_spec(cb.shape),
                _const_spec((1, q_lora)), _const_spec(wqa.shape), _const_spec(wqb.shape),
                _const_spec((1, kv_lora)), _const_spec(wuk.shape), _const_spec(wuvt.shape),
            ],
            out_specs=[
                pl.BlockSpec((tm, MLA_HEADS * HEAD_PAD), row),
                pl.BlockSpec((tm, MLA_HEADS * HEAD_PAD), row),
                pl.BlockSpec((1, MLA_HEADS * HEAD_PAD, tm), chunk),
                pl.BlockSpec((tm, mw), row), pl.BlockSpec((tm, mw), row),
                pl.BlockSpec((1, mw, tm), chunk),
                pl.BlockSpec((1, mw, tm), chunk),
                pl.BlockSpec((n_gate, tm), col),
            ],
            out_shape=[
                jax.ShapeDtypeStruct((T, MLA_HEADS * HEAD_PAD), BF16),
                jax.ShapeDtypeStruct((T, MLA_HEADS * HEAD_PAD), BF16),
                jax.ShapeDtypeStruct((T // tm, MLA_HEADS * HEAD_PAD, tm), BF16),
                jax.ShapeDtypeStruct((T, mw), BF16), jax.ShapeDtypeStruct((T, mw), BF16),
                jax.ShapeDtypeStruct((T // tm, mw, tm), BF16),
                jax.ShapeDtypeStruct((T // tm, mw, tm), BF16),
                jax.ShapeDtypeStruct((n_gate, T), F32),
            ],
            scratch_shapes=[pltpu.VMEM((tm + 2 * HALO, 2 * mw), F32)],
            compiler_params=_params(1),
            name="inproj",
        )(xf, xf, xf, cos_t, sin_t, norm_mix_g[l].reshape(1, D), w1, w2t, w3, wgt, gate_b, cw, cb,
          mla_q_norm_g[l].reshape(1, q_lora), wqa, wqb, mla_kv_norm_g[l].reshape(1, kv_lora), wuk, wuvt)
        q_a, k_a, v_a, q_m, k_m, v_m, mo_s, gates = outs

        tq, tk = 256, 1024
        vchunk = tm
        y_attn = pl.pallas_call(
            functools.partial(_attn_kernel, tq=tq, tk=tk, vchunk=vchunk, n_q=S // tq, n_k=S // tk),
            grid=(B, MLA_HEADS // 2),
            in_specs=[
                pl.BlockSpec((S, 2 * HEAD_PAD), lambda b, p: (b, p)),
                pl.BlockSpec((S, 2 * HEAD_PAD), lambda b, p: (b, p)),
                pl.BlockSpec((S // vchunk, 2 * HEAD_PAD, vchunk), lambda b, p: (b, p, 0)),
            ],
            out_specs=pl.BlockSpec((S, 2 * MLA_V), lambda b, p: (b, p)),
            out_shape=jax.ShapeDtypeStruct((T, MLA_HEADS * MLA_V), BF16),
            scratch_shapes=[pltpu.VMEM((2, tk, tq), F32), pltpu.VMEM((2, tk, tq), F32),
                            pltpu.VMEM((2, tk, tq), BF16), pltpu.VMEM((2, tk, tq), BF16),
                            pltpu.VMEM((2, tk, tq), BF16), pltpu.VMEM((2, tk, tq), BF16),
                            pltpu.VMEM((2, HEAD_PAD, tq), F32)],
            compiler_params=_params(2),
            name="mla_attn",
        )(q_a, k_a, v_a)

        gates5 = gates.reshape(2, 2, MLSTM_HEADS, T // L, L)
        assert tm == L
        head_blk = lambda b, h: (b, h)
        head_blk_t = lambda b, h: (b, h, 0)
        aug = MLSTM_DH + BF16_SUBLANES
        row_scratch = pltpu.VMEM((2, nc, L), F32)
        y_mlstm_t = pl.pallas_call(
            functools.partial(_mlstm_kernel, L=L, nc=nc),
            grid=(B, MLSTM_HEADS),
            in_specs=[
                pl.BlockSpec((S, MLSTM_DH), head_blk), pl.BlockSpec((S, MLSTM_DH), head_blk),
                pl.BlockSpec((nc, MLSTM_DH, L), head_blk_t), pl.BlockSpec((nc, MLSTM_DH, L), head_blk_t),
                pl.BlockSpec((2, 2, 1, nc, L), lambda b, h: (0, 0, h, b, 0)),
                pl.BlockSpec((MLSTM_DH, 1), lambda b, h: (h, 0)),
            ],
            out_specs=pl.BlockSpec((nc, MLSTM_DH, L), head_blk_t),
            out_shape=jax.ShapeDtypeStruct((T // L, mw, L), BF16),
            scratch_shapes=[pltpu.VMEM((nc, MLSTM_DH, L), F32),
                            pltpu.VMEM((2, aug, MLSTM_DH), F32),
                            row_scratch, row_scratch, row_scratch,
                            pltpu.VMEM((2, nc, 1), F32), pltpu.VMEM((2, nc, 1), F32)],
            compiler_params=_params(2),
            name="mlstm",
        )(q_m, k_m, v_m, mo_s, gates5, mlstm_out_norm_g[l].reshape(mw, 1))

        tm4 = L
        last = l == depth - 1
        gf = norm_final_g.reshape(1, D)
        weights4 = (wgab, w_branch_mla[l].astype(BF16), w_branch_mlstm[l].astype(BF16),
                    w_out[l].astype(BF16), w_mlp_up[l].astype(BF16), w_mlp_down[l].astype(BF16))
        xf = pl.pallas_call(
            functools.partial(_merge_mlp_kernel, d=D, final_norm=last),
            grid=(T // tm4,),
            in_specs=[
                pl.BlockSpec((tm4, D), row),
                pl.BlockSpec((tm4, MLA_HEADS * MLA_V), row),
                pl.BlockSpec((1, mw, tm4), lambda i: (i, 0, 0)),
                _const_spec((1, D)),
                _resident_spec(weights4[0].shape), _resident_spec(weights4[1].shape),
                _resident_spec(weights4[2].shape), _resident_spec(weights4[3].shape),
                _const_spec((1, D)),
                _resident_spec(weights4[4].shape), _resident_spec(weights4[5].shape),
                _const_spec((1, D)),
            ],
            out_specs=pl.BlockSpec((tm4, D), row),
            out_shape=jax.ShapeDtypeStruct((T, D), F32),
            compiler_params=_params(1),
            name="merge_mlp",
        )(xf, y_attn, y_mlstm_t, norm_mix_g[l].reshape(1, D), weights4[0], weights4[1], weights4[2],
          weights4[3], norm_mlp_g[l].reshape(1, D), weights4[4], weights4[5], gf)

    return xf.reshape(B, S, D)
```

```python
import functools
import math

import jax
import jax.numpy as jnp
from jax import lax
from jax.experimental import pallas as pl
from jax.experimental.pallas import tpu as pltpu

MLA_HEADS = 8
MLA_NOPE = 64
MLA_ROPE = 32
MLA_V = 64
ROPE_THETA = 10000.0
MLSTM_HEADS = 4
MLSTM_DH = 128
CONV_WIDTH = 5
NORM_EPS = 1e-6

LANES = 128
BF16_SUBLANES = 16
VMEM_LIMIT_BYTES = 56 * 1024 * 1024

HEAD_PAD = LANES
MLSTM_CHUNK = 256
MLSTM_POSITIONS_PER_BODY = 2
HALO = BF16_SUBLANES

F32 = jnp.float32
BF16 = jnp.bfloat16


def _rms(x, g):
    return x * lax.rsqrt(jnp.mean(x * x, axis=-1, keepdims=True) + NORM_EPS) * g


def _dot(a, b):
    return jnp.dot(a, b, preferred_element_type=F32)


def _dot_nt(a, b):
    return lax.dot_general(a, b, (((1,), (1,)), ((), ())), preferred_element_type=F32)


def _dot_tn(a, b):
    return lax.dot_general(a, b, (((0,), (0,)), ((), ())), preferred_element_type=F32)


def _log_sigmoid(x):
    return jnp.minimum(x, 0.0) - jnp.log1p(jnp.exp(-jnp.abs(x)))


def _inproj_kernel(xm_ref, xp_ref, xn_ref, cos_ref, sin_ref, g_ref, w1_ref, w2t_ref, w3_ref,
                   wgt_ref, gb_ref, cw_ref, cb_ref, qg_ref, wqa_ref, wqb_ref, kvg_ref,
                   wuk_ref, wuvt_ref,
                   q_out, k_out, va_out, qm_out, km_out, vm_out, mo_out, gate_out,
                   pre_scr, *, tm, tiles_per_seq, q_lora, kv_lora, mw, q_scale, k_scale):
    i = pl.program_id(0)
    pos_in_seq = i % tiles_per_seq
    xp = jnp.where(pos_in_seq == 0, 0.0, xp_ref[...])
    xn = jnp.where(pos_in_seq == tiles_per_seq - 1, 0.0, xn_ref[...])
    xe = jnp.concatenate([xp, xm_ref[...], xn], axis=0)
    he = _rms(xe, g_ref[...]).astype(BF16)
    hm = he[HALO:HALO + tm]

    pre_scr[...] = _dot(he, w1_ref[...])
    conv = cb_ref[...]
    for j in range(CONV_WIDTH):
        conv = conv + cw_ref[j:j + 1, :] * pre_scr[pl.ds(HALO - CONV_WIDTH // 2 + j, tm), :]
    qk = conv * jax.nn.sigmoid(conv)
    qm_out[...] = qk[:, :mw].astype(BF16)
    km_out[...] = (qk[:, mw:] * k_scale).astype(BF16)

    vo_t = _dot_nt(w2t_ref[...], hm)
    vm_out[0] = vo_t[:mw].astype(BF16)
    mo_out[0] = jax.nn.sigmoid(vo_t[mw:]).astype(BF16)

    gt = _dot_nt(wgt_ref[...], hm) + gb_ref[...]
    row = lax.broadcasted_iota(jnp.int32, gt.shape, 0)
    is_f = (row % (2 * MLSTM_HEADS)) >= MLSTM_HEADS
    gate_out[...] = jnp.where(is_f, _log_sigmoid(gt), gt)

    c = _dot(hm, w3_ref[...])
    cqn = _rms(c[:, :q_lora], qg_ref[...]).astype(BF16)
    ckvn = _rms(c[:, q_lora:q_lora + kv_lora], kvg_ref[...]).astype(BF16)
    kra = c[:, q_lora + kv_lora:q_lora + kv_lora + HEAD_PAD]
    krb = c[:, q_lora + kv_lora + HEAD_PAD:]
    cos = cos_ref[...].T
    sin = sin_ref[...].T
    cos_h = jnp.concatenate([cos] * MLA_HEADS, axis=1)
    sin_h = jnp.concatenate([sin] * MLA_HEADS, axis=1)
    q = _dot(cqn, wqa_ref[...]) * cos_h + _dot(cqn, wqb_ref[...]) * sin_h
    q_out[...] = (q * q_scale).astype(BF16)
    kr = kra * cos + krb * sin
    k = _dot(ckvn, wuk_ref[...]) + jnp.concatenate([kr] * MLA_HEADS, axis=1)
    k_out[...] = k.astype(BF16)
    vt = _dot_nt(wuvt_ref[...], ckvn)
    ones = jnp.ones((HEAD_PAD - MLA_V, tm), F32)
    pieces = []
    for h in range(MLA_HEADS):
        pieces += [vt[h * MLA_V:(h + 1) * MLA_V], ones]
    va_out[0] = jnp.concatenate(pieces, axis=0).astype(BF16)


def _aligned(x, m):
    return x if isinstance(x, int) else pl.multiple_of(x, m)


def _attn_kernel(q_ref, k_ref, vt_ref, o_ref, s_0, s_1, p_0, p_1, p_2, p_3, acc_scr, *,
                 tq, tk, vchunk, n_q, n_k):
    s_buf, p_buf = (s_0, s_1), (p_0, p_1, p_2, p_3)
    n_slots = len(p_buf)
    sub = tk // vchunk
    n_total = n_q * n_k
    assert n_k % n_slots == 0 and n_slots % 2 == 0 and n_total >= 6

    def scores(j, par):
        qoff = _aligned((j // n_k) * tq, tq)
        koff = _aligned((j % n_k) * tk, tk)
        cmax = []
        for hh in range(2):
            cols = slice(hh * HEAD_PAD, (hh + 1) * HEAD_PAD)
            s = _dot_nt(k_ref[pl.ds(koff, tk), cols], q_ref[pl.ds(qoff, tq), cols])
            s_buf[par][hh] = s
            cmax.append(jnp.max(s, axis=0, keepdims=True))
        return tuple(cmax)

    def pv_issue(j, slot):
        kb = j % n_k
        out = []
        for hh in range(2):
            vt = jnp.concatenate([vt_ref[kb * sub + c, hh * HEAD_PAD:(hh + 1) * HEAD_PAD, :]
                                  for c in range(sub)], axis=1)
            out.append(_dot(vt, p_buf[slot][hh]))
        return out

    def pv_accumulate(r, alpha):
        for hh in range(2):
            acc_scr[hh] = alpha[hh] * acc_scr[hh] + r[hh]

    def softmax(j, par, slot, m, cmax):
        first = (j % n_k) == 0
        m_out, alpha = [], []
        for hh in range(2):
            m_old = jnp.where(first, -jnp.inf, m[hh])
            m_new = jnp.maximum(m_old, cmax[hh])
            p_buf[slot][hh] = jnp.exp2(s_buf[par][hh] - m_new).astype(BF16)
            alpha.append(jnp.exp2(m_old - m_new))
            m_out.append(m_new)
        return tuple(m_out), tuple(alpha)

    def finalize(j):
        qoff = _aligned((j // n_k) * tq, tq)
        o_t = jnp.concatenate([acc_scr[hh, :MLA_V] / acc_scr[hh, MLA_V:MLA_V + 1] for hh in range(2)],
                              axis=0)
        o_ref[pl.ds(qoff, tq), :] = o_t.T.astype(BF16)

    def pv(j, alpha, r=None):
        pv_accumulate(pv_issue(j, j % n_slots) if r is None else r, alpha)
        if isinstance(j, int) and j % n_k == n_k - 1:
            finalize(j)

    def body(k, kmod, m, alpha_prev, alpha_cur, cmax):
        par = kmod % 2
        r = pv_issue(k - 1, (kmod - 1) % n_slots)
        cmax_next = scores(k + 2, par)
        m, alpha_next = softmax(k + 1, 1 - par, (kmod + 1) % n_slots, m, cmax)
        pv(k - 1, alpha_prev, r)
        return m, alpha_cur, alpha_next, cmax_next

    def loop_body(kk, carry):
        k0 = n_slots * kk + 1
        for o in range(n_slots):
            carry = body(k0 + o, (1 + o) % n_slots, *carry)
        last = k0 + n_slots - 2
        pl.when(last % n_k == n_k - 1)(functools.partial(finalize, last))
        return carry

    acc_scr[...] = jnp.zeros_like(acc_scr)
    m = tuple(jnp.full((1, tq), -jnp.inf, F32) for _ in range(2))
    cmax0 = scores(0, 0)
    cmax1 = scores(1, 1)
    m, alpha0 = softmax(0, 0, 0, m, cmax0)
    cmax2 = scores(2, 0)
    m, alpha1 = softmax(1, 1, 1, m, cmax1)
    n_iter = (n_total - 3) // n_slots
    carry = lax.fori_loop(0, n_iter, loop_body, (m, alpha0, alpha1, cmax2))
    k = n_slots * n_iter + 1
    while k <= n_total - 3:
        carry = body(k, k % n_slots, *carry)
        k += 1
    m, alpha_prev, alpha_cur, cmax = carry
    r = pv_issue(k - 1, (k - 1) % n_slots)
    m, alpha_last = softmax(k + 1, (k + 1) % 2, (k + 1) % n_slots, m, cmax)
    pv(k - 1, alpha_prev, r)
    pv(n_total - 2, alpha_cur)
    pv(n_total - 1, alpha_last)


def _scan_lanes(x, op, fill, reverse):
    n = x.shape[1]
    lane = lax.broadcasted_iota(jnp.int32, x.shape, 1)
    shift = 1
    while shift < n:
        if reverse:
            moved, valid = pltpu.roll(x, n - shift, 1), lane < n - shift
        else:
            moved, valid = pltpu.roll(x, shift, 1), lane >= shift
        x = op(x, jnp.where(valid, moved, fill))
        shift *= 2
    return x


class _Chain:
    pass


def _mlstm_kernel(q_ref, k_ref, vt_ref, mot_ref, gate_ref, og_ref, o_ref,
                  hacc, st_scr, b_scr, a_scr, cm_scr, bl_scr, am_scr, *, L, nc):
    dh = MLSTM_DH
    aug = st_scr.shape[1]
    s_idx = lax.broadcasted_iota(jnp.int32, (L, L), 0)
    t_idx = lax.broadcasted_iota(jnp.int32, (L, L), 1)
    eye = s_idx == t_idx
    visible = (s_idx <= t_idx, s_idx >= t_idx)
    ones_blk = (lax.broadcasted_iota(jnp.int32, (aug - dh, L), 0) == 0).astype(BF16)

    for d in range(2):
        logi = gate_ref[d, 0, 0]
        logf = gate_ref[d, 1, 0]
        b = _scan_lanes(logf, jnp.add, 0.0, reverse=d == 1)
        a = logi - b
        b_scr[d] = b
        a_scr[d] = a
        cm_scr[d] = _scan_lanes(a, jnp.maximum, -jnp.inf, reverse=d == 1)
        bl_scr[d] = jnp.sum(logf, axis=1, keepdims=True)
        am_scr[d] = jnp.max(a, axis=1, keepdims=True)
    st_scr[...] = jnp.zeros_like(st_scr)

    def prepare(d, c, m):
        x = _Chain()
        off = pl.multiple_of(c * L, L)
        x.q = q_ref[pl.ds(off, L), :]
        x.k = k_ref[pl.ds(off, L), :]
        x.vaug = jnp.concatenate([vt_ref[c], ones_blk], axis=0)
        x.b_row = b_scr[d, pl.ds(c, 1), :]
        x.a_row = a_scr[d, pl.ds(c, 1), :]
        b_last = bl_scr[d, pl.ds(c, 1), :]
        x.g_row = jnp.maximum(cm_scr[d, pl.ds(c, 1), :], m)
        x.iw = jnp.exp(m - x.g_row)
        x.m_new = jnp.maximum(b_last + m, b_last + am_scr[d, pl.ds(c, 1), :])
        x.decay = jnp.exp(b_last + m - x.m_new)
        x.w_row = jnp.exp(b_last + x.a_row - x.m_new)
        return x

    def intra(d, x):
        a_col = jnp.sum(jnp.where(eye, x.a_row, 0.0), axis=1, keepdims=True)
        e_t = jnp.exp(jnp.where(visible[d], a_col - x.g_row, -jnp.inf))
        x.r = _dot(x.vaug, (x.st * e_t).astype(BF16))

    def finish(x):
        num = x.iw * x.inter[:dh] + x.r[:dh]
        den = x.iw * x.inter[dh:dh + 1] + x.r[dh:dh + 1]
        floor = jnp.exp(-(x.b_row + x.g_row))
        return num * (1.0 / jnp.maximum(jnp.abs(den), floor))

    def run(j0, m_f, m_b):
        chains = []
        for o in range(MLSTM_POSITIONS_PER_BODY):
            j = j0 + o
            xf = prepare(0, j, m_f)
            xb = prepare(1, nc - 1 - j, m_b)
            m_f, m_b = xf.m_new, xb.m_new
            chains += [(0, j, xf), (1, nc - 1 - j, xb)]
        for d, _, x in chains:
            x.st = _dot_nt(x.k, x.q)
            x.upd = _dot((x.vaug.astype(F32) * x.w_row).astype(BF16), x.k)
        for d, _, x in chains:
            x.inter = _dot_nt(st_scr[d].astype(BF16), x.q)
            st_scr[d] = x.decay * st_scr[d] + x.upd
        for d, _, x in chains:
            intra(d, x)
        return [(c, finish(x)) for _, c, x in chains], m_f, m_b

    def first_touch(jj, carry):
        outs, m_f, m_b = run(jj * MLSTM_POSITIONS_PER_BODY, *carry)
        for c, h in outs:
            hacc[c] = h
        return m_f, m_b

    def second_touch(jj, carry):
        outs, m_f, m_b = run(jj * MLSTM_POSITIONS_PER_BODY, *carry)
        for c, h in outs:
            tot = hacc[c] + h
            y = tot * lax.rsqrt(jnp.mean(tot * tot, axis=0, keepdims=True) + NORM_EPS) * og_ref[...]
            o_ref[c] = (y * mot_ref[c].astype(F32)).astype(BF16)
        return m_f, m_b

    n_iter = nc // MLSTM_POSITIONS_PER_BODY
    m0 = jnp.zeros((1, 1), F32)
    carry = lax.fori_loop(0, n_iter // 2, first_touch, (m0, m0))
    lax.fori_loop(n_iter // 2, n_iter, second_touch, carry)


def _merge_mlp_kernel(x_ref, ya_ref, ymt_ref, g1_ref, wgab_ref, wbm_ref, wbl_ref, wout_ref,
                      g2_ref, wup_ref, wdn_ref, gf_ref, o_ref, *, d, final_norm):
    x = x_ref[...]
    hn = _rms(x, g1_ref[...]).astype(BF16)
    gates = jax.nn.sigmoid(_dot(hn, wgab_ref[...]))
    merged = (gates[:, :d] * _dot(ya_ref[...], wbm_ref[...])
              + gates[:, d:] * _dot_tn(ymt_ref[0], wbl_ref[...]))
    x1 = x + _dot(merged.astype(BF16), wout_ref[...])
    u = _dot(_rms(x1, g2_ref[...]).astype(BF16), wup_ref[...])
    r = jnp.maximum(u, 0.0)
    x2 = x1 + _dot((r * r).astype(BF16), wdn_ref[...])
    o_ref[...] = _rms(x2, gf_ref[...]) if final_norm else x2


def _const_spec(shape):
    return pl.BlockSpec(shape, lambda *_: (0,) * len(shape))


def _resident_spec(shape):
    return pl.BlockSpec(shape, lambda *_: (0,) * len(shape), pipeline_mode=pl.Buffered(1))


def _params(n_axes):
    return pltpu.CompilerParams(dimension_semantics=("arbitrary",) * n_axes,
                                vmem_limit_bytes=VMEM_LIMIT_BYTES)


def _rope_kernel(pos_ref, freq_ref, cos_out, sin_out):
    t = pos_ref.shape[1]
    ang = freq_ref[...] * pos_ref[...]
    cos, sin = jnp.cos(ang), jnp.sin(ang)
    pad = jnp.zeros((HEAD_PAD - MLA_NOPE - MLA_ROPE, t), F32)
    cos_out[...] = jnp.concatenate([jnp.ones((MLA_NOPE, t), F32), cos, cos, pad], axis=0)
    sin_out[...] = jnp.concatenate([jnp.zeros((MLA_NOPE, t), F32), -sin, sin, pad], axis=0)


def _rope_tables(positions):
    t = positions.size
    inv_freq = ROPE_THETA ** (-jnp.arange(0, MLA_ROPE, 2, dtype=F32) / MLA_ROPE)
    table = jax.ShapeDtypeStruct((HEAD_PAD, t), F32)
    tile = min(t, 16 * LANES)
    return pl.pallas_call(
        _rope_kernel,
        grid=(t // tile,),
        in_specs=[pl.BlockSpec((1, tile), lambda i: (0, i)), _const_spec((MLA_ROPE // 2, 1))],
        out_specs=[pl.BlockSpec((HEAD_PAD, tile), lambda i: (0, i))] * 2,
        out_shape=[table, table], compiler_params=_params(1), name="rope_table",
    )(positions.astype(F32).reshape(1, t), inv_freq.reshape(-1, 1))


def _pad_heads(w, width_in, offset_out):
    k = w.shape[0]
    w = w.reshape(k, MLA_HEADS, width_in)
    w = jnp.pad(w, ((0, 0), (0, 0), (offset_out, HEAD_PAD - width_in - offset_out)))
    return w.reshape(k, MLA_HEADS * HEAD_PAD)


def kernel(x, positions, norm_mix_g, w_in, mla_q_norm_g, mla_w_uq, mla_kv_norm_g, mla_w_ukv, mlstm_conv_w, mlstm_conv_b, mlstm_igate_b, mlstm_fgate_b, mlstm_out_norm_g, w_branch_mla, w_branch_mlstm, w_out, norm_mlp_g, w_mlp_up, w_mlp_down, norm_final_g):
    B, S, D = x.shape
    T = B * S
    depth = w_in.shape[0]
    q_lora = mla_q_norm_g.shape[1]
    kv_lora = mla_kv_norm_g.shape[1]
    mw = MLSTM_HEADS * MLSTM_DH
    half = MLA_ROPE // 2
    n_gate = 4 * MLSTM_HEADS
    d_ff = w_mlp_up.shape[2]
    L = MLSTM_CHUNK
    nc = S // L
    assert S % (2 * MLSTM_POSITIONS_PER_BODY * L) == 0 and MLA_HEADS % 2 == 0

    cos_t, sin_t = _rope_tables(positions)
    xf = x.reshape(T, D)

    for l in range(depth):
        offs, o = [], 0
        for w in (q_lora, kv_lora, MLA_ROPE, mw, mw, mw, mw, n_gate, D, D):
            offs.append(o)
            o += w
        wl = w_in[l]
        w_cq = wl[:, offs[0]:offs[0] + q_lora]
        w_ckv = wl[:, offs[1]:offs[1] + kv_lora]
        w_kr = wl[:, offs[2]:offs[2] + MLA_ROPE]
        w_kr_swapped = jnp.concatenate([w_kr[:, half:], w_kr[:, :half]], axis=1)
        pad_kr = ((0, 0), (MLA_NOPE, HEAD_PAD - MLA_NOPE - MLA_ROPE))
        w1 = wl[:, offs[3]:offs[3] + 2 * mw].astype(BF16)
        w2t = wl[:, offs[5]:offs[5] + 2 * mw].T.astype(BF16)
        w3 = jnp.concatenate([w_cq, w_ckv, jnp.pad(w_kr, pad_kr), jnp.pad(w_kr_swapped, pad_kr)],
                             axis=1).astype(BF16)
        wgt = wl[:, offs[7]:offs[7] + n_gate].T.astype(BF16)
        wgab = wl[:, offs[8]:offs[8] + 2 * D].astype(BF16)
        gate_b = jnp.stack([mlstm_igate_b[l], mlstm_fgate_b[l]], axis=1).reshape(n_gate, 1)

        dqk = MLA_NOPE + MLA_ROPE
        wq = mla_w_uq[l].reshape(q_lora, MLA_HEADS, dqk)
        wq_rope = wq[:, :, MLA_NOPE:]
        wq_swapped = jnp.concatenate([jnp.zeros_like(wq[:, :, :MLA_NOPE]),
                                      wq_rope[:, :, half:], wq_rope[:, :, :half]], axis=2)
        wqa = _pad_heads(wq.reshape(q_lora, -1), dqk, 0).astype(BF16)
        wqb = _pad_heads(wq_swapped.reshape(q_lora, -1), dqk, 0).astype(BF16)
        wkv = mla_w_ukv[l].reshape(kv_lora, MLA_HEADS, MLA_NOPE + MLA_V)
        wuk = _pad_heads(wkv[:, :, :MLA_NOPE].reshape(kv_lora, -1), MLA_NOPE, 0).astype(BF16)
        wuvt = wkv[:, :, MLA_NOPE:].reshape(kv_lora, MLA_HEADS * MLA_V).T.astype(BF16)

        cw = jnp.pad(mlstm_conv_w[l].reshape(CONV_WIDTH, 2 * mw), ((0, 8 - CONV_WIDTH), (0, 0)))
        cb = mlstm_conv_b[l].reshape(1, 2 * mw)

        tm = 256
        tiles_per_seq = S // tm
        hb = tm // HALO
        n_halo_blocks = T // HALO
        row = lambda i: (i, 0)
        col = lambda i: (0, i)
        chunk = lambda i: (i, 0, 0)
        kern = functools.partial(
            _inproj_kernel, tm=tm, tiles_per_seq=tiles_per_seq, q_lora=q_lora, kv_lora=kv_lora, mw=mw,
            q_scale=(MLA_NOPE + MLA_ROPE) ** -0.5 * math.log2(math.e), k_scale=MLSTM_DH ** -0.5)
        outs = pl.pallas_call(
            kern,
            grid=(T // tm,),
            in_specs=[
                pl.BlockSpec((tm, D), row),
                pl.BlockSpec((HALO, D), lambda i: (jnp.maximum(i * hb - 1, 0), 0)),
                pl.BlockSpec((HALO, D), lambda i: (jnp.minimum((i + 1) * hb, n_halo_blocks - 1), 0)),
                pl.BlockSpec((HEAD_PAD, tm), col),
                pl.BlockSpec((HEAD_PAD, tm), col),
                _const_spec((1, D)),
                _const_spec(w1.shape), _const_spec(w2t.shape), _const_spec(w3.shape),
                _const_spec(wgt.shape), _const_spec(gate_b.shape),
                _const_spec(cw.shape), _const_spec(cb.shape),
                _const_spec((1, q_lora)), _const_spec(wqa.shape), _const_spec(wqb.shape),
                _const_spec((1, kv_lora)), _const_spec(wuk.shape), _const_spec(wuvt.shape),
            ],
            out_specs=[
                pl.BlockSpec((tm, MLA_HEADS * HEAD_PAD), row),
                pl.BlockSpec((tm, MLA_HEADS * HEAD_PAD), row),
                pl.BlockSpec((1, MLA_HEADS * HEAD_PAD, tm), chunk),
                pl.BlockSpec((tm, mw), row), pl.BlockSpec((tm, mw), row),
                pl.BlockSpec((1, mw, tm), chunk),
                pl.BlockSpec((1, mw, tm), chunk),
                pl.BlockSpec((n_gate, tm), col),
            ],
            out_shape=[
                jax.ShapeDtypeStruct((T, MLA_HEADS * HEAD_PAD), BF16),
                jax.ShapeDtypeStruct((T, MLA_HEADS * HEAD_PAD), BF16),
                jax.ShapeDtypeStruct((T // tm, MLA_HEADS * HEAD_PAD, tm), BF16),
                jax.ShapeDtypeStruct((T, mw), BF16), jax.ShapeDtypeStruct((T, mw), BF16),
                jax.ShapeDtypeStruct((T // tm, mw, tm), BF16),
                jax.ShapeDtypeStruct((T // tm, mw, tm), BF16),
                jax.ShapeDtypeStruct((n_gate, T), F32),
            ],
            scratch_shapes=[pltpu.VMEM((tm + 2 * HALO, 2 * mw), F32)],
            compiler_params=_params(1),
            name="inproj",
        )(xf, xf, xf, cos_t, sin_t, norm_mix_g[l].reshape(1, D), w1, w2t, w3, wgt, gate_b, cw, cb,
          mla_q_norm_g[l].reshape(1, q_lora), wqa, wqb, mla_kv_norm_g[l].reshape(1, kv_lora), wuk, wuvt)
        q_a, k_a, v_a, q_m, k_m, v_m, mo_s, gates = outs

        tq, tk = 256, 1024
        vchunk = tm
        y_attn = pl.pallas_call(
            functools.partial(_attn_kernel, tq=tq, tk=tk, vchunk=vchunk, n_q=S // tq, n_k=S // tk),
            grid=(B, MLA_HEADS // 2),
            in_specs=[
                pl.BlockSpec((S, 2 * HEAD_PAD), lambda b, p: (b, p)),
                pl.BlockSpec((S, 2 * HEAD_PAD), lambda b, p: (b, p)),
                pl.BlockSpec((S // vchunk, 2 * HEAD_PAD, vchunk), lambda b, p: (b, p, 0)),
            ],
            out_specs=pl.BlockSpec((S, 2 * MLA_V), lambda b, p: (b, p)),
            out_shape=jax.ShapeDtypeStruct((T, MLA_HEADS * MLA_V), BF16),
            scratch_shapes=[pltpu.VMEM((2, tk, tq), F32), pltpu.VMEM((2, tk, tq), F32),
                            pltpu.VMEM((2, tk, tq), BF16), pltpu.VMEM((2, tk, tq), BF16),
                            pltpu.VMEM((2, tk, tq), BF16), pltpu.VMEM((2, tk, tq), BF16),
                            pltpu.VMEM((2, HEAD_PAD, tq), F32)],
            compiler_params=_params(2),
            name="mla_attn",
        )(q_a, k_a, v_a)

        gates5 = gates.reshape(2, 2, MLSTM_HEADS, T // L, L)
        assert tm == L
        head_blk = lambda b, h: (b, h)
        head_blk_t = lambda b, h: (b, h, 0)
        aug = MLSTM_DH + BF16_SUBLANES
        row_scratch = pltpu.VMEM((2, nc, L), F32)
        y_mlstm_t = pl.pallas_call(
            functools.partial(_mlstm_kernel, L=L, nc=nc),
            grid=(B, MLSTM_HEADS),
            in_specs=[
                pl.BlockSpec((S, MLSTM_DH), head_blk), pl.BlockSpec((S, MLSTM_DH), head_blk),
                pl.BlockSpec((nc, MLSTM_DH, L), head_blk_t), pl.BlockSpec((nc, MLSTM_DH, L), head_blk_t),
                pl.BlockSpec((2, 2, 1, nc, L), lambda b, h: (0, 0, h, b, 0)),
                pl.BlockSpec((MLSTM_DH, 1), lambda b, h: (h, 0)),
            ],
            out_specs=pl.BlockSpec((nc, MLSTM_DH, L), head_blk_t),
            out_shape=jax.ShapeDtypeStruct((T // L, mw, L), BF16),
            scratch_shapes=[pltpu.VMEM((nc, MLSTM_DH, L), F32),
                            pltpu.VMEM((2, aug, MLSTM_DH), F32),
                            row_scratch, row_scratch, row_scratch,
                            pltpu.VMEM((2, nc, 1), F32), pltpu.VMEM((2, nc, 1), F32)],
            compiler_params=_params(2),
            name="mlstm",
        )(q_m, k_m, v_m, mo_s, gates5, mlstm_out_norm_g[l].reshape(mw, 1))

        tm4 = L
        last = l == depth - 1
        gf = norm_final_g.reshape(1, D)
        weights4 = (wgab, w_branch_mla[l].astype(BF16), w_branch_mlstm[l].astype(BF16),
                    w_out[l].astype(BF16), w_mlp_up[l].astype(BF16), w_mlp_down[l].astype(BF16))
        xf = pl.pallas_call(
            functools.partial(_merge_mlp_kernel, d=D, final_norm=last),
            grid=(T // tm4,),
            in_specs=[
                pl.BlockSpec((tm4, D), row),
                pl.BlockSpec((tm4, MLA_HEADS * MLA_V), row),
                pl.BlockSpec((1, mw, tm4), lambda i: (i, 0, 0)),
                _const_spec((1, D)),
                _resident_spec(weights4[0].shape), _resident_spec(weights4[1].shape),
                _resident_spec(weights4[2].shape), _resident_spec(weights4[3].shape),
                _const_spec((1, D)),
                _resident_spec(weights4[4].shape), _resident_spec(weights4[5].shape),
                _const_spec((1, D)),
            ],
            out_specs=pl.BlockSpec((tm4, D), row),
            out_shape=jax.ShapeDtypeStruct((T, D), F32),
            compiler_params=_params(1),
            name="merge_mlp",
        )(xf, y_attn, y_mlstm_t, norm_mix_g[l].reshape(1, D), weights4[0], weights4[1], weights4[2],
          weights4[3], norm_mlp_g[l].reshape(1, D), weights4[4], weights4[5], gf)

    return xf.reshape(B, S, D)
```

```python
import functools
import math

import jax
import jax.numpy as jnp
from jax import lax
from jax.experimental import pallas as pl
from jax.experimental.pallas import tpu as pltpu

MLA_HEADS = 8
MLA_NOPE = 64
MLA_ROPE = 32
MLA_V = 64
ROPE_THETA = 10000.0
MLSTM_HEADS = 4
MLSTM_DH = 128
CONV_WIDTH = 5
NORM_EPS = 1e-6

LANES = 128
BF16_SUBLANES = 16
VMEM_LIMIT_BYTES = 56 * 1024 * 1024

HEAD_PAD = LANES
MLSTM_CHUNK = 256
MLSTM_POSITIONS_PER_BODY = 2
HALO = BF16_SUBLANES

F32 = jnp.float32
BF16 = jnp.bfloat16


def _rms(x, g):
    return x * lax.rsqrt(jnp.mean(x * x, axis=-1, keepdims=True) + NORM_EPS) * g


def _dot(a, b):
    return jnp.dot(a, b, preferred_element_type=F32)


def _dot_nt(a, b):
    return lax.dot_general(a, b, (((1,), (1,)), ((), ())), preferred_element_type=F32)


def _dot_tn(a, b):
    return lax.dot_general(a, b, (((0,), (0,)), ((), ())), preferred_element_type=F32)


def _log_sigmoid(x):
    return jnp.minimum(x, 0.0) - jnp.log1p(jnp.exp(-jnp.abs(x)))


def _inproj_kernel(xm_ref, xp_ref, xn_ref, cos_ref, sin_ref, g_ref, w1_ref, w2t_ref, w3_ref,
                   gb_ref, cw_ref, cb_ref, qg_ref, wqa_ref, kvg_ref,
                   wuk_ref, wuvt_ref,
                   q_out, k_out, va_out, qm_out, km_out, vm_out, mo_out, gate_out,
                   pre_scr, *, tm, tiles_per_seq, q_lora, kv_lora, mw, q_scale, k_scale):
    i = pl.program_id(0)
    pos_in_seq = i % tiles_per_seq
    xp = jnp.where(pos_in_seq == 0, 0.0, xp_ref[...])
    xn = jnp.where(pos_in_seq == tiles_per_seq - 1, 0.0, xn_ref[...])
    xe = jnp.concatenate([xp, xm_ref[...], xn], axis=0)
    he = _rms(xe, g_ref[...]).astype(BF16)
    hm = he[HALO:HALO + tm]

    pre_scr[...] = _dot(he, w1_ref[...])
    conv = cb_ref[...]
    for j in range(CONV_WIDTH):
        conv = conv + cw_ref[j:j + 1, :] * pre_scr[pl.ds(HALO - CONV_WIDTH // 2 + j, tm), :]
    qk = conv * jax.nn.sigmoid(conv)
    qm_out[...] = qk[:, :mw].astype(BF16)
    km_out[...] = (qk[:, mw:] * k_scale).astype(BF16)

    vo_t = _dot_nt(w2t_ref[...], hm)
    vm_out[0] = vo_t[:mw].astype(BF16)
    mo_out[0] = jax.nn.sigmoid(vo_t[mw:2 * mw]).astype(BF16)

    gt = vo_t[2 * mw:] + gb_ref[...]
    row = lax.broadcasted_iota(jnp.int32, gt.shape, 0)
    is_f = (row % (2 * MLSTM_HEADS)) >= MLSTM_HEADS
    gate_out[...] = jnp.where(is_f, _log_sigmoid(gt), gt)

    c = _dot(hm, w3_ref[...])
    cqn = _rms(c[:, :q_lora], qg_ref[...]).astype(BF16)
    ckvn = _rms(c[:, q_lora:q_lora + kv_lora], kvg_ref[...]).astype(BF16)
    cos = cos_ref[...].T
    sin = sin_ref[...].T
    half = MLA_ROPE // 2
    lane = lax.broadcasted_iota(jnp.int32, sin.shape, 1)
    sin_lo = jnp.where(lane < MLA_NOPE + half, sin, 0.0)
    sin_hi = sin - sin_lo

    def rope(x):
        return (x * cos + pltpu.roll(x, HEAD_PAD - half, 1) * sin_lo + pltpu.roll(x, half, 1) * sin_hi)

    qa = _dot(cqn, wqa_ref[...])
    q = jnp.concatenate([rope(qa[:, h * HEAD_PAD:(h + 1) * HEAD_PAD]) for h in range(MLA_HEADS)], axis=1)
    q_out[...] = (q * q_scale).astype(BF16)
    kr = rope(c[:, q_lora + kv_lora:])
    k = _dot(ckvn, wuk_ref[...]) + jnp.concatenate([kr] * MLA_HEADS, axis=1)
    k_out[...] = k.astype(BF16)
    vt = _dot_nt(wuvt_ref[...], ckvn)
    ones = jnp.ones((HEAD_PAD - MLA_V, tm), F32)
    pieces = []
    for h in range(MLA_HEADS):
        pieces += [vt[h * MLA_V:(h + 1) * MLA_V], ones]
    va_out[0] = jnp.concatenate(pieces, axis=0).astype(BF16)


def _aligned(x, m):
    return x if isinstance(x, int) else pl.multiple_of(x, m)


def _attn_kernel(q_ref, k_ref, vt_ref, o_ref, s_0, s_1, p_0, p_1, p_2, p_3, acc_scr, *,
                 tq, tk, vchunk, n_q, n_k):
    s_buf, p_buf = (s_0, s_1), (p_0, p_1, p_2, p_3)
    n_slots = len(p_buf)
    sub = tk // vchunk
    n_total = n_q * n_k
    assert n_k % n_slots == 0 and n_slots % 2 == 0 and n_total >= 6

    def scores(j, par):
        qoff = _aligned((j // n_k) * tq, tq)
        koff = _aligned((j % n_k) * tk, tk)
        cmax = []
        for hh in range(2):
            cols = slice(hh * HEAD_PAD, (hh + 1) * HEAD_PAD)
            s = _dot_nt(k_ref[pl.ds(koff, tk), cols], q_ref[pl.ds(qoff, tq), cols])
            s_buf[par][hh] = s
            cmax.append(jnp.max(s, axis=0, keepdims=True))
        return tuple(cmax)

    def pv_issue(j, slot):
        kb = j % n_k
        out = []
        for hh in range(2):
            vt = jnp.concatenate([vt_ref[kb * sub + c, hh * HEAD_PAD:(hh + 1) * HEAD_PAD, :]
                                  for c in range(sub)], axis=1)
            out.append(_dot(vt, p_buf[slot][hh]))
        return out

    def pv_accumulate(r, alpha):
        for hh in range(2):
            acc_scr[hh] = alpha[hh] * acc_scr[hh] + r[hh]

    def softmax(j, par, slot, m, cmax):
        first = (j % n_k) == 0
        m_out, alpha = [], []
        for hh in range(2):
            m_old = jnp.where(first, -jnp.inf, m[hh])
            m_new = jnp.maximum(m_old, cmax[hh])
            p_buf[slot][hh] = jnp.exp2(s_buf[par][hh] - m_new).astype(BF16)
            alpha.append(jnp.exp2(m_old - m_new))
            m_out.append(m_new)
        return tuple(m_out), tuple(alpha)

    def finalize(j):
        qoff = _aligned((j // n_k) * tq, tq)
        o_t = jnp.concatenate([acc_scr[hh, :MLA_V] / acc_scr[hh, MLA_V:MLA_V + 1] for hh in range(2)],
                              axis=0)
        o_ref[pl.ds(qoff, tq), :] = o_t.T.astype(BF16)

    def pv(j, alpha, r=None):
        pv_accumulate(pv_issue(j, j % n_slots) if r is None else r, alpha)
        if isinstance(j, int) and j % n_k == n_k - 1:
            finalize(j)

    def body(k, kmod, m, alpha_prev, alpha_cur, cmax):
        par = kmod % 2
        r = pv_issue(k - 1, (kmod - 1) % n_slots)
        cmax_next = scores(k + 2, par)
        m, alpha_next = softmax(k + 1, 1 - par, (kmod + 1) % n_slots, m, cmax)
        pv(k - 1, alpha_prev, r)
        return m, alpha_cur, alpha_next, cmax_next

    def loop_body(kk, carry):
        k0 = n_slots * kk + 1
        for o in range(n_slots):
            carry = body(k0 + o, (1 + o) % n_slots, *carry)
        last = k0 + n_slots - 2
        pl.when(last % n_k == n_k - 1)(functools.partial(finalize, last))
        return carry

    acc_scr[...] = jnp.zeros_like(acc_scr)
    m = tuple(jnp.full((1, tq), -jnp.inf, F32) for _ in range(2))
    cmax0 = scores(0, 0)
    cmax1 = scores(1, 1)
    m, alpha0 = softmax(0, 0, 0, m, cmax0)
    cmax2 = scores(2, 0)
    m, alpha1 = softmax(1, 1, 1, m, cmax1)
    n_iter = (n_total - 3) // n_slots
    carry = lax.fori_loop(0, n_iter, loop_body, (m, alpha0, alpha1, cmax2))
    k = n_slots * n_iter + 1
    while k <= n_total - 3:
        carry = body(k, k % n_slots, *carry)
        k += 1
    m, alpha_prev, alpha_cur, cmax = carry
    r = pv_issue(k - 1, (k - 1) % n_slots)
    m, alpha_last = softmax(k + 1, (k + 1) % 2, (k + 1) % n_slots, m, cmax)
    pv(k - 1, alpha_prev, r)
    pv(n_total - 2, alpha_cur)
    pv(n_total - 1, alpha_last)


def _scan_lanes(x, op, fill, reverse):
    n = x.shape[1]
    lane = lax.broadcasted_iota(jnp.int32, x.shape, 1)
    shift = 1
    while shift < n:
        if reverse:
            moved, valid = pltpu.roll(x, n - shift, 1), lane < n - shift
        else:
            moved, valid = pltpu.roll(x, shift, 1), lane >= shift
        x = op(x, jnp.where(valid, moved, fill))
        shift *= 2
    return x


class _Chain:
    pass


def _mlstm_kernel(q_ref, k_ref, vt_ref, mot_ref, gate_ref, og_ref, o_ref,
                  hacc, st_scr, b_scr, a_scr, cm_scr, bl_scr, am_scr, *, L, nc):
    dh = MLSTM_DH
    aug = st_scr.shape[1]
    s_idx = lax.broadcasted_iota(jnp.int32, (L, L), 0)
    t_idx = lax.broadcasted_iota(jnp.int32, (L, L), 1)
    eye = s_idx == t_idx
    visible = (s_idx <= t_idx, s_idx >= t_idx)
    ones_blk = (lax.broadcasted_iota(jnp.int32, (aug - dh, L), 0) == 0).astype(BF16)

    for d in range(2):
        logi = gate_ref[d, 0, 0]
        logf = gate_ref[d, 1, 0]
        b = _scan_lanes(logf, jnp.add, 0.0, reverse=d == 1)
        a = logi - b
        b_scr[d] = b
        a_scr[d] = a
        cm_scr[d] = _scan_lanes(a, jnp.maximum, -jnp.inf, reverse=d == 1)
        bl_scr[d] = jnp.sum(logf, axis=1, keepdims=True)
        am_scr[d] = jnp.max(a, axis=1, keepdims=True)
    st_scr[...] = jnp.zeros_like(st_scr)

    def prepare(d, c, m):
        x = _Chain()
        off = pl.multiple_of(c * L, L)
        x.q = q_ref[pl.ds(off, L), :]
        x.k = k_ref[pl.ds(off, L), :]
        x.vaug = jnp.concatenate([vt_ref[c], ones_blk], axis=0)
        x.b_row = b_scr[d, pl.ds(c, 1), :]
        x.a_row = a_scr[d, pl.ds(c, 1), :]
        b_last = bl_scr[d, pl.ds(c, 1), :]
        x.g_row = jnp.maximum(cm_scr[d, pl.ds(c, 1), :], m)
        x.iw = jnp.exp(m - x.g_row)
        x.m_new = jnp.maximum(b_last + m, b_last + am_scr[d, pl.ds(c, 1), :])
        x.decay = jnp.exp(b_last + m - x.m_new)
        x.w_row = jnp.exp(b_last + x.a_row - x.m_new)
        return x

    def intra(d, x):
        a_col = jnp.sum(jnp.where(eye, x.a_row, 0.0), axis=1, keepdims=True)
        e_t = jnp.exp(jnp.where(visible[d], a_col - x.g_row, -jnp.inf))
        x.r = _dot(x.vaug, (x.st * e_t).astype(BF16))

    def finish(x):
        num = x.iw * x.inter[:dh] + x.r[:dh]
        den = x.iw * x.inter[dh:dh + 1] + x.r[dh:dh + 1]
        floor = jnp.exp(-(x.b_row + x.g_row))
        return num * (1.0 / jnp.maximum(jnp.abs(den), floor))

    def run(j0, m_f, m_b):
        chains = []
        for o in range(MLSTM_POSITIONS_PER_BODY):
            j = j0 + o
            xf = prepare(0, j, m_f)
            xb = prepare(1, nc - 1 - j, m_b)
            m_f, m_b = xf.m_new, xb.m_new
            chains += [(0, j, xf), (1, nc - 1 - j, xb)]
        for d, _, x in chains:
            x.st = _dot_nt(x.k, x.q)
            x.upd = _dot((x.vaug.astype(F32) * x.w_row).astype(BF16), x.k)
        for d, _, x in chains:
            x.inter = _dot_nt(st_scr[d].astype(BF16), x.q)
            st_scr[d] = x.decay * st_scr[d] + x.upd
        for d, _, x in chains:
            intra(d, x)
        return [(c, finish(x)) for _, c, x in chains], m_f, m_b

    def first_touch(jj, carry):
        outs, m_f, m_b = run(jj * MLSTM_POSITIONS_PER_BODY, *carry)
        for c, h in outs:
            hacc[c] = h
        return m_f, m_b

    def second_touch(jj, carry):
        outs, m_f, m_b = run(jj * MLSTM_POSITIONS_PER_BODY, *carry)
        for c, h in outs:
            tot = hacc[c] + h
            y = tot * lax.rsqrt(jnp.mean(tot * tot, axis=0, keepdims=True) + NORM_EPS) * og_ref[...]
            o_ref[c] = (y * mot_ref[c].astype(F32)).astype(BF16)
        return m_f, m_b

    n_iter = nc // MLSTM_POSITIONS_PER_BODY
    m0 = jnp.zeros((1, 1), F32)
    carry = lax.fori_loop(0, n_iter // 2, first_touch, (m0, m0))
    lax.fori_loop(n_iter // 2, n_iter, second_touch, carry)


def _merge_mlp_kernel(x_ref, ya_ref, ymt_ref, g1_ref, wgab_ref, wbm_ref, wbl_ref, wout_ref,
                      g2_ref, wup_ref, wdn_ref, gf_ref, o_ref, *, d, final_norm):
    x = x_ref[...]
    hn = _rms(x, g1_ref[...]).astype(BF16)
    gates = jax.nn.sigmoid(_dot(hn, wgab_ref[...]))
    merged = (gates[:, :d] * _dot(ya_ref[...], wbm_ref[...])
              + gates[:, d:] * _dot_tn(ymt_ref[0], wbl_ref[...]))
    x1 = x + _dot(merged.astype(BF16), wout_ref[...])
    u = _dot(_rms(x1, g2_ref[...]).astype(BF16), wup_ref[...])
    r = jnp.maximum(u, 0.0)
    x2 = x1 + _dot((r * r).astype(BF16), wdn_ref[...])
    o_ref[...] = _rms(x2, gf_ref[...]) if final_norm else x2


def _const_spec(shape):
    return pl.BlockSpec(shape, lambda *_: (0,) * len(shape))


def _resident_spec(shape):
    return pl.BlockSpec(shape, lambda *_: (0,) * len(shape), pipeline_mode=pl.Buffered(1))


def _params(n_axes):
    return pltpu.CompilerParams(dimension_semantics=("arbitrary",) * n_axes,
                                vmem_limit_bytes=VMEM_LIMIT_BYTES)


def _rope_kernel(pos_ref, freq_ref, cos_out, sin_out):
    t = pos_ref.shape[1]
    ang = freq_ref[...] * pos_ref[...]
    cos, sin = jnp.cos(ang), jnp.sin(ang)
    pad = jnp.zeros((HEAD_PAD - MLA_NOPE - MLA_ROPE, t), F32)
    cos_out[...] = jnp.concatenate([jnp.ones((MLA_NOPE, t), F32), cos, cos, pad], axis=0)
    sin_out[...] = jnp.concatenate([jnp.zeros((MLA_NOPE, t), F32), -sin, sin, pad], axis=0)


def _rope_tables(positions):
    t = positions.size
    inv_freq = ROPE_THETA ** (-jnp.arange(0, MLA_ROPE, 2, dtype=F32) / MLA_ROPE)
    table = jax.ShapeDtypeStruct((HEAD_PAD, t), F32)
    tile = min(t, 16 * LANES)
    return pl.pallas_call(
        _rope_kernel,
        grid=(t // tile,),
        in_specs=[pl.BlockSpec((1, tile), lambda i: (0, i)), _const_spec((MLA_ROPE // 2, 1))],
        out_specs=[pl.BlockSpec((HEAD_PAD, tile), lambda i: (0, i))] * 2,
        out_shape=[table, table], compiler_params=_params(1), name="rope_table",
    )(positions.astype(F32).reshape(1, t), inv_freq.reshape(-1, 1))


def _pad_heads(w, width_in, offset_out):
    k = w.shape[0]
    w = w.reshape(k, MLA_HEADS, width_in)
    w = jnp.pad(w, ((0, 0), (0, 0), (offset_out, HEAD_PAD - width_in - offset_out)))
    return w.reshape(k, MLA_HEADS * HEAD_PAD)


def kernel(x, positions, norm_mix_g, w_in, mla_q_norm_g, mla_w_uq, mla_kv_norm_g, mla_w_ukv, mlstm_conv_w, mlstm_conv_b, mlstm_igate_b, mlstm_fgate_b, mlstm_out_norm_g, w_branch_mla, w_branch_mlstm, w_out, norm_mlp_g, w_mlp_up, w_mlp_down, norm_final_g):
    B, S, D = x.shape
    T = B * S
    depth = w_in.shape[0]
    q_lora = mla_q_norm_g.shape[1]
    kv_lora = mla_kv_norm_g.shape[1]
    mw = MLSTM_HEADS * MLSTM_DH
    n_gate = 4 * MLSTM_HEADS
    L = MLSTM_CHUNK
    nc = S // L
    assert S % (2 * MLSTM_POSITIONS_PER_BODY * L) == 0 and MLA_HEADS % 2 == 0

    cos_t, sin_t = _rope_tables(positions)
    xf = x.reshape(T, D)

    for l in range(depth):
        offs, o = [], 0
        for w in (q_lora, kv_lora, MLA_ROPE, mw, mw, mw, mw, n_gate, D, D):
            offs.append(o)
            o += w
        wl = w_in[l]
        w_cq = wl[:, offs[0]:offs[0] + q_lora]
        w_ckv = wl[:, offs[1]:offs[1] + kv_lora]
        w_kr = wl[:, offs[2]:offs[2] + MLA_ROPE]
        pad_kr = ((0, 0), (MLA_NOPE, HEAD_PAD - MLA_NOPE - MLA_ROPE))
        w1 = wl[:, offs[3]:offs[3] + 2 * mw].astype(BF16)
        w2t = wl[:, offs[5]:offs[5] + 2 * mw].T.astype(BF16)
        w3 = jnp.concatenate([w_cq, w_ckv, jnp.pad(w_kr, pad_kr)], axis=1).astype(BF16)
        w2t = jnp.concatenate([w2t, wl[:, offs[7]:offs[7] + n_gate].T.astype(BF16)], axis=0)
        wgab = wl[:, offs[8]:offs[8] + 2 * D].astype(BF16)
        gate_b = jnp.stack([mlstm_igate_b[l], mlstm_fgate_b[l]], axis=1).reshape(n_gate, 1)

        wqa = _pad_heads(mla_w_uq[l], MLA_NOPE + MLA_ROPE, 0).astype(BF16)
        wkv = mla_w_ukv[l].reshape(kv_lora, MLA_HEADS, MLA_NOPE + MLA_V)
        wuk = _pad_heads(wkv[:, :, :MLA_NOPE].reshape(kv_lora, -1), MLA_NOPE, 0).astype(BF16)
        wuvt = wkv[:, :, MLA_NOPE:].reshape(kv_lora, MLA_HEADS * MLA_V).T.astype(BF16)

        cw = jnp.pad(mlstm_conv_w[l].reshape(CONV_WIDTH, 2 * mw), ((0, 8 - CONV_WIDTH), (0, 0)))
        cb = mlstm_conv_b[l].reshape(1, 2 * mw)

        tm = 256
        tiles_per_seq = S // tm
        hb = tm // HALO
        n_halo_blocks = T // HALO
        row = lambda i: (i, 0)
        col = lambda i: (0, i)
        chunk = lambda i: (i, 0, 0)
        kern = functools.partial(
            _inproj_kernel, tm=tm, tiles_per_seq=tiles_per_seq, q_lora=q_lora, kv_lora=kv_lora, mw=mw,
            q_scale=(MLA_NOPE + MLA_ROPE) ** -0.5 * math.log2(math.e), k_scale=MLSTM_DH ** -0.5)
        outs = pl.pallas_call(
            kern,
            grid=(T // tm,),
            in_specs=[
                pl.BlockSpec((tm, D), row),
                pl.BlockSpec((HALO, D), lambda i: (jnp.maximum(i * hb - 1, 0), 0)),
                pl.BlockSpec((HALO, D), lambda i: (jnp.minimum((i + 1) * hb, n_halo_blocks - 1), 0)),
                pl.BlockSpec((HEAD_PAD, tm), col),
                pl.BlockSpec((HEAD_PAD, tm), col),
                _const_spec((1, D)),
                _const_spec(w1.shape), _const_spec(w2t.shape), _const_spec(w3.shape),
                _const_spec(gate_b.shape),
                _const_spec(cw.shape), _const_spec(cb.shape),
                _const_spec((1, q_lora)), _const_spec(wqa.shape),
                _const_spec((1, kv_lora)), _const_spec(wuk.shape), _const_spec(wuvt.shape),
            ],
            out_specs=[
                pl.BlockSpec((tm, MLA_HEADS * HEAD_PAD), row),
                pl.BlockSpec((tm, MLA_HEADS * HEAD_PAD), row),
                pl.BlockSpec((1, MLA_HEADS * HEAD_PAD, tm), chunk),
                pl.BlockSpec((tm, mw), row), pl.BlockSpec((tm, mw), row),
                pl.BlockSpec((1, mw, tm), chunk),
                pl.BlockSpec((1, mw, tm), chunk),
                pl.BlockSpec((n_gate, tm), col),
            ],
            out_shape=[
                jax.ShapeDtypeStruct((T, MLA_HEADS * HEAD_PAD), BF16),
                jax.ShapeDtypeStruct((T, MLA_HEADS * HEAD_PAD), BF16),
                jax.ShapeDtypeStruct((T // tm, MLA_HEADS * HEAD_PAD, tm), BF16),
                jax.ShapeDtypeStruct((T, mw), BF16), jax.ShapeDtypeStruct((T, mw), BF16),
                jax.ShapeDtypeStruct((T // tm, mw, tm), BF16),
                jax.ShapeDtypeStruct((T // tm, mw, tm), BF16),
                jax.ShapeDtypeStruct((n_gate, T), F32),
            ],
            scratch_shapes=[pltpu.VMEM((tm + 2 * HALO, 2 * mw), F32)],
            compiler_params=_params(1),
            name="inproj",
        )(xf, xf, xf, cos_t, sin_t, norm_mix_g[l].reshape(1, D), w1, w2t, w3, gate_b, cw, cb,
          mla_q_norm_g[l].reshape(1, q_lora), wqa, mla_kv_norm_g[l].reshape(1, kv_lora), wuk, wuvt)
        q_a, k_a, v_a, q_m, k_m, v_m, mo_s, gates = outs

        tq, tk = 256, 1024
        vchunk = tm
        y_attn = pl.pallas_call(
            functools.partial(_attn_kernel, tq=tq, tk=tk, vchunk=vchunk, n_q=S // tq, n_k=S // tk),
            grid=(B, MLA_HEADS // 2),
            in_specs=[
                pl.BlockSpec((S, 2 * HEAD_PAD), lambda b, p: (b, p)),
                pl.BlockSpec((S, 2 * HEAD_PAD), lambda b, p: (b, p)),
                pl.BlockSpec((S // vchunk, 2 * HEAD_PAD, vchunk), lambda b, p: (b, p, 0)),
            ],
            out_specs=pl.BlockSpec((S, 2 * MLA_V), lambda b, p: (b, p)),
            out_shape=jax.ShapeDtypeStruct((T, MLA_HEADS * MLA_V), BF16),
            scratch_shapes=[pltpu.VMEM((2, tk, tq), F32), pltpu.VMEM((2, tk, tq), F32),
                            pltpu.VMEM((2, tk, tq), BF16), pltpu.VMEM((2, tk, tq), BF16),
                            pltpu.VMEM((2, tk, tq), BF16), pltpu.VMEM((2, tk, tq), BF16),
                            pltpu.VMEM((2, HEAD_PAD, tq), F32)],
            compiler_params=_params(2),
            name="mla_attn",
        )(q_a, k_a, v_a)

        gates5 = gates.reshape(2, 2, MLSTM_HEADS, T // L, L)
        assert tm == L
        head_blk = lambda b, h: (b, h)
        head_blk_t = lambda b, h: (b, h, 0)
        aug = MLSTM_DH + BF16_SUBLANES
        row_scratch = pltpu.VMEM((2, nc, L), F32)
        y_mlstm_t = pl.pallas_call(
            functools.partial(_mlstm_kernel, L=L, nc=nc),
            grid=(B, MLSTM_HEADS),
            in_specs=[
                pl.BlockSpec((S, MLSTM_DH), head_blk), pl.BlockSpec((S, MLSTM_DH), head_blk),
                pl.BlockSpec((nc, MLSTM_DH, L), head_blk_t), pl.BlockSpec((nc, MLSTM_DH, L), head_blk_t),
                pl.BlockSpec((2, 2, 1, nc, L), lambda b, h: (0, 0, h, b, 0)),
                pl.BlockSpec((MLSTM_DH, 1), lambda b, h: (h, 0)),
            ],
            out_specs=pl.BlockSpec((nc, MLSTM_DH, L), head_blk_t),
            out_shape=jax.ShapeDtypeStruct((T // L, mw, L), BF16),
            scratch_shapes=[pltpu.VMEM((nc, MLSTM_DH, L), F32),
                            pltpu.VMEM((2, aug, MLSTM_DH), F32),
                            row_scratch, row_scratch, row_scratch,
                            pltpu.VMEM((2, nc, 1), F32), pltpu.VMEM((2, nc, 1), F32)],
            compiler_params=_params(2),
            name="mlstm",
        )(q_m, k_m, v_m, mo_s, gates5, mlstm_out_norm_g[l].reshape(mw, 1))

        tm4 = L
        last = l == depth - 1
        gf = norm_final_g.reshape(1, D)
        weights4 = (wgab, w_branch_mla[l].astype(BF16), w_branch_mlstm[l].astype(BF16),
                    w_out[l].astype(BF16), w_mlp_up[l].astype(BF16), w_mlp_down[l].astype(BF16))
        xf = pl.pallas_call(
            functools.partial(_merge_mlp_kernel, d=D, final_norm=last),
            grid=(T // tm4,),
            in_specs=[
                pl.BlockSpec((tm4, D), row),
                pl.BlockSpec((tm4, MLA_HEADS * MLA_V), row),
                pl.BlockSpec((1, mw, tm4), lambda i: (i, 0, 0)),
                _const_spec((1, D)),
                _resident_spec(weights4[0].shape), _resident_spec(weights4[1].shape),
                _resident_spec(weights4[2].shape), _resident_spec(weights4[3].shape),
                _const_spec((1, D)),
                _resident_spec(weights4[4].shape), _resident_spec(weights4[5].shape),
                _const_spec((1, D)),
            ],
            out_specs=pl.BlockSpec((tm4, D), row),
            out_shape=jax.ShapeDtypeStruct((T, D), F32),
            compiler_params=_params(1),
            name="merge_mlp",
        )(xf, y_attn, y_mlstm_t, norm_mix_g[l].reshape(1, D), weights4[0], weights4[1], weights4[2],
          weights4[3], norm_mlp_g[l].reshape(1, D), weights4[4], weights4[5], gf)

    return xf.reshape(B, S, D)
```

```python
import functools
import math

import jax
import jax.numpy as jnp
from jax import lax
from jax.experimental import pallas as pl
from jax.experimental.pallas import tpu as pltpu

MLA_HEADS = 8
MLA_NOPE = 64
MLA_ROPE = 32
MLA_V = 64
ROPE_THETA = 10000.0
MLSTM_HEADS = 4
MLSTM_DH = 128
CONV_WIDTH = 5
NORM_EPS = 1e-6

LANES = 128
BF16_SUBLANES = 16
VMEM_LIMIT_BYTES = 56 * 1024 * 1024

HEAD_PAD = LANES
MLSTM_CHUNK = 256
MLSTM_POSITIONS_PER_BODY = 4
HALO = BF16_SUBLANES

F32 = jnp.float32
BF16 = jnp.bfloat16


def _rms(x, g):
    return x * lax.rsqrt(jnp.mean(x * x, axis=-1, keepdims=True) + NORM_EPS) * g


def _dot(a, b):
    return jnp.dot(a, b, preferred_element_type=F32)


def _dot_nt(a, b):
    return lax.dot_general(a, b, (((1,), (1,)), ((), ())), preferred_element_type=F32)


def _dot_tn(a, b):
    return lax.dot_general(a, b, (((0,), (0,)), ((), ())), preferred_element_type=F32)


def _log_sigmoid(x):
    return jnp.minimum(x, 0.0) - jnp.log1p(jnp.exp(-jnp.abs(x)))


def _inproj_kernel(xm_ref, xp_ref, xn_ref, cos_ref, sin_ref, g_ref, w1_ref, w2t_ref, w3_ref,
                   gb_ref, cw_ref, cb_ref, qg_ref, wqa_ref, kvg_ref,
                   wuk_ref, wuvt_ref,
                   q_out, k_out, va_out, qm_out, km_out, vm_out, mo_out, gate_out,
                   *, tm, tiles_per_seq, q_lora, kv_lora, mw, q_scale, k_scale):
    i = pl.program_id(0)
    pos_in_seq = i % tiles_per_seq
    xp = jnp.where(pos_in_seq == 0, 0.0, xp_ref[...])
    xn = jnp.where(pos_in_seq == tiles_per_seq - 1, 0.0, xn_ref[...])
    xe = jnp.concatenate([xp, xm_ref[...], xn], axis=0)
    he = _rms(xe, g_ref[...]).astype(BF16)
    hm = he[HALO:HALO + tm]

    pre = _dot(he, w1_ref[...])
    rows = pre.shape[0]
    conv = cb_ref[...]
    for j in range(CONV_WIDTH):
        shifted = pre if j == CONV_WIDTH // 2 else pltpu.roll(pre, (CONV_WIDTH // 2 - j) % rows, 0)
        conv = conv + cw_ref[j:j + 1, :] * shifted[HALO:HALO + tm]
    qk = conv * jax.nn.sigmoid(conv)
    qm_out[...] = qk[:, :mw].astype(BF16)
    km_out[...] = (qk[:, mw:] * k_scale).astype(BF16)

    vo_t = _dot_nt(w2t_ref[...], hm)
    vm_out[0] = vo_t[:mw].astype(BF16)
    mo_out[0] = jax.nn.sigmoid(vo_t[mw:2 * mw]).astype(BF16)

    gt = vo_t[2 * mw:] + gb_ref[...]
    row = lax.broadcasted_iota(jnp.int32, gt.shape, 0)
    is_f = (row % (2 * MLSTM_HEADS)) >= MLSTM_HEADS
    gate_out[...] = jnp.where(is_f, _log_sigmoid(gt), gt)

    c = _dot(hm, w3_ref[...])
    cqn = _rms(c[:, :q_lora], qg_ref[...]).astype(BF16)
    ckvn = _rms(c[:, q_lora:q_lora + kv_lora], kvg_ref[...]).astype(BF16)
    cos = cos_ref[...].T
    sin = sin_ref[...].T
    half = MLA_ROPE // 2
    lane = lax.broadcasted_iota(jnp.int32, sin.shape, 1)
    sin_lo = jnp.where(lane < MLA_NOPE + half, sin, 0.0)
    sin_hi = sin - sin_lo

    def rope(x):
        return (x * cos + pltpu.roll(x, HEAD_PAD - half, 1) * sin_lo + pltpu.roll(x, half, 1) * sin_hi)

    qa = _dot(cqn, wqa_ref[...])
    q = jnp.concatenate([rope(qa[:, h * HEAD_PAD:(h + 1) * HEAD_PAD]) for h in range(MLA_HEADS)], axis=1)
    q_out[...] = (q * q_scale).astype(BF16)
    kr = rope(c[:, q_lora + kv_lora:])
    k = _dot(ckvn, wuk_ref[...]) + jnp.concatenate([kr] * MLA_HEADS, axis=1)
    k_out[...] = k.astype(BF16)
    vt = _dot_nt(wuvt_ref[...], ckvn)
    ones = jnp.ones((HEAD_PAD - MLA_V, tm), F32)
    pieces = []
    for h in range(MLA_HEADS):
        pieces += [vt[h * MLA_V:(h + 1) * MLA_V], ones]
    va_out[0] = jnp.concatenate(pieces, axis=0).astype(BF16)


def _aligned(x, m):
    return x if isinstance(x, int) else pl.multiple_of(x, m)


def _attn_kernel(q_ref, k_ref, vt_ref, o_ref, s_0, s_1, p_0, p_1, p_2, p_3, acc_scr, *,
                 tq, tk, vchunk, n_q, n_k):
    s_buf, p_buf = (s_0, s_1), (p_0, p_1, p_2, p_3)
    n_slots = len(p_buf)
    sub = tk // vchunk
    n_total = n_q * n_k
    assert n_k % n_slots == 0 and n_slots % 2 == 0 and n_total >= 6

    def scores(j, par):
        qoff = _aligned((j // n_k) * tq, tq)
        koff = _aligned((j % n_k) * tk, tk)
        cmax = []
        for hh in range(2):
            cols = slice(hh * HEAD_PAD, (hh + 1) * HEAD_PAD)
            s = _dot_nt(k_ref[pl.ds(koff, tk), cols], q_ref[pl.ds(qoff, tq), cols])
            s_buf[par][hh] = s
            cmax.append(jnp.max(s, axis=0, keepdims=True))
        return tuple(cmax)

    def pv_issue(j, slot):
        kb = j % n_k
        out = []
        for hh in range(2):
            vt = jnp.concatenate([vt_ref[kb * sub + c, hh * HEAD_PAD:(hh + 1) * HEAD_PAD, :]
                                  for c in range(sub)], axis=1)
            out.append(_dot(vt, p_buf[slot][hh]))
        return out

    def pv_accumulate(r, alpha):
        for hh in range(2):
            acc_scr[hh] = alpha[hh] * acc_scr[hh] + r[hh]

    def softmax(j, par, slot, m, cmax):
        first = (j % n_k) == 0
        m_out, alpha = [], []
        for hh in range(2):
            m_old = jnp.where(first, -jnp.inf, m[hh])
            m_new = jnp.maximum(m_old, cmax[hh])
            p_buf[slot][hh] = jnp.exp2(s_buf[par][hh] - m_new).astype(BF16)
            alpha.append(jnp.exp2(m_old - m_new))
            m_out.append(m_new)
        return tuple(m_out), tuple(alpha)

    def finalize(j):
        qoff = _aligned((j // n_k) * tq, tq)
        o_t = jnp.concatenate([acc_scr[hh, :MLA_V] / acc_scr[hh, MLA_V:MLA_V + 1] for hh in range(2)],
                              axis=0)
        o_ref[pl.ds(qoff, tq), :] = o_t.T.astype(BF16)

    def pv(j, alpha, r=None):
        pv_accumulate(pv_issue(j, j % n_slots) if r is None else r, alpha)
        if isinstance(j, int) and j % n_k == n_k - 1:
            finalize(j)

    def body(k, kmod, m, alpha_prev, alpha_cur, cmax):
        par = kmod % 2
        r = pv_issue(k - 1, (kmod - 1) % n_slots)
        cmax_next = scores(k + 2, par)
        m, alpha_next = softmax(k + 1, 1 - par, (kmod + 1) % n_slots, m, cmax)
        pv(k - 1, alpha_prev, r)
        return m, alpha_cur, alpha_next, cmax_next

    def loop_body(kk, carry):
        k0 = n_slots * kk + 1
        for o in range(n_slots):
            carry = body(k0 + o, (1 + o) % n_slots, *carry)
        last = k0 + n_slots - 2
        pl.when(last % n_k == n_k - 1)(functools.partial(finalize, last))
        return carry

    acc_scr[...] = jnp.zeros_like(acc_scr)
    m = tuple(jnp.full((1, tq), -jnp.inf, F32) for _ in range(2))
    cmax0 = scores(0, 0)
    cmax1 = scores(1, 1)
    m, alpha0 = softmax(0, 0, 0, m, cmax0)
    cmax2 = scores(2, 0)
    m, alpha1 = softmax(1, 1, 1, m, cmax1)
    n_iter = (n_total - 3) // n_slots
    carry = lax.fori_loop(0, n_iter, loop_body, (m, alpha0, alpha1, cmax2))
    k = n_slots * n_iter + 1
    while k <= n_total - 3:
        carry = body(k, k % n_slots, *carry)
        k += 1
    m, alpha_prev, alpha_cur, cmax = carry
    r = pv_issue(k - 1, (k - 1) % n_slots)
    m, alpha_last = softmax(k + 1, (k + 1) % 2, (k + 1) % n_slots, m, cmax)
    pv(k - 1, alpha_prev, r)
    pv(n_total - 2, alpha_cur)
    pv(n_total - 1, alpha_last)


def _scan_lanes(x, op, fill, reverse):
    n = x.shape[1]
    lane = lax.broadcasted_iota(jnp.int32, x.shape, 1)
    shift = 1
    while shift < n:
        if reverse:
            moved, valid = pltpu.roll(x, n - shift, 1), lane < n - shift
        else:
            moved, valid = pltpu.roll(x, shift, 1), lane >= shift
        x = op(x, jnp.where(valid, moved, fill))
        shift *= 2
    return x


class _Chain:
    pass


def _mlstm_kernel(q_ref, k_ref, vt_ref, mot_ref, gate_ref, og_ref, o_ref,
                  hacc, st_scr, b_scr, a_scr, cm_scr, bl_scr, am_scr, *, L, nc):
    dh = MLSTM_DH
    aug = st_scr.shape[1]
    s_idx = lax.broadcasted_iota(jnp.int32, (L, L), 0)
    t_idx = lax.broadcasted_iota(jnp.int32, (L, L), 1)
    eye = s_idx == t_idx
    visible = (s_idx <= t_idx, s_idx >= t_idx)
    ones_blk = (lax.broadcasted_iota(jnp.int32, (aug - dh, L), 0) == 0).astype(BF16)

    for d in range(2):
        logi = gate_ref[d, 0, 0]
        logf = gate_ref[d, 1, 0]
        b = _scan_lanes(logf, jnp.add, 0.0, reverse=d == 1)
        a = logi - b
        b_scr[d] = b
        a_scr[d] = a
        cm_scr[d] = _scan_lanes(a, jnp.maximum, -jnp.inf, reverse=d == 1)
        bl_scr[d] = jnp.sum(logf, axis=1, keepdims=True)
        am_scr[d] = jnp.max(a, axis=1, keepdims=True)
    st_scr[...] = jnp.zeros_like(st_scr)

    def prepare(d, c, m):
        x = _Chain()
        off = pl.multiple_of(c * L, L)
        x.q = q_ref[pl.ds(off, L), :]
        x.k = k_ref[pl.ds(off, L), :]
        x.vaug = jnp.concatenate([vt_ref[c], ones_blk], axis=0)
        x.b_row = b_scr[d, pl.ds(c, 1), :]
        x.a_row = a_scr[d, pl.ds(c, 1), :]
        b_last = bl_scr[d, pl.ds(c, 1), :]
        x.g_row = jnp.maximum(cm_scr[d, pl.ds(c, 1), :], m)
        x.iw = jnp.exp(m - x.g_row)
        x.m_new = jnp.maximum(b_last + m, b_last + am_scr[d, pl.ds(c, 1), :])
        x.decay = jnp.exp(b_last + m - x.m_new)
        x.w_row = jnp.exp(b_last + x.a_row - x.m_new)
        return x

    def intra(d, x):
        a_col = jnp.sum(jnp.where(eye, x.a_row, 0.0), axis=1, keepdims=True)
        e_t = jnp.exp(jnp.where(visible[d], a_col - x.g_row, -jnp.inf))
        x.r = _dot(x.vaug, (x.st * e_t).astype(BF16))

    def finish(x):
        num = x.iw * x.inter[:dh] + x.r[:dh]
        den = x.iw * x.inter[dh:dh + 1] + x.r[dh:dh + 1]
        floor = jnp.exp(-(x.b_row + x.g_row))
        return num * (1.0 / jnp.maximum(jnp.abs(den), floor))

    def run(j0, m_f, m_b):
        chains = []
        for o in range(MLSTM_POSITIONS_PER_BODY):
            j = j0 + o
            xf = prepare(0, j, m_f)
            xb = prepare(1, nc - 1 - j, m_b)
            m_f, m_b = xf.m_new, xb.m_new
            chains += [(0, j, xf), (1, nc - 1 - j, xb)]
        for d, _, x in chains:
            x.st = _dot_nt(x.k, x.q)
            x.upd = _dot((x.vaug.astype(F32) * x.w_row).astype(BF16), x.k)
        for d, _, x in chains:
            x.inter = _dot_nt(st_scr[d].astype(BF16), x.q)
            st_scr[d] = x.decay * st_scr[d] + x.upd
        for d, _, x in chains:
            intra(d, x)
        return [(c, finish(x)) for _, c, x in chains], m_f, m_b

    def first_touch(jj, carry):
        outs, m_f, m_b = run(jj * MLSTM_POSITIONS_PER_BODY, *carry)
        for c, h in outs:
            hacc[c] = h
        return m_f, m_b

    def second_touch(jj, carry):
        outs, m_f, m_b = run(jj * MLSTM_POSITIONS_PER_BODY, *carry)
        for c, h in outs:
            tot = hacc[c] + h
            y = tot * lax.rsqrt(jnp.mean(tot * tot, axis=0, keepdims=True) + NORM_EPS) * og_ref[...]
            o_ref[c] = (y * mot_ref[c].astype(F32)).astype(BF16)
        return m_f, m_b

    n_iter = nc // MLSTM_POSITIONS_PER_BODY
    m0 = jnp.zeros((1, 1), F32)
    carry = lax.fori_loop(0, n_iter // 2, first_touch, (m0, m0))
    lax.fori_loop(n_iter // 2, n_iter, second_touch, carry)


def _merge_mlp_kernel(x_ref, ya_ref, ymt_ref, g1_ref, wgab_ref, wbm_ref, wbl_ref, wout_ref,
                      g2_ref, wup_ref, wdn_ref, gf_ref, o_ref, *, d, final_norm):
    x = x_ref[...]
    hn = _rms(x, g1_ref[...]).astype(BF16)
    gates = jax.nn.sigmoid(_dot(hn, wgab_ref[...]))
    merged = (gates[:, :d] * _dot(ya_ref[...], wbm_ref[...])
              + gates[:, d:] * _dot_tn(ymt_ref[0], wbl_ref[...]))
    x1 = x + _dot(merged.astype(BF16), wout_ref[...])
    u = _dot(_rms(x1, g2_ref[...]).astype(BF16), wup_ref[...])
    r = jnp.maximum(u, 0.0)
    x2 = x1 + _dot((r * r).astype(BF16), wdn_ref[...])
    o_ref[...] = _rms(x2, gf_ref[...]) if final_norm else x2


def _const_spec(shape):
    return pl.BlockSpec(shape, lambda *_: (0,) * len(shape))


def _resident_spec(shape):
    return pl.BlockSpec(shape, lambda *_: (0,) * len(shape), pipeline_mode=pl.Buffered(1))


def _params(n_axes):
    return pltpu.CompilerParams(dimension_semantics=("arbitrary",) * n_axes,
                                vmem_limit_bytes=VMEM_LIMIT_BYTES)


def _rope_kernel(pos_ref, freq_ref, cos_out, sin_out):
    t = pos_ref.shape[1]
    ang = freq_ref[...] * pos_ref[...]
    cos, sin = jnp.cos(ang), jnp.sin(ang)
    pad = jnp.zeros((HEAD_PAD - MLA_NOPE - MLA_ROPE, t), F32)
    cos_out[...] = jnp.concatenate([jnp.ones((MLA_NOPE, t), F32), cos, cos, pad], axis=0)
    sin_out[...] = jnp.concatenate([jnp.zeros((MLA_NOPE, t), F32), -sin, sin, pad], axis=0)


def _rope_tables(positions):
    t = positions.size
    inv_freq = ROPE_THETA ** (-jnp.arange(0, MLA_ROPE, 2, dtype=F32) / MLA_ROPE)
    table = jax.ShapeDtypeStruct((HEAD_PAD, t), F32)
    tile = min(t, 16 * LANES)
    return pl.pallas_call(
        _rope_kernel,
        grid=(t // tile,),
        in_specs=[pl.BlockSpec((1, tile), lambda i: (0, i)), _const_spec((MLA_ROPE // 2, 1))],
        out_specs=[pl.BlockSpec((HEAD_PAD, tile), lambda i: (0, i))] * 2,
        out_shape=[table, table], compiler_params=_params(1), name="rope_table",
    )(positions.astype(F32).reshape(1, t), inv_freq.reshape(-1, 1))


def _pad_heads(w, width_in, offset_out):
    k = w.shape[0]
    w = w.reshape(k, MLA_HEADS, width_in)
    w = jnp.pad(w, ((0, 0), (0, 0), (offset_out, HEAD_PAD - width_in - offset_out)))
    return w.reshape(k, MLA_HEADS * HEAD_PAD)


def kernel(x, positions, norm_mix_g, w_in, mla_q_norm_g, mla_w_uq, mla_kv_norm_g, mla_w_ukv, mlstm_conv_w, mlstm_conv_b, mlstm_igate_b, mlstm_fgate_b, mlstm_out_norm_g, w_branch_mla, w_branch_mlstm, w_out, norm_mlp_g, w_mlp_up, w_mlp_down, norm_final_g):
    B, S, D = x.shape
    T = B * S
    depth = w_in.shape[0]
    q_lora = mla_q_norm_g.shape[1]
    kv_lora = mla_kv_norm_g.shape[1]
    mw = MLSTM_HEADS * MLSTM_DH
    n_gate = 4 * MLSTM_HEADS
    L = MLSTM_CHUNK
    nc = S // L
    assert S % (2 * MLSTM_POSITIONS_PER_BODY * L) == 0 and MLA_HEADS % 2 == 0

    cos_t, sin_t = _rope_tables(positions)
    xf = x.reshape(T, D)

    for l in range(depth):
        offs, o = [], 0
        for w in (q_lora, kv_lora, MLA_ROPE, mw, mw, mw, mw, n_gate, D, D):
            offs.append(o)
            o += w
        wl = w_in[l]
        w_cq = wl[:, offs[0]:offs[0] + q_lora]
        w_ckv = wl[:, offs[1]:offs[1] + kv_lora]
        w_kr = wl[:, offs[2]:offs[2] + MLA_ROPE]
        pad_kr = ((0, 0), (MLA_NOPE, HEAD_PAD - MLA_NOPE - MLA_ROPE))
        w1 = wl[:, offs[3]:offs[3] + 2 * mw].astype(BF16)
        w2t = wl[:, offs[5]:offs[5] + 2 * mw].T.astype(BF16)
        w3 = jnp.concatenate([w_cq, w_ckv, jnp.pad(w_kr, pad_kr)], axis=1).astype(BF16)
        w2t = jnp.concatenate([w2t, wl[:, offs[7]:offs[7] + n_gate].T.astype(BF16)], axis=0)
        wgab = wl[:, offs[8]:offs[8] + 2 * D].astype(BF16)
        gate_b = jnp.stack([mlstm_igate_b[l], mlstm_fgate_b[l]], axis=1).reshape(n_gate, 1)

        wqa = _pad_heads(mla_w_uq[l], MLA_NOPE + MLA_ROPE, 0).astype(BF16)
        wkv = mla_w_ukv[l].reshape(kv_lora, MLA_HEADS, MLA_NOPE + MLA_V)
        wuk = _pad_heads(wkv[:, :, :MLA_NOPE].reshape(kv_lora, -1), MLA_NOPE, 0).astype(BF16)
        wuvt = wkv[:, :, MLA_NOPE:].reshape(kv_lora, MLA_HEADS * MLA_V).T.astype(BF16)

        cw = jnp.pad(mlstm_conv_w[l].reshape(CONV_WIDTH, 2 * mw), ((0, 8 - CONV_WIDTH), (0, 0)))
        cb = mlstm_conv_b[l].reshape(1, 2 * mw)

        tm = 256
        tiles_per_seq = S // tm
        hb = tm // HALO
        n_halo_blocks = T // HALO
        row = lambda i: (i, 0)
        col = lambda i: (0, i)
        chunk = lambda i: (i, 0, 0)
        kern = functools.partial(
            _inproj_kernel, tm=tm, tiles_per_seq=tiles_per_seq, q_lora=q_lora, kv_lora=kv_lora, mw=mw,
            q_scale=(MLA_NOPE + MLA_ROPE) ** -0.5 * math.log2(math.e), k_scale=MLSTM_DH ** -0.5)
        outs = pl.pallas_call(
            kern,
            grid=(T // tm,),
            in_specs=[
                pl.BlockSpec((tm, D), row),
                pl.BlockSpec((HALO, D), lambda i: (jnp.maximum(i * hb - 1, 0), 0)),
                pl.BlockSpec((HALO, D), lambda i: (jnp.minimum((i + 1) * hb, n_halo_blocks - 1), 0)),
                pl.BlockSpec((HEAD_PAD, tm), col),
                pl.BlockSpec((HEAD_PAD, tm), col),
                _const_spec((1, D)),
                _const_spec(w1.shape), _const_spec(w2t.shape), _const_spec(w3.shape),
                _const_spec(gate_b.shape),
                _const_spec(cw.shape), _const_spec(cb.shape),
                _const_spec((1, q_lora)), _const_spec(wqa.shape),
                _const_spec((1, kv_lora)), _const_spec(wuk.shape), _const_spec(wuvt.shape),
            ],
            out_specs=[
                pl.BlockSpec((tm, MLA_HEADS * HEAD_PAD), row),
                pl.BlockSpec((tm, MLA_HEADS * HEAD_PAD), row),
                pl.BlockSpec((1, MLA_HEADS * HEAD_PAD, tm), chunk),
                pl.BlockSpec((tm, mw), row), pl.BlockSpec((tm, mw), row),
                pl.BlockSpec((1, mw, tm), chunk),
                pl.BlockSpec((1, mw, tm), chunk),
                pl.BlockSpec((n_gate, tm), col),
            ],
            out_shape=[
                jax.ShapeDtypeStruct((T, MLA_HEADS * HEAD_PAD), BF16),
                jax.ShapeDtypeStruct((T, MLA_HEADS * HEAD_PAD), BF16),
                jax.ShapeDtypeStruct((T // tm, MLA_HEADS * HEAD_PAD, tm), BF16),
                jax.ShapeDtypeStruct((T, mw), BF16), jax.ShapeDtypeStruct((T, mw), BF16),
                jax.ShapeDtypeStruct((T // tm, mw, tm), BF16),
                jax.ShapeDtypeStruct((T // tm, mw, tm), BF16),
                jax.ShapeDtypeStruct((n_gate, T), F32),
            ],
            compiler_params=_params(1),
            name="inproj",
        )(xf, xf, xf, cos_t, sin_t, norm_mix_g[l].reshape(1, D), w1, w2t, w3, gate_b, cw, cb,
          mla_q_norm_g[l].reshape(1, q_lora), wqa, mla_kv_norm_g[l].reshape(1, kv_lora), wuk, wuvt)
        q_a, k_a, v_a, q_m, k_m, v_m, mo_s, gates = outs

        tq, tk = 256, 1024
        vchunk = tm
        y_attn = pl.pallas_call(
            functools.partial(_attn_kernel, tq=tq, tk=tk, vchunk=vchunk, n_q=S // tq, n_k=S // tk),
            grid=(B, MLA_HEADS // 2),
            in_specs=[
                pl.BlockSpec((S, 2 * HEAD_PAD), lambda b, p: (b, p)),
                pl.BlockSpec((S, 2 * HEAD_PAD), lambda b, p: (b, p)),
                pl.BlockSpec((S // vchunk, 2 * HEAD_PAD, vchunk), lambda b, p: (b, p, 0)),
            ],
            out_specs=pl.BlockSpec((S, 2 * MLA_V), lambda b, p: (b, p)),
            out_shape=jax.ShapeDtypeStruct((T, MLA_HEADS * MLA_V), BF16),
            scratch_shapes=[pltpu.VMEM((2, tk, tq), F32), pltpu.VMEM((2, tk, tq), F32),
                            pltpu.VMEM((2, tk, tq), BF16), pltpu.VMEM((2, tk, tq), BF16),
                            pltpu.VMEM((2, tk, tq), BF16), pltpu.VMEM((2, tk, tq), BF16),
                            pltpu.VMEM((2, HEAD_PAD, tq), F32)],
            compiler_params=_params(2),
            name="mla_attn",
        )(q_a, k_a, v_a)

        gates5 = gates.reshape(2, 2, MLSTM_HEADS, T // L, L)
        assert tm == L
        head_blk = lambda b, h: (b, h)
        head_blk_t = lambda b, h: (b, h, 0)
        aug = MLSTM_DH + BF16_SUBLANES
        row_scratch = pltpu.VMEM((2, nc, L), F32)
        y_mlstm_t = pl.pallas_call(
            functools.partial(_mlstm_kernel, L=L, nc=nc),
            grid=(B, MLSTM_HEADS),
            in_specs=[
                pl.BlockSpec((S, MLSTM_DH), head_blk), pl.BlockSpec((S, MLSTM_DH), head_blk),
                pl.BlockSpec((nc, MLSTM_DH, L), head_blk_t), pl.BlockSpec((nc, MLSTM_DH, L), head_blk_t),
                pl.BlockSpec((2, 2, 1, nc, L), lambda b, h: (0, 0, h, b, 0)),
                pl.BlockSpec((MLSTM_DH, 1), lambda b, h: (h, 0)),
            ],
            out_specs=pl.BlockSpec((nc, MLSTM_DH, L), head_blk_t),
            out_shape=jax.ShapeDtypeStruct((T // L, mw, L), BF16),
            scratch_shapes=[pltpu.VMEM((nc, MLSTM_DH, L), F32),
                            pltpu.VMEM((2, aug, MLSTM_DH), F32),
                            row_scratch, row_scratch, row_scratch,
                            pltpu.VMEM((2, nc, 1), F32), pltpu.VMEM((2, nc, 1), F32)],
            compiler_params=_params(2),
            name="mlstm",
        )(q_m, k_m, v_m, mo_s, gates5, mlstm_out_norm_g[l].reshape(mw, 1))

        tm4 = L
        last = l == depth - 1
        gf = norm_final_g.reshape(1, D)
        weights4 = (wgab, w_branch_mla[l].astype(BF16), w_branch_mlstm[l].astype(BF16),
                    w_out[l].astype(BF16), w_mlp_up[l].astype(BF16), w_mlp_down[l].astype(BF16))
        xf = pl.pallas_call(
            functools.partial(_merge_mlp_kernel, d=D, final_norm=last),
            grid=(T // tm4,),
            in_specs=[
                pl.BlockSpec((tm4, D), row),
                pl.BlockSpec((tm4, MLA_HEADS * MLA_V), row),
                pl.BlockSpec((1, mw, tm4), lambda i: (i, 0, 0)),
                _const_spec((1, D)),
                _resident_spec(weights4[0].shape), _resident_spec(weights4[1].shape),
                _resident_spec(weights4[2].shape), _resident_spec(weights4[3].shape),
                _const_spec((1, D)),
                _resident_spec(weights4[4].shape), _resident_spec(weights4[5].shape),
                _const_spec((1, D)),
            ],
            out_specs=pl.BlockSpec((tm4, D), row),
            out_shape=jax.ShapeDtypeStruct((T, D), F32),
            compiler_params=_params(1),
            name="merge_mlp",
        )(xf, y_attn, y_mlstm_t, norm_mix_g[l].reshape(1, D), weights4[0], weights4[1], weights4[2],
          weights4[3], norm_mlp_g[l].reshape(1, D), weights4[4], weights4[5], gf)

    return xf.reshape(B, S, D)
```

```python
import functools
import math

import jax
import jax.numpy as jnp
from jax import lax
from jax.experimental import pallas as pl
from jax.experimental.pallas import tpu as pltpu

MLA_HEADS = 8
MLA_NOPE = 64
MLA_ROPE = 32
MLA_V = 64
ROPE_THETA = 10000.0
MLSTM_HEADS = 4
MLSTM_DH = 128
CONV_WIDTH = 5
NORM_EPS = 1e-6

LANES = 128
F32_SUBLANES = 8
BF16_SUBLANES = 16
VMEM_LIMIT_BYTES = 56 * 1024 * 1024

MXU_WIDTH = 256

HEAD_PAD = LANES
MLSTM_CHUNK = MXU_WIDTH
MLSTM_POSITIONS_PER_BODY = 4
ROW_TILE = MLSTM_CHUNK
ATTN_Q_TILE = MXU_WIDTH
ATTN_KEY_STEP = 4 * MXU_WIDTH
HALO = BF16_SUBLANES
ROPE_TABLE_LANES = 16 * LANES

F32 = jnp.float32
BF16 = jnp.bfloat16


def _rms(x, g):
    return x * lax.rsqrt(jnp.mean(x * x, axis=-1, keepdims=True) + NORM_EPS) * g


def _dot(a, b):
    return jnp.dot(a, b, preferred_element_type=F32)


def _dot_nt(a, b):
    return lax.dot_general(a, b, (((1,), (1,)), ((), ())), preferred_element_type=F32)


def _dot_tn(a, b):
    return lax.dot_general(a, b, (((0,), (0,)), ((), ())), preferred_element_type=F32)


def _log_sigmoid(x):
    return jnp.minimum(x, 0.0) - jnp.log1p(jnp.exp(-jnp.abs(x)))


def _inproj_kernel(xm_ref, xp_ref, xn_ref, cos_ref, sin_ref, g_ref, w1_ref, w2t_ref, w3_ref,
                   gb_ref, cw_ref, cb_ref, qg_ref, wqa_ref, kvg_ref,
                   wuk_ref, wuvt_ref,
                   q_out, k_out, va_out, qm_out, km_out, vm_out, mo_out, gate_out,
                   *, tm, tiles_per_seq, q_lora, kv_lora, mw, q_scale, k_scale):
    i = pl.program_id(0)
    pos_in_seq = i % tiles_per_seq
    xp = jnp.where(pos_in_seq == 0, 0.0, xp_ref[...])
    xn = jnp.where(pos_in_seq == tiles_per_seq - 1, 0.0, xn_ref[...])
    xe = jnp.concatenate([xp, xm_ref[...], xn], axis=0)
    he = _rms(xe, g_ref[...]).astype(BF16)
    hm = he[HALO:HALO + tm]

    pre = _dot(he, w1_ref[...])
    rows = pre.shape[0]
    conv = cb_ref[...]
    for j in range(CONV_WIDTH):
        shifted = pre if j == CONV_WIDTH // 2 else pltpu.roll(pre, (CONV_WIDTH // 2 - j) % rows, 0)
        conv = conv + cw_ref[j:j + 1, :] * shifted[HALO:HALO + tm]
    qk = conv * jax.nn.sigmoid(conv)
    qm_out[...] = qk[:, :mw].astype(BF16)
    km_out[...] = (qk[:, mw:] * k_scale).astype(BF16)

    vo_t = _dot_nt(w2t_ref[...], hm)
    vm_out[0] = vo_t[:mw].astype(BF16)
    mo_out[0] = jax.nn.sigmoid(vo_t[mw:2 * mw]).astype(BF16)

    gt = vo_t[2 * mw:] + gb_ref[...]
    row = lax.broadcasted_iota(jnp.int32, gt.shape, 0)
    is_f = (row % (2 * MLSTM_HEADS)) >= MLSTM_HEADS
    gate_out[...] = jnp.where(is_f, _log_sigmoid(gt), gt)

    c = _dot(hm, w3_ref[...])
    cqn = _rms(c[:, :q_lora], qg_ref[...]).astype(BF16)
    ckvn = _rms(c[:, q_lora:q_lora + kv_lora], kvg_ref[...]).astype(BF16)
    cos = cos_ref[...].T
    sin = sin_ref[...].T
    half = MLA_ROPE // 2
    lane = lax.broadcasted_iota(jnp.int32, sin.shape, 1)
    sin_lo = jnp.where(lane < MLA_NOPE + half, sin, 0.0)
    sin_hi = sin - sin_lo

    def rope(x):
        return (x * cos + pltpu.roll(x, HEAD_PAD - half, 1) * sin_lo + pltpu.roll(x, half, 1) * sin_hi)

    qa = _dot(cqn, wqa_ref[...])
    q = jnp.concatenate([rope(qa[:, h * HEAD_PAD:(h + 1) * HEAD_PAD]) for h in range(MLA_HEADS)], axis=1)
    q_out[...] = (q * q_scale).astype(BF16)
    kr = rope(c[:, q_lora + kv_lora:])
    k = _dot(ckvn, wuk_ref[...]) + jnp.concatenate([kr] * MLA_HEADS, axis=1)
    k_out[...] = k.astype(BF16)
    vt = _dot_nt(wuvt_ref[...], ckvn)
    ones = jnp.ones((HEAD_PAD - MLA_V, tm), F32)
    pieces = []
    for h in range(MLA_HEADS):
        pieces += [vt[h * MLA_V:(h + 1) * MLA_V], ones]
    va_out[0] = jnp.concatenate(pieces, axis=0).astype(BF16)


def _aligned(x, m):
    return x if isinstance(x, int) else pl.multiple_of(x, m)


def _attn_kernel(q_ref, k_ref, vt_ref, o_ref, s_0, s_1, p_0, p_1, p_2, p_3, acc_scr, *,
                 tq, tk, vchunk, n_q, n_k):
    s_buf, p_buf = (s_0, s_1), (p_0, p_1, p_2, p_3)
    n_slots = len(p_buf)
    sub = tk // vchunk
    n_total = n_q * n_k
    assert n_k % n_slots == 0 and n_slots % 2 == 0 and n_total >= 6

    def scores(j, par):
        qoff = _aligned((j // n_k) * tq, tq)
        koff = _aligned((j % n_k) * tk, tk)
        cmax = []
        for hh in range(2):
            cols = slice(hh * HEAD_PAD, (hh + 1) * HEAD_PAD)
            s = _dot_nt(k_ref[pl.ds(koff, tk), cols], q_ref[pl.ds(qoff, tq), cols])
            s_buf[par][hh] = s
            cmax.append(jnp.max(s, axis=0, keepdims=True))
        return tuple(cmax)

    def pv_issue(j, slot):
        kb = j % n_k
        out = []
        for hh in range(2):
            vt = jnp.concatenate([vt_ref[kb * sub + c, hh * HEAD_PAD:(hh + 1) * HEAD_PAD, :]
                                  for c in range(sub)], axis=1)
            out.append(_dot(vt, p_buf[slot][hh]))
        return out

    def pv_accumulate(r, alpha):
        for hh in range(2):
            acc_scr[hh] = alpha[hh] * acc_scr[hh] + r[hh]

    def softmax(j, par, slot, m, cmax):
        first = (j % n_k) == 0
        m_out, alpha = [], []
        for hh in range(2):
            m_old = jnp.where(first, -jnp.inf, m[hh])
            m_new = jnp.maximum(m_old, cmax[hh])
            p_buf[slot][hh] = jnp.exp2(s_buf[par][hh] - m_new).astype(BF16)
            alpha.append(jnp.exp2(m_old - m_new))
            m_out.append(m_new)
        return tuple(m_out), tuple(alpha)

    def finalize(j):
        qoff = _aligned((j // n_k) * tq, tq)
        o_t = jnp.concatenate([acc_scr[hh, :MLA_V] / acc_scr[hh, MLA_V:MLA_V + 1] for hh in range(2)],
                              axis=0)
        o_ref[pl.ds(qoff, tq), :] = o_t.T.astype(BF16)

    def pv(j, alpha, r=None):
        pv_accumulate(pv_issue(j, j % n_slots) if r is None else r, alpha)
        if isinstance(j, int) and j % n_k == n_k - 1:
            finalize(j)

    def body(k, kmod, m, alpha_prev, alpha_cur, cmax):
        par = kmod % 2
        r = pv_issue(k - 1, (kmod - 1) % n_slots)
        cmax_next = scores(k + 2, par)
        m, alpha_next = softmax(k + 1, 1 - par, (kmod + 1) % n_slots, m, cmax)
        pv(k - 1, alpha_prev, r)
        return m, alpha_cur, alpha_next, cmax_next

    def loop_body(kk, carry):
        k0 = n_slots * kk + 1
        for o in range(n_slots):
            carry = body(k0 + o, (1 + o) % n_slots, *carry)
        last = k0 + n_slots - 2
        pl.when(last % n_k == n_k - 1)(functools.partial(finalize, last))
        return carry

    acc_scr[...] = jnp.zeros_like(acc_scr)
    m = tuple(jnp.full((1, tq), -jnp.inf, F32) for _ in range(2))
    cmax0 = scores(0, 0)
    cmax1 = scores(1, 1)
    m, alpha0 = softmax(0, 0, 0, m, cmax0)
    cmax2 = scores(2, 0)
    m, alpha1 = softmax(1, 1, 1, m, cmax1)
    n_iter = (n_total - 3) // n_slots
    carry = lax.fori_loop(0, n_iter, loop_body, (m, alpha0, alpha1, cmax2))
    k = n_slots * n_iter + 1
    while k <= n_total - 3:
        carry = body(k, k % n_slots, *carry)
        k += 1
    m, alpha_prev, alpha_cur, cmax = carry
    r = pv_issue(k - 1, (k - 1) % n_slots)
    m, alpha_last = softmax(k + 1, (k + 1) % 2, (k + 1) % n_slots, m, cmax)
    pv(k - 1, alpha_prev, r)
    pv(n_total - 2, alpha_cur)
    pv(n_total - 1, alpha_last)


def _scan_lanes(x, op, fill, reverse):
    n = x.shape[1]
    lane = lax.broadcasted_iota(jnp.int32, x.shape, 1)
    shift = 1
    while shift < n:
        if reverse:
            moved, valid = pltpu.roll(x, n - shift, 1), lane < n - shift
        else:
            moved, valid = pltpu.roll(x, shift, 1), lane >= shift
        x = op(x, jnp.where(valid, moved, fill))
        shift *= 2
    return x


class _Chain:
    pass


def _mlstm_kernel(q_ref, k_ref, vt_ref, mot_ref, gate_ref, og_ref, o_ref,
                  hacc, st_scr, b_scr, a_scr, cm_scr, bl_scr, am_scr, *, L, nc):
    dh = MLSTM_DH
    aug = st_scr.shape[1]
    s_idx = lax.broadcasted_iota(jnp.int32, (L, L), 0)
    t_idx = lax.broadcasted_iota(jnp.int32, (L, L), 1)
    eye = s_idx == t_idx
    visible = (s_idx <= t_idx, s_idx >= t_idx)
    ones_blk = (lax.broadcasted_iota(jnp.int32, (aug - dh, L), 0) == 0).astype(BF16)

    for d in range(2):
        logi = gate_ref[d, 0, 0]
        logf = gate_ref[d, 1, 0]
        b = _scan_lanes(logf, jnp.add, 0.0, reverse=d == 1)
        a = logi - b
        b_scr[d] = b
        a_scr[d] = a
        cm_scr[d] = _scan_lanes(a, jnp.maximum, -jnp.inf, reverse=d == 1)
        bl_scr[d] = jnp.sum(logf, axis=1, keepdims=True)
        am_scr[d] = jnp.max(a, axis=1, keepdims=True)
    st_scr[...] = jnp.zeros_like(st_scr)

    def prepare(d, c, m):
        x = _Chain()
        off = pl.multiple_of(c * L, L)
        x.q = q_ref[pl.ds(off, L), :]
        x.k = k_ref[pl.ds(off, L), :]
        x.vaug = jnp.concatenate([vt_ref[c], ones_blk], axis=0)
        x.b_row = b_scr[d, pl.ds(c, 1), :]
        x.a_row = a_scr[d, pl.ds(c, 1), :]
        b_last = bl_scr[d, pl.ds(c, 1), :]
        x.g_row = jnp.maximum(cm_scr[d, pl.ds(c, 1), :], m)
        x.iw = jnp.exp(m - x.g_row)
        x.m_new = jnp.maximum(b_last + m, b_last + am_scr[d, pl.ds(c, 1), :])
        x.decay = jnp.exp(b_last + m - x.m_new)
        x.w_row = jnp.exp(b_last + x.a_row - x.m_new)
        return x

    def intra(d, x):
        a_col = jnp.sum(jnp.where(eye, x.a_row, 0.0), axis=1, keepdims=True)
        e_t = jnp.exp(jnp.where(visible[d], a_col - x.g_row, -jnp.inf))
        x.r = _dot(x.vaug, (x.st * e_t).astype(BF16))

    def finish(x):
        num = x.iw * x.inter[:dh] + x.r[:dh]
        den = x.iw * x.inter[dh:dh + 1] + x.r[dh:dh + 1]
        floor = jnp.exp(-(x.b_row + x.g_row))
        return num * (1.0 / jnp.maximum(jnp.abs(den), floor))

    def run(j0, m_f, m_b):
        chains = []
        for o in range(MLSTM_POSITIONS_PER_BODY):
            j = j0 + o
            xf = prepare(0, j, m_f)
            xb = prepare(1, nc - 1 - j, m_b)
            m_f, m_b = xf.m_new, xb.m_new
            chains += [(0, j, xf), (1, nc - 1 - j, xb)]
        for d, _, x in chains:
            x.st = _dot_nt(x.k, x.q)
            x.upd = _dot((x.vaug.astype(F32) * x.w_row).astype(BF16), x.k)
        for d, _, x in chains:
            x.inter = _dot_nt(st_scr[d].astype(BF16), x.q)
            st_scr[d] = x.decay * st_scr[d] + x.upd
        for d, _, x in chains:
            intra(d, x)
        return [(c, finish(x)) for _, c, x in chains], m_f, m_b

    def first_touch(jj, carry):
        outs, m_f, m_b = run(jj * MLSTM_POSITIONS_PER_BODY, *carry)
        for c, h in outs:
            hacc[c] = h
        return m_f, m_b

    def second_touch(jj, carry):
        outs, m_f, m_b = run(jj * MLSTM_POSITIONS_PER_BODY, *carry)
        for c, h in outs:
            tot = hacc[c] + h
            y = tot * lax.rsqrt(jnp.mean(tot * tot, axis=0, keepdims=True) + NORM_EPS) * og_ref[...]
            o_ref[c] = (y * mot_ref[c].astype(F32)).astype(BF16)
        return m_f, m_b

    n_iter = nc // MLSTM_POSITIONS_PER_BODY
    m0 = jnp.zeros((1, 1), F32)
    carry = lax.fori_loop(0, n_iter // 2, first_touch, (m0, m0))
    lax.fori_loop(n_iter // 2, n_iter, second_touch, carry)


def _merge_mlp_kernel(x_ref, ya_ref, ymt_ref, g1_ref, wgab_ref, wbm_ref, wbl_ref, wout_ref,
                      g2_ref, wup_ref, wdn_ref, gf_ref, o_ref, *, d, final_norm):
    x = x_ref[...]
    hn = _rms(x, g1_ref[...]).astype(BF16)
    gates = jax.nn.sigmoid(_dot(hn, wgab_ref[...]))
    merged = (gates[:, :d] * _dot(ya_ref[...], wbm_ref[...])
              + gates[:, d:] * _dot_tn(ymt_ref[0], wbl_ref[...]))
    x1 = x + _dot(merged.astype(BF16), wout_ref[...])
    u = _dot(_rms(x1, g2_ref[...]).astype(BF16), wup_ref[...])
    r = jnp.maximum(u, 0.0)
    x2 = x1 + _dot((r * r).astype(BF16), wdn_ref[...])
    o_ref[...] = _rms(x2, gf_ref[...]) if final_norm else x2


def _const_spec(shape):
    return pl.BlockSpec(shape, lambda *_: (0,) * len(shape))


def _resident_spec(shape):
    return pl.BlockSpec(shape, lambda *_: (0,) * len(shape), pipeline_mode=pl.Buffered(1))


def _params(n_axes):
    return pltpu.CompilerParams(dimension_semantics=("arbitrary",) * n_axes,
                                vmem_limit_bytes=VMEM_LIMIT_BYTES)


def _rope_kernel(pos_ref, freq_ref, cos_out, sin_out):
    t = pos_ref.shape[1]
    ang = freq_ref[...] * pos_ref[...]
    cos, sin = jnp.cos(ang), jnp.sin(ang)
    pad = jnp.zeros((HEAD_PAD - MLA_NOPE - MLA_ROPE, t), F32)
    cos_out[...] = jnp.concatenate([jnp.ones((MLA_NOPE, t), F32), cos, cos, pad], axis=0)
    sin_out[...] = jnp.concatenate([jnp.zeros((MLA_NOPE, t), F32), -sin, sin, pad], axis=0)


def _rope_tables(positions):
    t = positions.size
    inv_freq = ROPE_THETA ** (-jnp.arange(0, MLA_ROPE, 2, dtype=F32) / MLA_ROPE)
    table = jax.ShapeDtypeStruct((HEAD_PAD, t), F32)
    tile = min(t, ROPE_TABLE_LANES)
    return pl.pallas_call(
        _rope_kernel,
        grid=(t // tile,),
        in_specs=[pl.BlockSpec((1, tile), lambda i: (0, i)), _const_spec((MLA_ROPE // 2, 1))],
        out_specs=[pl.BlockSpec((HEAD_PAD, tile), lambda i: (0, i))] * 2,
        out_shape=[table, table], compiler_params=_params(1), name="rope_table",
    )(positions.astype(F32).reshape(1, t), inv_freq.reshape(-1, 1))


def _pad_heads(w, width_in, offset_out):
    k = w.shape[0]
    w = w.reshape(k, MLA_HEADS, width_in)
    w = jnp.pad(w, ((0, 0), (0, 0), (offset_out, HEAD_PAD - width_in - offset_out)))
    return w.reshape(k, MLA_HEADS * HEAD_PAD)


def kernel(x, positions, norm_mix_g, w_in, mla_q_norm_g, mla_w_uq, mla_kv_norm_g, mla_w_ukv, mlstm_conv_w, mlstm_conv_b, mlstm_igate_b, mlstm_fgate_b, mlstm_out_norm_g, w_branch_mla, w_branch_mlstm, w_out, norm_mlp_g, w_mlp_up, w_mlp_down, norm_final_g):
    B, S, D = x.shape
    T = B * S
    depth = w_in.shape[0]
    q_lora = mla_q_norm_g.shape[1]
    kv_lora = mla_kv_norm_g.shape[1]
    mw = MLSTM_HEADS * MLSTM_DH
    n_gate = 4 * MLSTM_HEADS
    L = MLSTM_CHUNK
    nc = S // L
    tm = tm4 = ROW_TILE
    tq, tk = ATTN_Q_TILE, ATTN_KEY_STEP
    assert S % (2 * MLSTM_POSITIONS_PER_BODY * L) == 0 and MLA_HEADS % 2 == 0
    assert S % tk == 0 and S % tq == 0 and tk % tm == 0 and tm % HALO == 0

    cos_t, sin_t = _rope_tables(positions)
    xf = x.reshape(T, D)

    for l in range(depth):
        offs, o = [], 0
        for w in (q_lora, kv_lora, MLA_ROPE, mw, mw, mw, mw, n_gate, D, D):
            offs.append(o)
            o += w
        wl = w_in[l]
        w_cq = wl[:, offs[0]:offs[0] + q_lora]
        w_ckv = wl[:, offs[1]:offs[1] + kv_lora]
        w_kr = wl[:, offs[2]:offs[2] + MLA_ROPE]
        pad_kr = ((0, 0), (MLA_NOPE, HEAD_PAD - MLA_NOPE - MLA_ROPE))
        w1 = wl[:, offs[3]:offs[3] + 2 * mw].astype(BF16)
        w2t = wl[:, offs[5]:offs[5] + 2 * mw].T.astype(BF16)
        w3 = jnp.concatenate([w_cq, w_ckv, jnp.pad(w_kr, pad_kr)], axis=1).astype(BF16)
        w2t = jnp.concatenate([w2t, wl[:, offs[7]:offs[7] + n_gate].T.astype(BF16)], axis=0)
        wgab = wl[:, offs[8]:offs[8] + 2 * D].astype(BF16)
        gate_b = jnp.stack([mlstm_igate_b[l], mlstm_fgate_b[l]], axis=1).reshape(n_gate, 1)

        wqa = _pad_heads(mla_w_uq[l], MLA_NOPE + MLA_ROPE, 0).astype(BF16)
        wkv = mla_w_ukv[l].reshape(kv_lora, MLA_HEADS, MLA_NOPE + MLA_V)
        wuk = _pad_heads(wkv[:, :, :MLA_NOPE].reshape(kv_lora, -1), MLA_NOPE, 0).astype(BF16)
        wuvt = wkv[:, :, MLA_NOPE:].reshape(kv_lora, MLA_HEADS * MLA_V).T.astype(BF16)

        cw = jnp.pad(mlstm_conv_w[l].reshape(CONV_WIDTH, 2 * mw), ((0, F32_SUBLANES - CONV_WIDTH), (0, 0)))
        cb = mlstm_conv_b[l].reshape(1, 2 * mw)

        tiles_per_seq = S // tm
        hb = tm // HALO
        n_halo_blocks = T // HALO
        row = lambda i: (i, 0)
        col = lambda i: (0, i)
        chunk = lambda i: (i, 0, 0)
        kern = functools.partial(
            _inproj_kernel, tm=tm, tiles_per_seq=tiles_per_seq, q_lora=q_lora, kv_lora=kv_lora, mw=mw,
            q_scale=(MLA_NOPE + MLA_ROPE) ** -0.5 * math.log2(math.e), k_scale=MLSTM_DH ** -0.5)
        outs = pl.pallas_call(
            kern,
            grid=(T // tm,),
            in_specs=[
                pl.BlockSpec((tm, D), row),
                pl.BlockSpec((HALO, D), lambda i: (jnp.maximum(i * hb - 1, 0), 0)),
                pl.BlockSpec((HALO, D), lambda i: (jnp.minimum((i + 1) * hb, n_halo_blocks - 1), 0)),
                pl.BlockSpec((HEAD_PAD, tm), col),
                pl.BlockSpec((HEAD_PAD, tm), col),
                _const_spec((1, D)),
                _const_spec(w1.shape), _const_spec(w2t.shape), _const_spec(w3.shape),
                _const_spec(gate_b.shape),
                _const_spec(cw.shape), _const_spec(cb.shape),
                _const_spec((1, q_lora)), _const_spec(wqa.shape),
                _const_spec((1, kv_lora)), _const_spec(wuk.shape), _const_spec(wuvt.shape),
            ],
            out_specs=[
                pl.BlockSpec((tm, MLA_HEADS * HEAD_PAD), row),
                pl.BlockSpec((tm, MLA_HEADS * HEAD_PAD), row),
                pl.BlockSpec((1, MLA_HEADS * HEAD_PAD, tm), chunk),
                pl.BlockSpec((tm, mw), row), pl.BlockSpec((tm, mw), row),
                pl.BlockSpec((1, mw, tm), chunk),
                pl.BlockSpec((1, mw, tm), chunk),
                pl.BlockSpec((n_gate, tm), col),
            ],
            out_shape=[
                jax.ShapeDtypeStruct((T, MLA_HEADS * HEAD_PAD), BF16),
                jax.ShapeDtypeStruct((T, MLA_HEADS * HEAD_PAD), BF16),
                jax.ShapeDtypeStruct((T // tm, MLA_HEADS * HEAD_PAD, tm), BF16),
                jax.ShapeDtypeStruct((T, mw), BF16), jax.ShapeDtypeStruct((T, mw), BF16),
                jax.ShapeDtypeStruct((T // tm, mw, tm), BF16),
                jax.ShapeDtypeStruct((T // tm, mw, tm), BF16),
                jax.ShapeDtypeStruct((n_gate, T), F32),
            ],
            compiler_params=_params(1),
            name="inproj",
        )(xf, xf, xf, cos_t, sin_t, norm_mix_g[l].reshape(1, D), w1, w2t, w3, gate_b, cw, cb,
          mla_q_norm_g[l].reshape(1, q_lora), wqa, mla_kv_norm_g[l].reshape(1, kv_lora), wuk, wuvt)
        q_a, k_a, v_a, q_m, k_m, v_m, mo_s, gates = outs

        vchunk = tm
        y_attn = pl.pallas_call(
            functools.partial(_attn_kernel, tq=tq, tk=tk, vchunk=vchunk, n_q=S // tq, n_k=S // tk),
            grid=(B, MLA_HEADS // 2),
            in_specs=[
                pl.BlockSpec((S, 2 * HEAD_PAD), lambda b, p: (b, p)),
                pl.BlockSpec((S, 2 * HEAD_PAD), lambda b, p: (b, p)),
                pl.BlockSpec((S // vchunk, 2 * HEAD_PAD, vchunk), lambda b, p: (b, p, 0)),
            ],
            out_specs=pl.BlockSpec((S, 2 * MLA_V), lambda b, p: (b, p)),
            out_shape=jax.ShapeDtypeStruct((T, MLA_HEADS * MLA_V), BF16),
            scratch_shapes=[pltpu.VMEM((2, tk, tq), F32), pltpu.VMEM((2, tk, tq), F32),
                            pltpu.VMEM((2, tk, tq), BF16), pltpu.VMEM((2, tk, tq), BF16),
                            pltpu.VMEM((2, tk, tq), BF16), pltpu.VMEM((2, tk, tq), BF16),
                            pltpu.VMEM((2, HEAD_PAD, tq), F32)],
            compiler_params=_params(2),
            name="mla_attn",
        )(q_a, k_a, v_a)

        gates5 = gates.reshape(2, 2, MLSTM_HEADS, T // L, L)
        head_blk = lambda b, h: (b, h)
        head_blk_t = lambda b, h: (b, h, 0)
        aug = MLSTM_DH + BF16_SUBLANES
        row_scratch = pltpu.VMEM((2, nc, L), F32)
        y_mlstm_t = pl.pallas_call(
            functools.partial(_mlstm_kernel, L=L, nc=nc),
            grid=(B, MLSTM_HEADS),
            in_specs=[
                pl.BlockSpec((S, MLSTM_DH), head_blk), pl.BlockSpec((S, MLSTM_DH), head_blk),
                pl.BlockSpec((nc, MLSTM_DH, L), head_blk_t), pl.BlockSpec((nc, MLSTM_DH, L), head_blk_t),
                pl.BlockSpec((2, 2, 1, nc, L), lambda b, h: (0, 0, h, b, 0)),
                pl.BlockSpec((MLSTM_DH, 1), lambda b, h: (h, 0)),
            ],
            out_specs=pl.BlockSpec((nc, MLSTM_DH, L), head_blk_t),
            out_shape=jax.ShapeDtypeStruct((T // L, mw, L), BF16),
            scratch_shapes=[pltpu.VMEM((nc, MLSTM_DH, L), F32),
                            pltpu.VMEM((2, aug, MLSTM_DH), F32),
                            row_scratch, row_scratch, row_scratch,
                            pltpu.VMEM((2, nc, 1), F32), pltpu.VMEM((2, nc, 1), F32)],
            compiler_params=_params(2),
            name="mlstm",
        )(q_m, k_m, v_m, mo_s, gates5, mlstm_out_norm_g[l].reshape(mw, 1))

        last = l == depth - 1
        gf = norm_final_g.reshape(1, D)
        weights4 = (wgab, w_branch_mla[l].astype(BF16), w_branch_mlstm[l].astype(BF16),
                    w_out[l].astype(BF16), w_mlp_up[l].astype(BF16), w_mlp_down[l].astype(BF16))
        xf = pl.pallas_call(
            functools.partial(_merge_mlp_kernel, d=D, final_norm=last),
            grid=(T // tm4,),
            in_specs=[
                pl.BlockSpec((tm4, D), row),
                pl.BlockSpec((tm4, MLA_HEADS * MLA_V), row),
                pl.BlockSpec((1, mw, tm4), lambda i: (i, 0, 0)),
                _const_spec((1, D)),
                _resident_spec(weights4[0].shape), _resident_spec(weights4[1].shape),
                _resident_spec(weights4[2].shape), _resident_spec(weights4[3].shape),
                _const_spec((1, D)),
                _resident_spec(weights4[4].shape), _resident_spec(weights4[5].shape),
                _const_spec((1, D)),
            ],
            out_specs=pl.BlockSpec((tm4, D), row),
            out_shape=jax.ShapeDtypeStruct((T, D), F32),
            compiler_params=_params(1),
            name="merge_mlp",
        )(xf, y_attn, y_mlstm_t, norm_mix_g[l].reshape(1, D), weights4[0], weights4[1], weights4[2],
          weights4[3], norm_mlp_g[l].reshape(1, D), weights4[4], weights4[5], gf)

    return xf.reshape(B, S, D)
```

```python
import functools
import math

import jax
import jax.numpy as jnp
from jax import lax
from jax.experimental import pallas as pl
from jax.experimental.pallas import tpu as pltpu

MLA_HEADS = 8
MLA_NOPE = 64
MLA_ROPE = 32
MLA_V = 64
ROPE_THETA = 10000.0
MLSTM_HEADS = 4
MLSTM_DH = 128
CONV_WIDTH = 5
NORM_EPS = 1e-6

LANES = 128
F32_SUBLANES = 8
BF16_SUBLANES = 16
VMEM_LIMIT_BYTES = 56 * 1024 * 1024

MXU_WIDTH = 256

HEAD_PAD = LANES
V_AUG = MLA_V + BF16_SUBLANES
MLSTM_CHUNK = MXU_WIDTH
MLSTM_POSITIONS_PER_BODY = 4
ROW_TILE = MLSTM_CHUNK
ATTN_Q_TILE = MXU_WIDTH
ATTN_KEY_STEP = 4 * MXU_WIDTH
HALO = BF16_SUBLANES
ROPE_TABLE_LANES = 16 * LANES

F32 = jnp.float32
BF16 = jnp.bfloat16


def _rms(x, g):
    return x * lax.rsqrt(jnp.mean(x * x, axis=-1, keepdims=True) + NORM_EPS) * g


def _dot(a, b):
    return jnp.dot(a, b, preferred_element_type=F32)


def _dot_nt(a, b):
    return lax.dot_general(a, b, (((1,), (1,)), ((), ())), preferred_element_type=F32)


def _dot_tn(a, b):
    return lax.dot_general(a, b, (((0,), (0,)), ((), ())), preferred_element_type=F32)


def _log_sigmoid(x):
    return jnp.minimum(x, 0.0) - jnp.log1p(jnp.exp(-jnp.abs(x)))


def _inproj_kernel(xm_ref, xp_ref, xn_ref, cos_ref, sin_ref, g_ref, w1_ref, w2t_ref, w3_ref,
                   gb_ref, cw_ref, cb_ref, qg_ref, wqa_ref, kvg_ref,
                   wuk_ref, wuvt_ref,
                   q_out, k_out, va_out, qm_out, km_out, vm_out, mo_out, gate_out,
                   *, tm, tiles_per_seq, q_lora, kv_lora, mw, q_scale, k_scale):
    i = pl.program_id(0)
    pos_in_seq = i % tiles_per_seq
    xp = jnp.where(pos_in_seq == 0, 0.0, xp_ref[...])
    xn = jnp.where(pos_in_seq == tiles_per_seq - 1, 0.0, xn_ref[...])
    xe = jnp.concatenate([xp, xm_ref[...], xn], axis=0)
    he = _rms(xe, g_ref[...]).astype(BF16)
    hm = he[HALO:HALO + tm]

    pre = _dot(he, w1_ref[...])
    rows = pre.shape[0]
    conv = cb_ref[...]
    for j in range(CONV_WIDTH):
        shifted = pre if j == CONV_WIDTH // 2 else pltpu.roll(pre, (CONV_WIDTH // 2 - j) % rows, 0)
        conv = conv + cw_ref[j:j + 1, :] * shifted[HALO:HALO + tm]
    qk = conv * jax.nn.sigmoid(conv)
    qm_out[...] = qk[:, :mw].astype(BF16)
    km_out[...] = (qk[:, mw:] * k_scale).astype(BF16)

    vo_t = _dot_nt(w2t_ref[...], hm)
    vm_out[0] = vo_t[:mw].astype(BF16)
    mo_out[0] = jax.nn.sigmoid(vo_t[mw:2 * mw]).astype(BF16)

    gt = vo_t[2 * mw:] + gb_ref[...]
    row = lax.broadcasted_iota(jnp.int32, gt.shape, 0)
    is_f = (row % (2 * MLSTM_HEADS)) >= MLSTM_HEADS
    gate_out[...] = jnp.where(is_f, _log_sigmoid(gt), gt)

    c = _dot(hm, w3_ref[...])
    cqn = _rms(c[:, :q_lora], qg_ref[...]).astype(BF16)
    ckvn = _rms(c[:, q_lora:q_lora + kv_lora], kvg_ref[...]).astype(BF16)
    cos = cos_ref[...].T
    sin = sin_ref[...].T
    half = MLA_ROPE // 2
    lane = lax.broadcasted_iota(jnp.int32, sin.shape, 1)
    sin_lo = jnp.where(lane < MLA_NOPE + half, sin, 0.0)
    sin_hi = sin - sin_lo

    def rope(x):
        return (x * cos + pltpu.roll(x, HEAD_PAD - half, 1) * sin_lo + pltpu.roll(x, half, 1) * sin_hi)

    qa = _dot(cqn, wqa_ref[...])
    q = jnp.concatenate([rope(qa[:, h * HEAD_PAD:(h + 1) * HEAD_PAD]) for h in range(MLA_HEADS)], axis=1)
    q_out[...] = (q * q_scale).astype(BF16)
    kr = rope(c[:, q_lora + kv_lora:])
    k = _dot(ckvn, wuk_ref[...]) + jnp.concatenate([kr] * MLA_HEADS, axis=1)
    k_out[...] = k.astype(BF16)
    vt = _dot_nt(wuvt_ref[...], ckvn)
    ones_row = (lax.broadcasted_iota(jnp.int32, (V_AUG - MLA_V, tm), 0) == 0).astype(F32)
    pieces = []
    for h in range(MLA_HEADS):
        pieces += [vt[h * MLA_V:(h + 1) * MLA_V], ones_row]
    va_out[0] = jnp.concatenate(pieces, axis=0).astype(BF16)


def _aligned(x, m):
    return x if isinstance(x, int) else pl.multiple_of(x, m)


def _attn_kernel(q_ref, k_ref, vt_ref, o_ref, s_0, s_1, p_0, p_1, p_2, p_3, acc_scr, *,
                 tq, tk, vchunk, n_q, n_k):
    s_buf, p_buf = (s_0, s_1), (p_0, p_1, p_2, p_3)
    n_slots = len(p_buf)
    sub = tk // vchunk
    n_total = n_q * n_k
    assert n_k % n_slots == 0 and n_slots % 2 == 0 and n_total >= 6

    def scores(j, par):
        qoff = _aligned((j // n_k) * tq, tq)
        koff = _aligned((j % n_k) * tk, tk)
        cmax = []
        for hh in range(2):
            cols = slice(hh * HEAD_PAD, (hh + 1) * HEAD_PAD)
            s = _dot_nt(k_ref[pl.ds(koff, tk), cols], q_ref[pl.ds(qoff, tq), cols])
            s_buf[par][hh] = s
            cmax.append(jnp.max(s, axis=0, keepdims=True))
        return tuple(cmax)

    def pv_issue(j, slot):
        kb = j % n_k
        out = []
        for hh in range(2):
            vt = jnp.concatenate([vt_ref[kb * sub + c, hh * V_AUG:(hh + 1) * V_AUG, :]
                                  for c in range(sub)], axis=1)
            out.append(_dot(vt, p_buf[slot][hh]))
        return out

    def pv_accumulate(r, alpha):
        for hh in range(2):
            acc_scr[hh] = alpha[hh] * acc_scr[hh] + r[hh]

    def softmax(j, par, slot, m, cmax):
        first = (j % n_k) == 0
        m_out, alpha = [], []
        for hh in range(2):
            m_old = jnp.where(first, -jnp.inf, m[hh])
            m_new = jnp.maximum(m_old, cmax[hh])
            p_buf[slot][hh] = jnp.exp2(s_buf[par][hh] - m_new).astype(BF16)
            alpha.append(jnp.exp2(m_old - m_new))
            m_out.append(m_new)
        return tuple(m_out), tuple(alpha)

    def finalize(j):
        qoff = _aligned((j // n_k) * tq, tq)
        o_t = jnp.concatenate([acc_scr[hh, :MLA_V] / acc_scr[hh, MLA_V:MLA_V + 1] for hh in range(2)],
                              axis=0)
        o_ref[pl.ds(qoff, tq), :] = o_t.T.astype(BF16)

    def pv(j, alpha, r=None):
        pv_accumulate(pv_issue(j, j % n_slots) if r is None else r, alpha)
        if isinstance(j, int) and j % n_k == n_k - 1:
            finalize(j)

    def body(k, kmod, m, alpha_prev, alpha_cur, cmax):
        par = kmod % 2
        r = pv_issue(k - 1, (kmod - 1) % n_slots)
        cmax_next = scores(k + 2, par)
        m, alpha_next = softmax(k + 1, 1 - par, (kmod + 1) % n_slots, m, cmax)
        pv(k - 1, alpha_prev, r)
        return m, alpha_cur, alpha_next, cmax_next

    def loop_body(kk, carry):
        k0 = n_slots * kk + 1
        for o in range(n_slots):
            carry = body(k0 + o, (1 + o) % n_slots, *carry)
        last = k0 + n_slots - 2
        pl.when(last % n_k == n_k - 1)(functools.partial(finalize, last))
        return carry

    acc_scr[...] = jnp.zeros_like(acc_scr)
    m = tuple(jnp.full((1, tq), -jnp.inf, F32) for _ in range(2))
    cmax0 = scores(0, 0)
    cmax1 = scores(1, 1)
    m, alpha0 = softmax(0, 0, 0, m, cmax0)
    cmax2 = scores(2, 0)
    m, alpha1 = softmax(1, 1, 1, m, cmax1)
    n_iter = (n_total - 3) // n_slots
    carry = lax.fori_loop(0, n_iter, loop_body, (m, alpha0, alpha1, cmax2))
    k = n_slots * n_iter + 1
    while k <= n_total - 3:
        carry = body(k, k % n_slots, *carry)
        k += 1
    m, alpha_prev, alpha_cur, cmax = carry
    r = pv_issue(k - 1, (k - 1) % n_slots)
    m, alpha_last = softmax(k + 1, (k + 1) % 2, (k + 1) % n_slots, m, cmax)
    pv(k - 1, alpha_prev, r)
    pv(n_total - 2, alpha_cur)
    pv(n_total - 1, alpha_last)


def _scan_lanes(x, op, fill, reverse):
    n = x.shape[1]
    lane = lax.broadcasted_iota(jnp.int32, x.shape, 1)
    shift = 1
    while shift < n:
        if reverse:
            moved, valid = pltpu.roll(x, n - shift, 1), lane < n - shift
        else:
            moved, valid = pltpu.roll(x, shift, 1), lane >= shift
        x = op(x, jnp.where(valid, moved, fill))
        shift *= 2
    return x


class _Chain:
    pass


def _mlstm_kernel(q_ref, k_ref, vt_ref, mot_ref, gate_ref, og_ref, o_ref,
                  hacc, st_scr, b_scr, a_scr, cm_scr, bl_scr, am_scr, *, L, nc):
    dh = MLSTM_DH
    aug = st_scr.shape[1]
    s_idx = lax.broadcasted_iota(jnp.int32, (L, L), 0)
    t_idx = lax.broadcasted_iota(jnp.int32, (L, L), 1)
    eye = s_idx == t_idx
    visible = (s_idx <= t_idx, s_idx >= t_idx)
    ones_blk = (lax.broadcasted_iota(jnp.int32, (aug - dh, L), 0) == 0).astype(BF16)

    for d in range(2):
        logi = gate_ref[d, 0, 0]
        logf = gate_ref[d, 1, 0]
        b = _scan_lanes(logf, jnp.add, 0.0, reverse=d == 1)
        a = logi - b
        b_scr[d] = b
        a_scr[d] = a
        cm_scr[d] = _scan_lanes(a, jnp.maximum, -jnp.inf, reverse=d == 1)
        bl_scr[d] = jnp.sum(logf, axis=1, keepdims=True)
        am_scr[d] = jnp.max(a, axis=1, keepdims=True)
    st_scr[...] = jnp.zeros_like(st_scr)

    def prepare(d, c, m):
        x = _Chain()
        off = pl.multiple_of(c * L, L)
        x.q = q_ref[pl.ds(off, L), :]
        x.k = k_ref[pl.ds(off, L), :]
        x.vaug = jnp.concatenate([vt_ref[c], ones_blk], axis=0)
        x.b_row = b_scr[d, pl.ds(c, 1), :]
        x.a_row = a_scr[d, pl.ds(c, 1), :]
        b_last = bl_scr[d, pl.ds(c, 1), :]
        x.g_row = jnp.maximum(cm_scr[d, pl.ds(c, 1), :], m)
        x.iw = jnp.exp(m - x.g_row)
        x.m_new = jnp.maximum(b_last + m, b_last + am_scr[d, pl.ds(c, 1), :])
        x.decay = jnp.exp(b_last + m - x.m_new)
        x.w_row = jnp.exp(b_last + x.a_row - x.m_new)
        return x

    def intra(d, x):
        a_col = jnp.sum(jnp.where(eye, x.a_row, 0.0), axis=1, keepdims=True)
        e_t = jnp.exp(jnp.where(visible[d], a_col - x.g_row, -jnp.inf))
        x.r = _dot(x.vaug, (x.st * e_t).astype(BF16))

    def finish(x):
        num = x.iw * x.inter[:dh] + x.r[:dh]
        den = x.iw * x.inter[dh:dh + 1] + x.r[dh:dh + 1]
        floor = jnp.exp(-(x.b_row + x.g_row))
        return num * (1.0 / jnp.maximum(jnp.abs(den), floor))

    def run(j0, m_f, m_b):
        chains = []
        for o in range(MLSTM_POSITIONS_PER_BODY):
            j = j0 + o
            xf = prepare(0, j, m_f)
            xb = prepare(1, nc - 1 - j, m_b)
            m_f, m_b = xf.m_new, xb.m_new
            chains += [(0, j, xf), (1, nc - 1 - j, xb)]
        for d, _, x in chains:
            x.st = _dot_nt(x.k, x.q)
            x.upd = _dot((x.vaug.astype(F32) * x.w_row).astype(BF16), x.k)
        for d, _, x in chains:
            x.inter = _dot_nt(st_scr[d].astype(BF16), x.q)
            st_scr[d] = x.decay * st_scr[d] + x.upd
        for d, _, x in chains:
            intra(d, x)
        return [(c, finish(x)) for _, c, x in chains], m_f, m_b

    def first_touch(jj, carry):
        outs, m_f, m_b = run(jj * MLSTM_POSITIONS_PER_BODY, *carry)
        for c, h in outs:
            hacc[c] = h
        return m_f, m_b

    def second_touch(jj, carry):
        outs, m_f, m_b = run(jj * MLSTM_POSITIONS_PER_BODY, *carry)
        for c, h in outs:
            tot = hacc[c] + h
            y = tot * lax.rsqrt(jnp.mean(tot * tot, axis=0, keepdims=True) + NORM_EPS) * og_ref[...]
            o_ref[c] = (y * mot_ref[c].astype(F32)).astype(BF16)
        return m_f, m_b

    n_iter = nc // MLSTM_POSITIONS_PER_BODY
    m0 = jnp.zeros((1, 1), F32)
    carry = lax.fori_loop(0, n_iter // 2, first_touch, (m0, m0))
    lax.fori_loop(n_iter // 2, n_iter, second_touch, carry)


def _merge_mlp_kernel(x_ref, ya_ref, ymt_ref, g1_ref, wgab_ref, wbm_ref, wbl_ref, wout_ref,
                      g2_ref, wup_ref, wdn_ref, gf_ref, o_ref, *, d, final_norm):
    x = x_ref[...]
    hn = _rms(x, g1_ref[...]).astype(BF16)
    gates = jax.nn.sigmoid(_dot(hn, wgab_ref[...]))
    merged = (gates[:, :d] * _dot(ya_ref[...], wbm_ref[...])
              + gates[:, d:] * _dot_tn(ymt_ref[0], wbl_ref[...]))
    x1 = x + _dot(merged.astype(BF16), wout_ref[...])
    u = _dot(_rms(x1, g2_ref[...]).astype(BF16), wup_ref[...])
    r = jnp.maximum(u, 0.0)
    x2 = x1 + _dot((r * r).astype(BF16), wdn_ref[...])
    o_ref[...] = _rms(x2, gf_ref[...]) if final_norm else x2


def _const_spec(shape):
    return pl.BlockSpec(shape, lambda *_: (0,) * len(shape))


def _resident_spec(shape):
    return pl.BlockSpec(shape, lambda *_: (0,) * len(shape), pipeline_mode=pl.Buffered(1))


def _params(n_axes):
    return pltpu.CompilerParams(dimension_semantics=("arbitrary",) * n_axes,
                                vmem_limit_bytes=VMEM_LIMIT_BYTES)


def _rope_kernel(pos_ref, freq_ref, cos_out, sin_out):
    t = pos_ref.shape[1]
    ang = freq_ref[...] * pos_ref[...]
    cos, sin = jnp.cos(ang), jnp.sin(ang)
    pad = jnp.zeros((HEAD_PAD - MLA_NOPE - MLA_ROPE, t), F32)
    cos_out[...] = jnp.concatenate([jnp.ones((MLA_NOPE, t), F32), cos, cos, pad], axis=0)
    sin_out[...] = jnp.concatenate([jnp.zeros((MLA_NOPE, t), F32), -sin, sin, pad], axis=0)


def _rope_tables(positions):
    t = positions.size
    inv_freq = ROPE_THETA ** (-jnp.arange(0, MLA_ROPE, 2, dtype=F32) / MLA_ROPE)
    table = jax.ShapeDtypeStruct((HEAD_PAD, t), F32)
    tile = min(t, ROPE_TABLE_LANES)
    return pl.pallas_call(
        _rope_kernel,
        grid=(t // tile,),
        in_specs=[pl.BlockSpec((1, tile), lambda i: (0, i)), _const_spec((MLA_ROPE // 2, 1))],
        out_specs=[pl.BlockSpec((HEAD_PAD, tile), lambda i: (0, i))] * 2,
        out_shape=[table, table], compiler_params=_params(1), name="rope_table",
    )(positions.astype(F32).reshape(1, t), inv_freq.reshape(-1, 1))


def _pad_heads(w, width_in, offset_out):
    k = w.shape[0]
    w = w.reshape(k, MLA_HEADS, width_in)
    w = jnp.pad(w, ((0, 0), (0, 0), (offset_out, HEAD_PAD - width_in - offset_out)))
    return w.reshape(k, MLA_HEADS * HEAD_PAD)


def kernel(x, positions, norm_mix_g, w_in, mla_q_norm_g, mla_w_uq, mla_kv_norm_g, mla_w_ukv, mlstm_conv_w, mlstm_conv_b, mlstm_igate_b, mlstm_fgate_b, mlstm_out_norm_g, w_branch_mla, w_branch_mlstm, w_out, norm_mlp_g, w_mlp_up, w_mlp_down, norm_final_g):
    B, S, D = x.shape
    T = B * S
    depth = w_in.shape[0]
    q_lora = mla_q_norm_g.shape[1]
    kv_lora = mla_kv_norm_g.shape[1]
    mw = MLSTM_HEADS * MLSTM_DH
    n_gate = 4 * MLSTM_HEADS
    L = MLSTM_CHUNK
    nc = S // L
    tm = tm4 = ROW_TILE
    tq, tk = ATTN_Q_TILE, ATTN_KEY_STEP
    assert S % (2 * MLSTM_POSITIONS_PER_BODY * L) == 0 and MLA_HEADS % 2 == 0
    assert S % tk == 0 and S % tq == 0 and tk % tm == 0 and tm % HALO == 0

    cos_t, sin_t = _rope_tables(positions)
    xf = x.reshape(T, D)

    for l in range(depth):
        offs, o = [], 0
        for w in (q_lora, kv_lora, MLA_ROPE, mw, mw, mw, mw, n_gate, D, D):
            offs.append(o)
            o += w
        wl = w_in[l]
        w_cq = wl[:, offs[0]:offs[0] + q_lora]
        w_ckv = wl[:, offs[1]:offs[1] + kv_lora]
        w_kr = wl[:, offs[2]:offs[2] + MLA_ROPE]
        pad_kr = ((0, 0), (MLA_NOPE, HEAD_PAD - MLA_NOPE - MLA_ROPE))
        w1 = wl[:, offs[3]:offs[3] + 2 * mw].astype(BF16)
        w2t = wl[:, offs[5]:offs[5] + 2 * mw].T.astype(BF16)
        w3 = jnp.concatenate([w_cq, w_ckv, jnp.pad(w_kr, pad_kr)], axis=1).astype(BF16)
        w2t = jnp.concatenate([w2t, wl[:, offs[7]:offs[7] + n_gate].T.astype(BF16)], axis=0)
        wgab = wl[:, offs[8]:offs[8] + 2 * D].astype(BF16)
        gate_b = jnp.stack([mlstm_igate_b[l], mlstm_fgate_b[l]], axis=1).reshape(n_gate, 1)

        wqa = _pad_heads(mla_w_uq[l], MLA_NOPE + MLA_ROPE, 0).astype(BF16)
        wkv = mla_w_ukv[l].reshape(kv_lora, MLA_HEADS, MLA_NOPE + MLA_V)
        wuk = _pad_heads(wkv[:, :, :MLA_NOPE].reshape(kv_lora, -1), MLA_NOPE, 0).astype(BF16)
        wuvt = wkv[:, :, MLA_NOPE:].reshape(kv_lora, MLA_HEADS * MLA_V).T.astype(BF16)

        cw = jnp.pad(mlstm_conv_w[l].reshape(CONV_WIDTH, 2 * mw), ((0, F32_SUBLANES - CONV_WIDTH), (0, 0)))
        cb = mlstm_conv_b[l].reshape(1, 2 * mw)

        tiles_per_seq = S // tm
        hb = tm // HALO
        n_halo_blocks = T // HALO
        row = lambda i: (i, 0)
        col = lambda i: (0, i)
        chunk = lambda i: (i, 0, 0)
        kern = functools.partial(
            _inproj_kernel, tm=tm, tiles_per_seq=tiles_per_seq, q_lora=q_lora, kv_lora=kv_lora, mw=mw,
            q_scale=(MLA_NOPE + MLA_ROPE) ** -0.5 * math.log2(math.e), k_scale=MLSTM_DH ** -0.5)
        outs = pl.pallas_call(
            kern,
            grid=(T // tm,),
            in_specs=[
                pl.BlockSpec((tm, D), row),
                pl.BlockSpec((HALO, D), lambda i: (jnp.maximum(i * hb - 1, 0), 0)),
                pl.BlockSpec((HALO, D), lambda i: (jnp.minimum((i + 1) * hb, n_halo_blocks - 1), 0)),
                pl.BlockSpec((HEAD_PAD, tm), col),
                pl.BlockSpec((HEAD_PAD, tm), col),
                _const_spec((1, D)),
                _const_spec(w1.shape), _const_spec(w2t.shape), _const_spec(w3.shape),
                _const_spec(gate_b.shape),
                _const_spec(cw.shape), _const_spec(cb.shape),
                _const_spec((1, q_lora)), _const_spec(wqa.shape),
                _const_spec((1, kv_lora)), _const_spec(wuk.shape), _const_spec(wuvt.shape),
            ],
            out_specs=[
                pl.BlockSpec((tm, MLA_HEADS * HEAD_PAD), row),
                pl.BlockSpec((tm, MLA_HEADS * HEAD_PAD), row),
                pl.BlockSpec((1, MLA_HEADS * V_AUG, tm), chunk),
                pl.BlockSpec((tm, mw), row), pl.BlockSpec((tm, mw), row),
                pl.BlockSpec((1, mw, tm), chunk),
                pl.BlockSpec((1, mw, tm), chunk),
                pl.BlockSpec((n_gate, tm), col),
            ],
            out_shape=[
                jax.ShapeDtypeStruct((T, MLA_HEADS * HEAD_PAD), BF16),
                jax.ShapeDtypeStruct((T, MLA_HEADS * HEAD_PAD), BF16),
                jax.ShapeDtypeStruct((T // tm, MLA_HEADS * V_AUG, tm), BF16),
                jax.ShapeDtypeStruct((T, mw), BF16), jax.ShapeDtypeStruct((T, mw), BF16),
                jax.ShapeDtypeStruct((T // tm, mw, tm), BF16),
                jax.ShapeDtypeStruct((T // tm, mw, tm), BF16),
                jax.ShapeDtypeStruct((n_gate, T), F32),
            ],
            compiler_params=_params(1),
            name="inproj",
        )(xf, xf, xf, cos_t, sin_t, norm_mix_g[l].reshape(1, D), w1, w2t, w3, gate_b, cw, cb,
          mla_q_norm_g[l].reshape(1, q_lora), wqa, mla_kv_norm_g[l].reshape(1, kv_lora), wuk, wuvt)
        q_a, k_a, v_a, q_m, k_m, v_m, mo_s, gates = outs

        vchunk = tm
        y_attn = pl.pallas_call(
            functools.partial(_attn_kernel, tq=tq, tk=tk, vchunk=vchunk, n_q=S // tq, n_k=S // tk),
            grid=(B, MLA_HEADS // 2),
            in_specs=[
                pl.BlockSpec((S, 2 * HEAD_PAD), lambda b, p: (b, p)),
                pl.BlockSpec((S, 2 * HEAD_PAD), lambda b, p: (b, p)),
                pl.BlockSpec((S // vchunk, 2 * V_AUG, vchunk), lambda b, p: (b, p, 0)),
            ],
            out_specs=pl.BlockSpec((S, 2 * MLA_V), lambda b, p: (b, p)),
            out_shape=jax.ShapeDtypeStruct((T, MLA_HEADS * MLA_V), BF16),
            scratch_shapes=[pltpu.VMEM((2, tk, tq), F32), pltpu.VMEM((2, tk, tq), F32),
                            pltpu.VMEM((2, tk, tq), BF16), pltpu.VMEM((2, tk, tq), BF16),
                            pltpu.VMEM((2, tk, tq), BF16), pltpu.VMEM((2, tk, tq), BF16),
                            pltpu.VMEM((2, V_AUG, tq), F32)],
            compiler_params=_params(2),
            name="mla_attn",
        )(q_a, k_a, v_a)

        gates5 = gates.reshape(2, 2, MLSTM_HEADS, T // L, L)
        head_blk = lambda b, h: (b, h)
        head_blk_t = lambda b, h: (b, h, 0)
        aug = MLSTM_DH + BF16_SUBLANES
        row_scratch = pltpu.VMEM((2, nc, L), F32)
        y_mlstm_t = pl.pallas_call(
            functools.partial(_mlstm_kernel, L=L, nc=nc),
            grid=(B, MLSTM_HEADS),
            in_specs=[
                pl.BlockSpec((S, MLSTM_DH), head_blk), pl.BlockSpec((S, MLSTM_DH), head_blk),
                pl.BlockSpec((nc, MLSTM_DH, L), head_blk_t), pl.BlockSpec((nc, MLSTM_DH, L), head_blk_t),
                pl.BlockSpec((2, 2, 1, nc, L), lambda b, h: (0, 0, h, b, 0)),
                pl.BlockSpec((MLSTM_DH, 1), lambda b, h: (h, 0)),
            ],
            out_specs=pl.BlockSpec((nc, MLSTM_DH, L), head_blk_t),
            out_shape=jax.ShapeDtypeStruct((T // L, mw, L), BF16),
            scratch_shapes=[pltpu.VMEM((nc, MLSTM_DH, L), F32),
                            pltpu.VMEM((2, aug, MLSTM_DH), F32),
                            row_scratch, row_scratch, row_scratch,
                            pltpu.VMEM((2, nc, 1), F32), pltpu.VMEM((2, nc, 1), F32)],
            compiler_params=_params(2),
            name="mlstm",
        )(q_m, k_m, v_m, mo_s, gates5, mlstm_out_norm_g[l].reshape(mw, 1))

        last = l == depth - 1
        gf = norm_final_g.reshape(1, D)
        weights4 = (wgab, w_branch_mla[l].astype(BF16), w_branch_mlstm[l].astype(BF16),
                    w_out[l].astype(BF16), w_mlp_up[l].astype(BF16), w_mlp_down[l].astype(BF16))
        xf = pl.pallas_call(
            functools.partial(_merge_mlp_kernel, d=D, final_norm=last),
            grid=(T // tm4,),
            in_specs=[
                pl.BlockSpec((tm4, D), row),
                pl.BlockSpec((tm4, MLA_HEADS * MLA_V), row),
                pl.BlockSpec((1, mw, tm4), lambda i: (i, 0, 0)),
                _const_spec((1, D)),
                _resident_spec(weights4[0].shape), _resident_spec(weights4[1].shape),
                _resident_spec(weights4[2].shape), _resident_spec(weights4[3].shape),
                _const_spec((1, D)),
                _resident_spec(weights4[4].shape), _resident_spec(weights4[5].shape),
                _const_spec((1, D)),
            ],
            out_specs=pl.BlockSpec((tm4, D), row),
            out_shape=jax.ShapeDtypeStruct((T, D), F32),
            compiler_params=_params(1),
            name="merge_mlp",
        )(xf, y_attn, y_mlstm_t, norm_mix_g[l].reshape(1, D), weights4[0], weights4[1], weights4[2],
          weights4[3], norm_mlp_g[l].reshape(1, D), weights4[4], weights4[5], gf)

    return xf.reshape(B, S, D)
```

```python
import functools
import math

import jax
import jax.numpy as jnp
from jax import lax
from jax.experimental import pallas as pl
from jax.experimental.pallas import tpu as pltpu

MLA_HEADS = 8
MLA_NOPE = 64
MLA_ROPE = 32
MLA_V = 64
ROPE_THETA = 10000.0
MLSTM_HEADS = 4
MLSTM_DH = 128
CONV_WIDTH = 5
NORM_EPS = 1e-6

LANES = 128
F32_SUBLANES = 8
BF16_SUBLANES = 16
VMEM_LIMIT_BYTES = 56 * 1024 * 1024

MXU_WIDTH = 256

HEAD_PAD = LANES
MLSTM_CHUNK = MXU_WIDTH
MLSTM_POSITIONS_PER_BODY = 4
ROW_TILE = MLSTM_CHUNK
ATTN_Q_TILE = MXU_WIDTH
ATTN_KEY_STEP = 4 * MXU_WIDTH
HALO = BF16_SUBLANES
ROPE_TABLE_LANES = 16 * LANES

F32 = jnp.float32
BF16 = jnp.bfloat16


def _rms(x, g):
    return x * lax.rsqrt(jnp.mean(x * x, axis=-1, keepdims=True) + NORM_EPS) * g


def _dot(a, b):
    return jnp.dot(a, b, preferred_element_type=F32)


def _dot_nt(a, b):
    return lax.dot_general(a, b, (((1,), (1,)), ((), ())), preferred_element_type=F32)


def _dot_tn(a, b):
    return lax.dot_general(a, b, (((0,), (0,)), ((), ())), preferred_element_type=F32)


def _log_sigmoid(x):
    return jnp.minimum(x, 0.0) - jnp.log1p(jnp.exp(-jnp.abs(x)))


def _inproj_kernel(xm_ref, xp_ref, xn_ref, cos_ref, sin_ref, g_ref, w1_ref, w2t_ref, w3_ref,
                   gb_ref, cw_ref, cb_ref, qg_ref, wqa_ref, kvg_ref,
                   wuk_ref, wuvt_ref,
                   q_out, k_out, va_out, qm_out, km_out, vm_out, mo_out, gate_out,
                   *, tm, tiles_per_seq, q_lora, kv_lora, mw, q_scale, k_scale):
    i = pl.program_id(0)
    pos_in_seq = i % tiles_per_seq
    xp = jnp.where(pos_in_seq == 0, 0.0, xp_ref[...])
    xn = jnp.where(pos_in_seq == tiles_per_seq - 1, 0.0, xn_ref[...])
    xe = jnp.concatenate([xp, xm_ref[...], xn], axis=0)
    he = _rms(xe, g_ref[...]).astype(BF16)
    hm = he[HALO:HALO + tm]

    pre = _dot(he, w1_ref[...])
    rows = pre.shape[0]
    conv = cb_ref[...]
    for j in range(CONV_WIDTH):
        shifted = pre if j == CONV_WIDTH // 2 else pltpu.roll(pre, (CONV_WIDTH // 2 - j) % rows, 0)
        conv = conv + cw_ref[j:j + 1, :] * shifted[HALO:HALO + tm]
    qk = conv * jax.nn.sigmoid(conv)
    qm_out[...] = qk[:, :mw].astype(BF16)
    km_out[...] = (qk[:, mw:] * k_scale).astype(BF16)

    vo_t = _dot_nt(w2t_ref[...], hm)
    vm_out[0] = vo_t[:mw].astype(BF16)
    mo_out[0] = jax.nn.sigmoid(vo_t[mw:2 * mw]).astype(BF16)

    gt = vo_t[2 * mw:] + gb_ref[...]
    row = lax.broadcasted_iota(jnp.int32, gt.shape, 0)
    is_f = (row % (2 * MLSTM_HEADS)) >= MLSTM_HEADS
    gate_out[...] = jnp.where(is_f, _log_sigmoid(gt), gt)

    c = _dot(hm, w3_ref[...])
    cqn = _rms(c[:, :q_lora], qg_ref[...]).astype(BF16)
    ckvn = _rms(c[:, q_lora:q_lora + kv_lora], kvg_ref[...]).astype(BF16)
    cos = cos_ref[...].T
    sin = sin_ref[...].T
    half = MLA_ROPE // 2
    lane = lax.broadcasted_iota(jnp.int32, sin.shape, 1)
    sin_lo = jnp.where(lane < MLA_NOPE + half, sin, 0.0)
    sin_hi = sin - sin_lo

    def rope(x):
        return (x * cos + pltpu.roll(x, HEAD_PAD - half, 1) * sin_lo + pltpu.roll(x, half, 1) * sin_hi)

    qa = _dot(cqn, wqa_ref[...])
    q = jnp.concatenate([rope(qa[:, h * HEAD_PAD:(h + 1) * HEAD_PAD]) for h in range(MLA_HEADS)], axis=1)
    q_out[...] = (q * q_scale).astype(BF16)
    kr = rope(c[:, q_lora + kv_lora:])
    k = _dot(ckvn, wuk_ref[...]) + jnp.concatenate([kr] * MLA_HEADS, axis=1)
    k_out[...] = k.astype(BF16)
    vt = _dot_nt(wuvt_ref[...], ckvn)
    ones = jnp.ones((HEAD_PAD - MLA_V, tm), F32)
    pieces = []
    for h in range(MLA_HEADS):
        pieces += [vt[h * MLA_V:(h + 1) * MLA_V], ones]
    va_out[0] = jnp.concatenate(pieces, axis=0).astype(BF16)


def _aligned(x, m):
    return x if isinstance(x, int) else pl.multiple_of(x, m)


def _attn_kernel(q_ref, k_ref, vt_ref, o_ref, s_0, s_1, p_0, p_1, p_2, p_3, acc_scr, *,
                 tq, tk, vchunk, n_q, n_k):
    s_buf, p_buf = (s_0, s_1), (p_0, p_1, p_2, p_3)
    n_slots = len(p_buf)
    sub = tk // vchunk
    n_total = n_q * n_k
    assert n_k % n_slots == 0 and n_slots % 2 == 0 and n_total >= 6

    def scores(j, par):
        qoff = _aligned((j // n_k) * tq, tq)
        koff = _aligned((j % n_k) * tk, tk)
        cmax = []
        for hh in range(2):
            cols = slice(hh * HEAD_PAD, (hh + 1) * HEAD_PAD)
            s = _dot_nt(k_ref[pl.ds(koff, tk), cols], q_ref[pl.ds(qoff, tq), cols])
            s_buf[par][hh] = s
            cmax.append(jnp.max(s, axis=0, keepdims=True))
        return tuple(cmax)

    def pv_issue(j, slot):
        kb = j % n_k
        out = []
        for hh in range(2):
            vt = jnp.concatenate([vt_ref[kb * sub + c, hh * HEAD_PAD:(hh + 1) * HEAD_PAD, :]
                                  for c in range(sub)], axis=1)
            out.append(_dot(vt, p_buf[slot][hh]))
        return out

    def pv_accumulate(r, alpha):
        for hh in range(2):
            acc_scr[hh] = alpha[hh] * acc_scr[hh] + r[hh]

    def softmax(j, par, slot, m, cmax):
        first = (j % n_k) == 0
        m_out, alpha = [], []
        for hh in range(2):
            m_old = jnp.where(first, -jnp.inf, m[hh])
            m_new = jnp.maximum(m_old, cmax[hh])
            p_buf[slot][hh] = jnp.exp2((s_buf[par][hh] - m_new).astype(BF16))
            alpha.append(jnp.exp2(m_old - m_new))
            m_out.append(m_new)
        return tuple(m_out), tuple(alpha)

    def finalize(j):
        qoff = _aligned((j // n_k) * tq, tq)
        o_t = jnp.concatenate([acc_scr[hh, :MLA_V] / acc_scr[hh, MLA_V:MLA_V + 1] for hh in range(2)],
                              axis=0)
        o_ref[pl.ds(qoff, tq), :] = o_t.T.astype(BF16)

    def pv(j, alpha, r=None):
        pv_accumulate(pv_issue(j, j % n_slots) if r is None else r, alpha)
        if isinstance(j, int) and j % n_k == n_k - 1:
            finalize(j)

    def body(k, kmod, m, alpha_prev, alpha_cur, cmax):
        par = kmod % 2
        r = pv_issue(k - 1, (kmod - 1) % n_slots)
        cmax_next = scores(k + 2, par)
        m, alpha_next = softmax(k + 1, 1 - par, (kmod + 1) % n_slots, m, cmax)
        pv(k - 1, alpha_prev, r)
        return m, alpha_cur, alpha_next, cmax_next

    def loop_body(kk, carry):
        k0 = n_slots * kk + 1
        for o in range(n_slots):
            carry = body(k0 + o, (1 + o) % n_slots, *carry)
        last = k0 + n_slots - 2
        pl.when(last % n_k == n_k - 1)(functools.partial(finalize, last))
        return carry

    acc_scr[...] = jnp.zeros_like(acc_scr)
    m = tuple(jnp.full((1, tq), -jnp.inf, F32) for _ in range(2))
    cmax0 = scores(0, 0)
    cmax1 = scores(1, 1)
    m, alpha0 = softmax(0, 0, 0, m, cmax0)
    cmax2 = scores(2, 0)
    m, alpha1 = softmax(1, 1, 1, m, cmax1)
    n_iter = (n_total - 3) // n_slots
    carry = lax.fori_loop(0, n_iter, loop_body, (m, alpha0, alpha1, cmax2))
    k = n_slots * n_iter + 1
    while k <= n_total - 3:
        carry = body(k, k % n_slots, *carry)
        k += 1
    m, alpha_prev, alpha_cur, cmax = carry
    r = pv_issue(k - 1, (k - 1) % n_slots)
    m, alpha_last = softmax(k + 1, (k + 1) % 2, (k + 1) % n_slots, m, cmax)
    pv(k - 1, alpha_prev, r)
    pv(n_total - 2, alpha_cur)
    pv(n_total - 1, alpha_last)


def _scan_lanes(x, op, fill, reverse):
    n = x.shape[1]
    lane = lax.broadcasted_iota(jnp.int32, x.shape, 1)
    shift = 1
    while shift < n:
        if reverse:
            moved, valid = pltpu.roll(x, n - shift, 1), lane < n - shift
        else:
            moved, valid = pltpu.roll(x, shift, 1), lane >= shift
        x = op(x, jnp.where(valid, moved, fill))
        shift *= 2
    return x


class _Chain:
    pass


def _mlstm_kernel(q_ref, k_ref, vt_ref, mot_ref, gate_ref, og_ref, o_ref,
                  hacc, st_scr, b_scr, a_scr, cm_scr, bl_scr, am_scr, *, L, nc):
    dh = MLSTM_DH
    aug = st_scr.shape[1]
    s_idx = lax.broadcasted_iota(jnp.int32, (L, L), 0)
    t_idx = lax.broadcasted_iota(jnp.int32, (L, L), 1)
    eye = s_idx == t_idx
    visible = (s_idx <= t_idx, s_idx >= t_idx)
    ones_blk = (lax.broadcasted_iota(jnp.int32, (aug - dh, L), 0) == 0).astype(BF16)

    for d in range(2):
        logi = gate_ref[d, 0, 0]
        logf = gate_ref[d, 1, 0]
        b = _scan_lanes(logf, jnp.add, 0.0, reverse=d == 1)
        a = logi - b
        b_scr[d] = b
        a_scr[d] = a
        cm_scr[d] = _scan_lanes(a, jnp.maximum, -jnp.inf, reverse=d == 1)
        bl_scr[d] = jnp.sum(logf, axis=1, keepdims=True)
        am_scr[d] = jnp.max(a, axis=1, keepdims=True)
    st_scr[...] = jnp.zeros_like(st_scr)

    def prepare(d, c, m):
        x = _Chain()
        off = pl.multiple_of(c * L, L)
        x.q = q_ref[pl.ds(off, L), :]
        x.k = k_ref[pl.ds(off, L), :]
        x.vaug = jnp.concatenate([vt_ref[c], ones_blk], axis=0)
        x.b_row = b_scr[d, pl.ds(c, 1), :]
        x.a_row = a_scr[d, pl.ds(c, 1), :]
        b_last = bl_scr[d, pl.ds(c, 1), :]
        x.g_row = jnp.maximum(cm_scr[d, pl.ds(c, 1), :], m)
        x.iw = jnp.exp(m - x.g_row)
        x.m_new = jnp.maximum(b_last + m, b_last + am_scr[d, pl.ds(c, 1), :])
        x.decay = jnp.exp(b_last + m - x.m_new)
        x.w_row = jnp.exp(b_last + x.a_row - x.m_new)
        return x

    def intra(d, x):
        a_col = jnp.sum(jnp.where(eye, x.a_row, 0.0), axis=1, keepdims=True)
        e_t = jnp.exp(jnp.where(visible[d], a_col - x.g_row, -jnp.inf))
        x.r = _dot(x.vaug, (x.st * e_t).astype(BF16))

    def finish(x):
        num = x.iw * x.inter[:dh] + x.r[:dh]
        den = x.iw * x.inter[dh:dh + 1] + x.r[dh:dh + 1]
        floor = jnp.exp(-(x.b_row + x.g_row))
        return num * (1.0 / jnp.maximum(jnp.abs(den), floor))

    def run(j0, m_f, m_b):
        chains = []
        for o in range(MLSTM_POSITIONS_PER_BODY):
            j = j0 + o
            xf = prepare(0, j, m_f)
            xb = prepare(1, nc - 1 - j, m_b)
            m_f, m_b = xf.m_new, xb.m_new
            chains += [(0, j, xf), (1, nc - 1 - j, xb)]
        for d, _, x in chains:
            x.st = _dot_nt(x.k, x.q)
            x.upd = _dot((x.vaug.astype(F32) * x.w_row).astype(BF16), x.k)
        for d, _, x in chains:
            x.inter = _dot_nt(st_scr[d].astype(BF16), x.q)
            st_scr[d] = x.decay * st_scr[d] + x.upd
        for d, _, x in chains:
            intra(d, x)
        return [(c, finish(x)) for _, c, x in chains], m_f, m_b

    def first_touch(jj, carry):
        outs, m_f, m_b = run(jj * MLSTM_POSITIONS_PER_BODY, *carry)
        for c, h in outs:
            hacc[c] = h
        return m_f, m_b

    def second_touch(jj, carry):
        outs, m_f, m_b = run(jj * MLSTM_POSITIONS_PER_BODY, *carry)
        for c, h in outs:
            tot = hacc[c] + h
            y = tot * lax.rsqrt(jnp.mean(tot * tot, axis=0, keepdims=True) + NORM_EPS) * og_ref[...]
            o_ref[c] = (y * mot_ref[c].astype(F32)).astype(BF16)
        return m_f, m_b

    n_iter = nc // MLSTM_POSITIONS_PER_BODY
    m0 = jnp.zeros((1, 1), F32)
    carry = lax.fori_loop(0, n_iter // 2, first_touch, (m0, m0))
    lax.fori_loop(n_iter // 2, n_iter, second_touch, carry)


def _merge_mlp_kernel(x_ref, ya_ref, ymt_ref, g1_ref, wgab_ref, wbm_ref, wbl_ref, wout_ref,
                      g2_ref, wup_ref, wdn_ref, gf_ref, o_ref, *, d, final_norm):
    x = x_ref[...]
    hn = _rms(x, g1_ref[...]).astype(BF16)
    gates = jax.nn.sigmoid(_dot(hn, wgab_ref[...]))
    merged = (gates[:, :d] * _dot(ya_ref[...], wbm_ref[...])
              + gates[:, d:] * _dot_tn(ymt_ref[0], wbl_ref[...]))
    x1 = x + _dot(merged.astype(BF16), wout_ref[...])
    u = _dot(_rms(x1, g2_ref[...]).astype(BF16), wup_ref[...])
    r = jnp.maximum(u, 0.0)
    x2 = x1 + _dot((r * r).astype(BF16), wdn_ref[...])
    o_ref[...] = _rms(x2, gf_ref[...]) if final_norm else x2


def _const_spec(shape):
    return pl.BlockSpec(shape, lambda *_: (0,) * len(shape))


def _resident_spec(shape):
    return pl.BlockSpec(shape, lambda *_: (0,) * len(shape), pipeline_mode=pl.Buffered(1))


def _params(n_axes):
    return pltpu.CompilerParams(dimension_semantics=("arbitrary",) * n_axes,
                                vmem_limit_bytes=VMEM_LIMIT_BYTES)


def _rope_kernel(pos_ref, freq_ref, cos_out, sin_out):
    t = pos_ref.shape[1]
    ang = freq_ref[...] * pos_ref[...]
    cos, sin = jnp.cos(ang), jnp.sin(ang)
    pad = jnp.zeros((HEAD_PAD - MLA_NOPE - MLA_ROPE, t), F32)
    cos_out[...] = jnp.concatenate([jnp.ones((MLA_NOPE, t), F32), cos, cos, pad], axis=0)
    sin_out[...] = jnp.concatenate([jnp.zeros((MLA_NOPE, t), F32), -sin, sin, pad], axis=0)


def _rope_tables(positions):
    t = positions.size
    inv_freq = ROPE_THETA ** (-jnp.arange(0, MLA_ROPE, 2, dtype=F32) / MLA_ROPE)
    table = jax.ShapeDtypeStruct((HEAD_PAD, t), F32)
    tile = min(t, ROPE_TABLE_LANES)
    return pl.pallas_call(
        _rope_kernel,
        grid=(t // tile,),
        in_specs=[pl.BlockSpec((1, tile), lambda i: (0, i)), _const_spec((MLA_ROPE // 2, 1))],
        out_specs=[pl.BlockSpec((HEAD_PAD, tile), lambda i: (0, i))] * 2,
        out_shape=[table, table], compiler_params=_params(1), name="rope_table",
    )(positions.astype(F32).reshape(1, t), inv_freq.reshape(-1, 1))


def _pad_heads(w, width_in, offset_out):
    k = w.shape[0]
    w = w.reshape(k, MLA_HEADS, width_in)
    w = jnp.pad(w, ((0, 0), (0, 0), (offset_out, HEAD_PAD - width_in - offset_out)))
    return w.reshape(k, MLA_HEADS * HEAD_PAD)


def kernel(x, positions, norm_mix_g, w_in, mla_q_norm_g, mla_w_uq, mla_kv_norm_g, mla_w_ukv, mlstm_conv_w, mlstm_conv_b, mlstm_igate_b, mlstm_fgate_b, mlstm_out_norm_g, w_branch_mla, w_branch_mlstm, w_out, norm_mlp_g, w_mlp_up, w_mlp_down, norm_final_g):
    B, S, D = x.shape
    T = B * S
    depth = w_in.shape[0]
    q_lora = mla_q_norm_g.shape[1]
    kv_lora = mla_kv_norm_g.shape[1]
    mw = MLSTM_HEADS * MLSTM_DH
    n_gate = 4 * MLSTM_HEADS
    L = MLSTM_CHUNK
    nc = S // L
    tm = tm4 = ROW_TILE
    tq, tk = ATTN_Q_TILE, ATTN_KEY_STEP
    assert S % (2 * MLSTM_POSITIONS_PER_BODY * L) == 0 and MLA_HEADS % 2 == 0
    assert S % tk == 0 and S % tq == 0 and tk % tm == 0 and tm % HALO == 0

    cos_t, sin_t = _rope_tables(positions)
    xf = x.reshape(T, D)

    for l in range(depth):
        offs, o = [], 0
        for w in (q_lora, kv_lora, MLA_ROPE, mw, mw, mw, mw, n_gate, D, D):
            offs.append(o)
            o += w
        wl = w_in[l]
        w_cq = wl[:, offs[0]:offs[0] + q_lora]
        w_ckv = wl[:, offs[1]:offs[1] + kv_lora]
        w_kr = wl[:, offs[2]:offs[2] + MLA_ROPE]
        pad_kr = ((0, 0), (MLA_NOPE, HEAD_PAD - MLA_NOPE - MLA_ROPE))
        w1 = wl[:, offs[3]:offs[3] + 2 * mw].astype(BF16)
        w2t = wl[:, offs[5]:offs[5] + 2 * mw].T.astype(BF16)
        w3 = jnp.concatenate([w_cq, w_ckv, jnp.pad(w_kr, pad_kr)], axis=1).astype(BF16)
        w2t = jnp.concatenate([w2t, wl[:, offs[7]:offs[7] + n_gate].T.astype(BF16)], axis=0)
        wgab = wl[:, offs[8]:offs[8] + 2 * D].astype(BF16)
        gate_b = jnp.stack([mlstm_igate_b[l], mlstm_fgate_b[l]], axis=1).reshape(n_gate, 1)

        wqa = _pad_heads(mla_w_uq[l], MLA_NOPE + MLA_ROPE, 0).astype(BF16)
        wkv = mla_w_ukv[l].reshape(kv_lora, MLA_HEADS, MLA_NOPE + MLA_V)
        wuk = _pad_heads(wkv[:, :, :MLA_NOPE].reshape(kv_lora, -1), MLA_NOPE, 0).astype(BF16)
        wuvt = wkv[:, :, MLA_NOPE:].reshape(kv_lora, MLA_HEADS * MLA_V).T.astype(BF16)

        cw = jnp.pad(mlstm_conv_w[l].reshape(CONV_WIDTH, 2 * mw), ((0, F32_SUBLANES - CONV_WIDTH), (0, 0)))
        cb = mlstm_conv_b[l].reshape(1, 2 * mw)

        tiles_per_seq = S // tm
        hb = tm // HALO
        n_halo_blocks = T // HALO
        row = lambda i: (i, 0)
        col = lambda i: (0, i)
        chunk = lambda i: (i, 0, 0)
        kern = functools.partial(
            _inproj_kernel, tm=tm, tiles_per_seq=tiles_per_seq, q_lora=q_lora, kv_lora=kv_lora, mw=mw,
            q_scale=(MLA_NOPE + MLA_ROPE) ** -0.5 * math.log2(math.e), k_scale=MLSTM_DH ** -0.5)
        outs = pl.pallas_call(
            kern,
            grid=(T // tm,),
            in_specs=[
                pl.BlockSpec((tm, D), row),
                pl.BlockSpec((HALO, D), lambda i: (jnp.maximum(i * hb - 1, 0), 0)),
                pl.BlockSpec((HALO, D), lambda i: (jnp.minimum((i + 1) * hb, n_halo_blocks - 1), 0)),
                pl.BlockSpec((HEAD_PAD, tm), col),
                pl.BlockSpec((HEAD_PAD, tm), col),
                _const_spec((1, D)),
                _const_spec(w1.shape), _const_spec(w2t.shape), _const_spec(w3.shape),
                _const_spec(gate_b.shape),
                _const_spec(cw.shape), _const_spec(cb.shape),
                _const_spec((1, q_lora)), _const_spec(wqa.shape),
                _const_spec((1, kv_lora)), _const_spec(wuk.shape), _const_spec(wuvt.shape),
            ],
            out_specs=[
                pl.BlockSpec((tm, MLA_HEADS * HEAD_PAD), row),
                pl.BlockSpec((tm, MLA_HEADS * HEAD_PAD), row),
                pl.BlockSpec((1, MLA_HEADS * HEAD_PAD, tm), chunk),
                pl.BlockSpec((tm, mw), row), pl.BlockSpec((tm, mw), row),
                pl.BlockSpec((1, mw, tm), chunk),
                pl.BlockSpec((1, mw, tm), chunk),
                pl.BlockSpec((n_gate, tm), col),
            ],
            out_shape=[
                jax.ShapeDtypeStruct((T, MLA_HEADS * HEAD_PAD), BF16),
                jax.ShapeDtypeStruct((T, MLA_HEADS * HEAD_PAD), BF16),
                jax.ShapeDtypeStruct((T // tm, MLA_HEADS * HEAD_PAD, tm), BF16),
                jax.ShapeDtypeStruct((T, mw), BF16), jax.ShapeDtypeStruct((T, mw), BF16),
                jax.ShapeDtypeStruct((T // tm, mw, tm), BF16),
                jax.ShapeDtypeStruct((T // tm, mw, tm), BF16),
                jax.ShapeDtypeStruct((n_gate, T), F32),
            ],
            compiler_params=_params(1),
            name="inproj",
        )(xf, xf, xf, cos_t, sin_t, norm_mix_g[l].reshape(1, D), w1, w2t, w3, gate_b, cw, cb,
          mla_q_norm_g[l].reshape(1, q_lora), wqa, mla_kv_norm_g[l].reshape(1, kv_lora), wuk, wuvt)
        q_a, k_a, v_a, q_m, k_m, v_m, mo_s, gates = outs

        vchunk = tm
        y_attn = pl.pallas_call(
            functools.partial(_attn_kernel, tq=tq, tk=tk, vchunk=vchunk, n_q=S // tq, n_k=S // tk),
            grid=(B, MLA_HEADS // 2),
            in_specs=[
                pl.BlockSpec((S, 2 * HEAD_PAD), lambda b, p: (b, p)),
                pl.BlockSpec((S, 2 * HEAD_PAD), lambda b, p: (b, p)),
                pl.BlockSpec((S // vchunk, 2 * HEAD_PAD, vchunk), lambda b, p: (b, p, 0)),
            ],
            out_specs=pl.BlockSpec((S, 2 * MLA_V), lambda b, p: (b, p)),
            out_shape=jax.ShapeDtypeStruct((T, MLA_HEADS * MLA_V), BF16),
            scratch_shapes=[pltpu.VMEM((2, tk, tq), F32), pltpu.VMEM((2, tk, tq), F32),
                            pltpu.VMEM((2, tk, tq), BF16), pltpu.VMEM((2, tk, tq), BF16),
                            pltpu.VMEM((2, tk, tq), BF16), pltpu.VMEM((2, tk, tq), BF16),
                            pltpu.VMEM((2, HEAD_PAD, tq), F32)],
            compiler_params=_params(2),
            name="mla_attn",
        )(q_a, k_a, v_a)

        gates5 = gates.reshape(2, 2, MLSTM_HEADS, T // L, L)
        head_blk = lambda b, h: (b, h)
        head_blk_t = lambda b, h: (b, h, 0)
        aug = MLSTM_DH + BF16_SUBLANES
        row_scratch = pltpu.VMEM((2, nc, L), F32)
        y_mlstm_t = pl.pallas_call(
            functools.partial(_mlstm_kernel, L=L, nc=nc),
            grid=(B, MLSTM_HEADS),
            in_specs=[
                pl.BlockSpec((S, MLSTM_DH), head_blk), pl.BlockSpec((S, MLSTM_DH), head_blk),
                pl.BlockSpec((nc, MLSTM_DH, L), head_blk_t), pl.BlockSpec((nc, MLSTM_DH, L), head_blk_t),
                pl.BlockSpec((2, 2, 1, nc, L), lambda b, h: (0, 0, h, b, 0)),
                pl.BlockSpec((MLSTM_DH, 1), lambda b, h: (h, 0)),
            ],
            out_specs=pl.BlockSpec((nc, MLSTM_DH, L), head_blk_t),
            out_shape=jax.ShapeDtypeStruct((T // L, mw, L), BF16),
            scratch_shapes=[pltpu.VMEM((nc, MLSTM_DH, L), F32),
                            pltpu.VMEM((2, aug, MLSTM_DH), F32),
                            row_scratch, row_scratch, row_scratch,
                            pltpu.VMEM((2, nc, 1), F32), pltpu.VMEM((2, nc, 1), F32)],
            compiler_params=_params(2),
            name="mlstm",
        )(q_m, k_m, v_m, mo_s, gates5, mlstm_out_norm_g[l].reshape(mw, 1))

        last = l == depth - 1
        gf = norm_final_g.reshape(1, D)
        weights4 = (wgab, w_branch_mla[l].astype(BF16), w_branch_mlstm[l].astype(BF16),
                    w_out[l].astype(BF16), w_mlp_up[l].astype(BF16), w_mlp_down[l].astype(BF16))
        xf = pl.pallas_call(
            functools.partial(_merge_mlp_kernel, d=D, final_norm=last),
            grid=(T // tm4,),
            in_specs=[
                pl.BlockSpec((tm4, D), row),
                pl.BlockSpec((tm4, MLA_HEADS * MLA_V), row),
                pl.BlockSpec((1, mw, tm4), lambda i: (i, 0, 0)),
                _const_spec((1, D)),
                _resident_spec(weights4[0].shape), _resident_spec(weights4[1].shape),
                _resident_spec(weights4[2].shape), _resident_spec(weights4[3].shape),
                _const_spec((1, D)),
                _resident_spec(weights4[4].shape), _resident_spec(weights4[5].shape),
                _const_spec((1, D)),
            ],
            out_specs=pl.BlockSpec((tm4, D), row),
            out_shape=jax.ShapeDtypeStruct((T, D), F32),
            compiler_params=_params(1),
            name="merge_mlp",
        )(xf, y_attn, y_mlstm_t, norm_mix_g[l].reshape(1, D), weights4[0], weights4[1], weights4[2],
          weights4[3], norm_mlp_g[l].reshape(1, D), weights4[4], weights4[5], gf)

    return xf.reshape(B, S, D)
```

```python
import functools
import math

import jax
import jax.numpy as jnp
from jax import lax
from jax.experimental import pallas as pl
from jax.experimental.pallas import tpu as pltpu

MLA_HEADS = 8
MLA_NOPE = 64
MLA_ROPE = 32
MLA_V = 64
ROPE_THETA = 10000.0
MLSTM_HEADS = 4
MLSTM_DH = 128
CONV_WIDTH = 5
NORM_EPS = 1e-6

LANES = 128
F32_SUBLANES = 8
BF16_SUBLANES = 16
VMEM_LIMIT_BYTES = 56 * 1024 * 1024

MXU_WIDTH = 256

HEAD_PAD = LANES
MLSTM_CHUNK = MXU_WIDTH
MLSTM_POSITIONS_PER_BODY = 4
ROW_TILE = MLSTM_CHUNK
ATTN_Q_TILE = MXU_WIDTH
ATTN_KEY_STEP = 4 * MXU_WIDTH
ATTN_BODIES_PER_ITER = 8
HALO = BF16_SUBLANES
ROPE_TABLE_LANES = 16 * LANES

F32 = jnp.float32
BF16 = jnp.bfloat16


def _rms(x, g):
    return x * lax.rsqrt(jnp.mean(x * x, axis=-1, keepdims=True) + NORM_EPS) * g


def _dot(a, b):
    return jnp.dot(a, b, preferred_element_type=F32)


def _dot_nt(a, b):
    return lax.dot_general(a, b, (((1,), (1,)), ((), ())), preferred_element_type=F32)


def _dot_tn(a, b):
    return lax.dot_general(a, b, (((0,), (0,)), ((), ())), preferred_element_type=F32)


def _log_sigmoid(x):
    return jnp.minimum(x, 0.0) - jnp.log1p(jnp.exp(-jnp.abs(x)))


def _inproj_kernel(xm_ref, xp_ref, xn_ref, cos_ref, sin_ref, g_ref, w1_ref, w2t_ref, w3_ref,
                   gb_ref, cw_ref, cb_ref, qg_ref, wqa_ref, kvg_ref,
                   wuk_ref, wuvt_ref,
                   q_out, k_out, va_out, qm_out, km_out, vm_out, mo_out, gate_out,
                   *, tm, tiles_per_seq, q_lora, kv_lora, mw, q_scale, k_scale):
    i = pl.program_id(0)
    pos_in_seq = i % tiles_per_seq
    xp = jnp.where(pos_in_seq == 0, 0.0, xp_ref[...])
    xn = jnp.where(pos_in_seq == tiles_per_seq - 1, 0.0, xn_ref[...])
    xe = jnp.concatenate([xp, xm_ref[...], xn], axis=0)
    he = _rms(xe, g_ref[...]).astype(BF16)
    hm = he[HALO:HALO + tm]

    pre = _dot(he, w1_ref[...])
    rows = pre.shape[0]
    conv = cb_ref[...]
    for j in range(CONV_WIDTH):
        shifted = pre if j == CONV_WIDTH // 2 else pltpu.roll(pre, (CONV_WIDTH // 2 - j) % rows, 0)
        conv = conv + cw_ref[j:j + 1, :] * shifted[HALO:HALO + tm]
    qk = conv * jax.nn.sigmoid(conv)
    qm_out[...] = qk[:, :mw].astype(BF16)
    km_out[...] = (qk[:, mw:] * k_scale).astype(BF16)

    vo_t = _dot_nt(w2t_ref[...], hm)
    vm_out[0] = vo_t[:mw].astype(BF16)
    mo_out[0] = jax.nn.sigmoid(vo_t[mw:2 * mw]).astype(BF16)

    gt = vo_t[2 * mw:] + gb_ref[...]
    row = lax.broadcasted_iota(jnp.int32, gt.shape, 0)
    is_f = (row % (2 * MLSTM_HEADS)) >= MLSTM_HEADS
    gate_out[...] = jnp.where(is_f, _log_sigmoid(gt), gt)

    c = _dot(hm, w3_ref[...])
    cqn = _rms(c[:, :q_lora], qg_ref[...]).astype(BF16)
    ckvn = _rms(c[:, q_lora:q_lora + kv_lora], kvg_ref[...]).astype(BF16)
    cos = cos_ref[...].T
    sin = sin_ref[...].T
    half = MLA_ROPE // 2
    lane = lax.broadcasted_iota(jnp.int32, sin.shape, 1)
    sin_lo = jnp.where(lane < MLA_NOPE + half, sin, 0.0)
    sin_hi = sin - sin_lo

    def rope(x):
        return (x * cos + pltpu.roll(x, HEAD_PAD - half, 1) * sin_lo + pltpu.roll(x, half, 1) * sin_hi)

    qa = _dot(cqn, wqa_ref[...])
    q = jnp.concatenate([rope(qa[:, h * HEAD_PAD:(h + 1) * HEAD_PAD]) for h in range(MLA_HEADS)], axis=1)
    q_out[...] = (q * q_scale).astype(BF16)
    kr = rope(c[:, q_lora + kv_lora:])
    k = _dot(ckvn, wuk_ref[...]) + jnp.concatenate([kr] * MLA_HEADS, axis=1)
    k_out[...] = k.astype(BF16)
    vt = _dot_nt(wuvt_ref[...], ckvn)
    ones = jnp.ones((HEAD_PAD - MLA_V, tm), F32)
    pieces = []
    for h in range(MLA_HEADS):
        pieces += [vt[h * MLA_V:(h + 1) * MLA_V], ones]
    va_out[0] = jnp.concatenate(pieces, axis=0).astype(BF16)


def _aligned(x, m):
    return x if isinstance(x, int) else pl.multiple_of(x, m)


def _attn_kernel(q_ref, k_ref, vt_ref, o_ref, s_0, s_1, p_0, p_1, p_2, p_3, acc_scr, *,
                 tq, tk, vchunk, n_q, n_k):
    s_buf, p_buf = (s_0, s_1), (p_0, p_1, p_2, p_3)
    n_slots = len(p_buf)
    bodies = ATTN_BODIES_PER_ITER
    sub = tk // vchunk
    n_total = n_q * n_k
    assert bodies % n_slots == 0 and n_k % bodies == 0 and n_slots % 2 == 0 and n_total >= 6

    def scores(j, par):
        qoff = _aligned((j // n_k) * tq, tq)
        koff = _aligned((j % n_k) * tk, tk)
        cmax = []
        for hh in range(2):
            cols = slice(hh * HEAD_PAD, (hh + 1) * HEAD_PAD)
            s = _dot_nt(k_ref[pl.ds(koff, tk), cols], q_ref[pl.ds(qoff, tq), cols])
            s_buf[par][hh] = s
            cmax.append(jnp.max(s, axis=0, keepdims=True))
        return tuple(cmax)

    def pv_issue(j, slot):
        kb = j % n_k
        out = []
        for hh in range(2):
            vt = jnp.concatenate([vt_ref[kb * sub + c, hh * HEAD_PAD:(hh + 1) * HEAD_PAD, :]
                                  for c in range(sub)], axis=1)
            out.append(_dot(vt, p_buf[slot][hh]))
        return out

    def pv_accumulate(r, alpha):
        for hh in range(2):
            acc_scr[hh] = alpha[hh] * acc_scr[hh] + r[hh]

    def softmax(j, par, slot, m, cmax):
        first = (j % n_k) == 0
        m_out, alpha = [], []
        for hh in range(2):
            m_old = jnp.where(first, -jnp.inf, m[hh])
            m_new = jnp.maximum(m_old, cmax[hh])
            p_buf[slot][hh] = jnp.exp2(s_buf[par][hh] - m_new).astype(BF16)
            alpha.append(jnp.exp2(m_old - m_new))
            m_out.append(m_new)
        return tuple(m_out), tuple(alpha)

    def finalize(j):
        qoff = _aligned((j // n_k) * tq, tq)
        o_t = jnp.concatenate([acc_scr[hh, :MLA_V] / acc_scr[hh, MLA_V:MLA_V + 1] for hh in range(2)],
                              axis=0)
        o_ref[pl.ds(qoff, tq), :] = o_t.T.astype(BF16)

    def pv(j, alpha, r=None):
        pv_accumulate(pv_issue(j, j % n_slots) if r is None else r, alpha)
        if isinstance(j, int) and j % n_k == n_k - 1:
            finalize(j)

    def body(k, kmod, m, alpha_prev, alpha_cur, cmax):
        par = kmod % 2
        r = pv_issue(k - 1, (kmod - 1) % n_slots)
        cmax_next = scores(k + 2, par)
        m, alpha_next = softmax(k + 1, 1 - par, (kmod + 1) % n_slots, m, cmax)
        pv(k - 1, alpha_prev, r)
        return m, alpha_cur, alpha_next, cmax_next

    def loop_body(kk, carry):
        k0 = bodies * kk + 1
        for o in range(bodies):
            carry = body(k0 + o, (1 + o) % n_slots, *carry)
        last = k0 + bodies - 2
        pl.when(last % n_k == n_k - 1)(functools.partial(finalize, last))
        return carry

    acc_scr[...] = jnp.zeros_like(acc_scr)
    m = tuple(jnp.full((1, tq), -jnp.inf, F32) for _ in range(2))
    cmax0 = scores(0, 0)
    cmax1 = scores(1, 1)
    m, alpha0 = softmax(0, 0, 0, m, cmax0)
    cmax2 = scores(2, 0)
    m, alpha1 = softmax(1, 1, 1, m, cmax1)
    n_iter = (n_total - 3) // bodies
    carry = lax.fori_loop(0, n_iter, loop_body, (m, alpha0, alpha1, cmax2))
    k = bodies * n_iter + 1
    while k <= n_total - 3:
        carry = body(k, k % n_slots, *carry)
        k += 1
    m, alpha_prev, alpha_cur, cmax = carry
    r = pv_issue(k - 1, (k - 1) % n_slots)
    m, alpha_last = softmax(k + 1, (k + 1) % 2, (k + 1) % n_slots, m, cmax)
    pv(k - 1, alpha_prev, r)
    pv(n_total - 2, alpha_cur)
    pv(n_total - 1, alpha_last)


def _scan_lanes(x, op, fill, reverse):
    n = x.shape[1]
    lane = lax.broadcasted_iota(jnp.int32, x.shape, 1)
    shift = 1
    while shift < n:
        if reverse:
            moved, valid = pltpu.roll(x, n - shift, 1), lane < n - shift
        else:
            moved, valid = pltpu.roll(x, shift, 1), lane >= shift
        x = op(x, jnp.where(valid, moved, fill))
        shift *= 2
    return x


class _Chain:
    pass


def _mlstm_kernel(q_ref, k_ref, vt_ref, mot_ref, gate_ref, og_ref, o_ref,
                  hacc, st_scr, b_scr, a_scr, cm_scr, bl_scr, am_scr, *, L, nc):
    dh = MLSTM_DH
    aug = st_scr.shape[1]
    s_idx = lax.broadcasted_iota(jnp.int32, (L, L), 0)
    t_idx = lax.broadcasted_iota(jnp.int32, (L, L), 1)
    eye = s_idx == t_idx
    visible = (s_idx <= t_idx, s_idx >= t_idx)
    ones_blk = (lax.broadcasted_iota(jnp.int32, (aug - dh, L), 0) == 0).astype(BF16)

    for d in range(2):
        logi = gate_ref[d, 0, 0]
        logf = gate_ref[d, 1, 0]
        b = _scan_lanes(logf, jnp.add, 0.0, reverse=d == 1)
        a = logi - b
        b_scr[d] = b
        a_scr[d] = a
        cm_scr[d] = _scan_lanes(a, jnp.maximum, -jnp.inf, reverse=d == 1)
        bl_scr[d] = jnp.sum(logf, axis=1, keepdims=True)
        am_scr[d] = jnp.max(a, axis=1, keepdims=True)
    st_scr[...] = jnp.zeros_like(st_scr)

    def prepare(d, c, m):
        x = _Chain()
        off = pl.multiple_of(c * L, L)
        x.q = q_ref[pl.ds(off, L), :]
        x.k = k_ref[pl.ds(off, L), :]
        x.vaug = jnp.concatenate([vt_ref[c], ones_blk], axis=0)
        x.b_row = b_scr[d, pl.ds(c, 1), :]
        x.a_row = a_scr[d, pl.ds(c, 1), :]
        b_last = bl_scr[d, pl.ds(c, 1), :]
        x.g_row = jnp.maximum(cm_scr[d, pl.ds(c, 1), :], m)
        x.iw = jnp.exp(m - x.g_row)
        x.m_new = jnp.maximum(b_last + m, b_last + am_scr[d, pl.ds(c, 1), :])
        x.decay = jnp.exp(b_last + m - x.m_new)
        x.w_row = jnp.exp(b_last + x.a_row - x.m_new)
        return x

    def intra(d, x):
        a_col = jnp.sum(jnp.where(eye, x.a_row, 0.0), axis=1, keepdims=True)
        e_t = jnp.exp(jnp.where(visible[d], a_col - x.g_row, -jnp.inf))
        x.r = _dot(x.vaug, (x.st * e_t).astype(BF16))

    def finish(x):
        num = x.iw * x.inter[:dh] + x.r[:dh]
        den = x.iw * x.inter[dh:dh + 1] + x.r[dh:dh + 1]
        floor = jnp.exp(-(x.b_row + x.g_row))
        return num * (1.0 / jnp.maximum(jnp.abs(den), floor))

    def run(j0, m_f, m_b):
        chains = []
        for o in range(MLSTM_POSITIONS_PER_BODY):
            j = j0 + o
            xf = prepare(0, j, m_f)
            xb = prepare(1, nc - 1 - j, m_b)
            m_f, m_b = xf.m_new, xb.m_new
            chains += [(0, j, xf), (1, nc - 1 - j, xb)]
        for d, _, x in chains:
            x.st = _dot_nt(x.k, x.q)
            x.upd = _dot((x.vaug.astype(F32) * x.w_row).astype(BF16), x.k)
        for d, _, x in chains:
            x.inter = _dot_nt(st_scr[d].astype(BF16), x.q)
            st_scr[d] = x.decay * st_scr[d] + x.upd
        for d, _, x in chains:
            intra(d, x)
        return [(c, finish(x)) for _, c, x in chains], m_f, m_b

    def first_touch(jj, carry):
        outs, m_f, m_b = run(jj * MLSTM_POSITIONS_PER_BODY, *carry)
        for c, h in outs:
            hacc[c] = h
        return m_f, m_b

    def second_touch(jj, carry):
        outs, m_f, m_b = run(jj * MLSTM_POSITIONS_PER_BODY, *carry)
        for c, h in outs:
            tot = hacc[c] + h
            y = tot * lax.rsqrt(jnp.mean(tot * tot, axis=0, keepdims=True) + NORM_EPS) * og_ref[...]
            o_ref[c] = (y * mot_ref[c].astype(F32)).astype(BF16)
        return m_f, m_b

    n_iter = nc // MLSTM_POSITIONS_PER_BODY
    m0 = jnp.zeros((1, 1), F32)
    carry = lax.fori_loop(0, n_iter // 2, first_touch, (m0, m0))
    lax.fori_loop(n_iter // 2, n_iter, second_touch, carry)


def _merge_mlp_kernel(x_ref, ya_ref, ymt_ref, g1_ref, wgab_ref, wbm_ref, wbl_ref, wout_ref,
                      g2_ref, wup_ref, wdn_ref, gf_ref, o_ref, *, d, final_norm):
    x = x_ref[...]
    hn = _rms(x, g1_ref[...]).astype(BF16)
    gates = jax.nn.sigmoid(_dot(hn, wgab_ref[...]))
    merged = (gates[:, :d] * _dot(ya_ref[...], wbm_ref[...])
              + gates[:, d:] * _dot_tn(ymt_ref[0], wbl_ref[...]))
    x1 = x + _dot(merged.astype(BF16), wout_ref[...])
    u = _dot(_rms(x1, g2_ref[...]).astype(BF16), wup_ref[...])
    r = jnp.maximum(u, 0.0)
    x2 = x1 + _dot((r * r).astype(BF16), wdn_ref[...])
    o_ref[...] = _rms(x2, gf_ref[...]) if final_norm else x2


def _const_spec(shape):
    return pl.BlockSpec(shape, lambda *_: (0,) * len(shape))


def _resident_spec(shape):
    return pl.BlockSpec(shape, lambda *_: (0,) * len(shape), pipeline_mode=pl.Buffered(1))


def _params(n_axes):
    return pltpu.CompilerParams(dimension_semantics=("arbitrary",) * n_axes,
                                vmem_limit_bytes=VMEM_LIMIT_BYTES)


def _rope_kernel(pos_ref, freq_ref, cos_out, sin_out):
    t = pos_ref.shape[1]
    ang = freq_ref[...] * pos_ref[...]
    cos, sin = jnp.cos(ang), jnp.sin(ang)
    pad = jnp.zeros((HEAD_PAD - MLA_NOPE - MLA_ROPE, t), F32)
    cos_out[...] = jnp.concatenate([jnp.ones((MLA_NOPE, t), F32), cos, cos, pad], axis=0)
    sin_out[...] = jnp.concatenate([jnp.zeros((MLA_NOPE, t), F32), -sin, sin, pad], axis=0)


def _rope_tables(positions):
    t = positions.size
    inv_freq = ROPE_THETA ** (-jnp.arange(0, MLA_ROPE, 2, dtype=F32) / MLA_ROPE)
    table = jax.ShapeDtypeStruct((HEAD_PAD, t), F32)
    tile = min(t, ROPE_TABLE_LANES)
    return pl.pallas_call(
        _rope_kernel,
        grid=(t // tile,),
        in_specs=[pl.BlockSpec((1, tile), lambda i: (0, i)), _const_spec((MLA_ROPE // 2, 1))],
        out_specs=[pl.BlockSpec((HEAD_PAD, tile), lambda i: (0, i))] * 2,
        out_shape=[table, table], compiler_params=_params(1), name="rope_table",
    )(positions.astype(F32).reshape(1, t), inv_freq.reshape(-1, 1))


def _pad_heads(w, width_in, offset_out):
    k = w.shape[0]
    w = w.reshape(k, MLA_HEADS, width_in)
    w = jnp.pad(w, ((0, 0), (0, 0), (offset_out, HEAD_PAD - width_in - offset_out)))
    return w.reshape(k, MLA_HEADS * HEAD_PAD)


def kernel(x, positions, norm_mix_g, w_in, mla_q_norm_g, mla_w_uq, mla_kv_norm_g, mla_w_ukv, mlstm_conv_w, mlstm_conv_b, mlstm_igate_b, mlstm_fgate_b, mlstm_out_norm_g, w_branch_mla, w_branch_mlstm, w_out, norm_mlp_g, w_mlp_up, w_mlp_down, norm_final_g):
    B, S, D = x.shape
    T = B * S
    depth = w_in.shape[0]
    q_lora = mla_q_norm_g.shape[1]
    kv_lora = mla_kv_norm_g.shape[1]
    mw = MLSTM_HEADS * MLSTM_DH
    n_gate = 4 * MLSTM_HEADS
    L = MLSTM_CHUNK
    nc = S // L
    tm = tm4 = ROW_TILE
    tq, tk = ATTN_Q_TILE, ATTN_KEY_STEP
    assert S % (2 * MLSTM_POSITIONS_PER_BODY * L) == 0 and MLA_HEADS % 2 == 0
    assert S % tk == 0 and S % tq == 0 and tk % tm == 0 and tm % HALO == 0

    cos_t, sin_t = _rope_tables(positions)
    xf = x.reshape(T, D)

    for l in range(depth):
        offs, o = [], 0
        for w in (q_lora, kv_lora, MLA_ROPE, mw, mw, mw, mw, n_gate, D, D):
            offs.append(o)
            o += w
        wl = w_in[l]
        w_cq = wl[:, offs[0]:offs[0] + q_lora]
        w_ckv = wl[:, offs[1]:offs[1] + kv_lora]
        w_kr = wl[:, offs[2]:offs[2] + MLA_ROPE]
        pad_kr = ((0, 0), (MLA_NOPE, HEAD_PAD - MLA_NOPE - MLA_ROPE))
        w1 = wl[:, offs[3]:offs[3] + 2 * mw].astype(BF16)
        w2t = wl[:, offs[5]:offs[5] + 2 * mw].T.astype(BF16)
        w3 = jnp.concatenate([w_cq, w_ckv, jnp.pad(w_kr, pad_kr)], axis=1).astype(BF16)
        w2t = jnp.concatenate([w2t, wl[:, offs[7]:offs[7] + n_gate].T.astype(BF16)], axis=0)
        wgab = wl[:, offs[8]:offs[8] + 2 * D].astype(BF16)
        gate_b = jnp.stack([mlstm_igate_b[l], mlstm_fgate_b[l]], axis=1).reshape(n_gate, 1)

        wqa = _pad_heads(mla_w_uq[l], MLA_NOPE + MLA_ROPE, 0).astype(BF16)
        wkv = mla_w_ukv[l].reshape(kv_lora, MLA_HEADS, MLA_NOPE + MLA_V)
        wuk = _pad_heads(wkv[:, :, :MLA_NOPE].reshape(kv_lora, -1), MLA_NOPE, 0).astype(BF16)
        wuvt = wkv[:, :, MLA_NOPE:].reshape(kv_lora, MLA_HEADS * MLA_V).T.astype(BF16)

        cw = jnp.pad(mlstm_conv_w[l].reshape(CONV_WIDTH, 2 * mw), ((0, F32_SUBLANES - CONV_WIDTH), (0, 0)))
        cb = mlstm_conv_b[l].reshape(1, 2 * mw)

        tiles_per_seq = S // tm
        hb = tm // HALO
        n_halo_blocks = T // HALO
        row = lambda i: (i, 0)
        col = lambda i: (0, i)
        chunk = lambda i: (i, 0, 0)
        kern = functools.partial(
            _inproj_kernel, tm=tm, tiles_per_seq=tiles_per_seq, q_lora=q_lora, kv_lora=kv_lora, mw=mw,
            q_scale=(MLA_NOPE + MLA_ROPE) ** -0.5 * math.log2(math.e), k_scale=MLSTM_DH ** -0.5)
        outs = pl.pallas_call(
            kern,
            grid=(T // tm,),
            in_specs=[
                pl.BlockSpec((tm, D), row),
                pl.BlockSpec((HALO, D), lambda i: (jnp.maximum(i * hb - 1, 0), 0)),
                pl.BlockSpec((HALO, D), lambda i: (jnp.minimum((i + 1) * hb, n_halo_blocks - 1), 0)),
                pl.BlockSpec((HEAD_PAD, tm), col),
                pl.BlockSpec((HEAD_PAD, tm), col),
                _const_spec((1, D)),
                _const_spec(w1.shape), _const_spec(w2t.shape), _const_spec(w3.shape),
                _const_spec(gate_b.shape),
                _const_spec(cw.shape), _const_spec(cb.shape),
                _const_spec((1, q_lora)), _const_spec(wqa.shape),
                _const_spec((1, kv_lora)), _const_spec(wuk.shape), _const_spec(wuvt.shape),
            ],
            out_specs=[
                pl.BlockSpec((tm, MLA_HEADS * HEAD_PAD), row),
                pl.BlockSpec((tm, MLA_HEADS * HEAD_PAD), row),
                pl.BlockSpec((1, MLA_HEADS * HEAD_PAD, tm), chunk),
                pl.BlockSpec((tm, mw), row), pl.BlockSpec((tm, mw), row),
                pl.BlockSpec((1, mw, tm), chunk),
                pl.BlockSpec((1, mw, tm), chunk),
                pl.BlockSpec((n_gate, tm), col),
            ],
            out_shape=[
                jax.ShapeDtypeStruct((T, MLA_HEADS * HEAD_PAD), BF16),
                jax.ShapeDtypeStruct((T, MLA_HEADS * HEAD_PAD), BF16),
                jax.ShapeDtypeStruct((T // tm, MLA_HEADS * HEAD_PAD, tm), BF16),
                jax.ShapeDtypeStruct((T, mw), BF16), jax.ShapeDtypeStruct((T, mw), BF16),
                jax.ShapeDtypeStruct((T // tm, mw, tm), BF16),
                jax.ShapeDtypeStruct((T // tm, mw, tm), BF16),
                jax.ShapeDtypeStruct((n_gate, T), F32),
            ],
            compiler_params=_params(1),
            name="inproj",
        )(xf, xf, xf, cos_t, sin_t, norm_mix_g[l].reshape(1, D), w1, w2t, w3, gate_b, cw, cb,
          mla_q_norm_g[l].reshape(1, q_lora), wqa, mla_kv_norm_g[l].reshape(1, kv_lora), wuk, wuvt)
        q_a, k_a, v_a, q_m, k_m, v_m, mo_s, gates = outs

        vchunk = tm
        y_attn = pl.pallas_call(
            functools.partial(_attn_kernel, tq=tq, tk=tk, vchunk=vchunk, n_q=S // tq, n_k=S // tk),
            grid=(B, MLA_HEADS // 2),
            in_specs=[
                pl.BlockSpec((S, 2 * HEAD_PAD), lambda b, p: (b, p)),
                pl.BlockSpec((S, 2 * HEAD_PAD), lambda b, p: (b, p)),
                pl.BlockSpec((S // vchunk, 2 * HEAD_PAD, vchunk), lambda b, p: (b, p, 0)),
            ],
            out_specs=pl.BlockSpec((S, 2 * MLA_V), lambda b, p: (b, p)),
            out_shape=jax.ShapeDtypeStruct((T, MLA_HEADS * MLA_V), BF16),
            scratch_shapes=[pltpu.VMEM((2, tk, tq), F32), pltpu.VMEM((2, tk, tq), F32),
                            pltpu.VMEM((2, tk, tq), BF16), pltpu.VMEM((2, tk, tq), BF16),
                            pltpu.VMEM((2, tk, tq), BF16), pltpu.VMEM((2, tk, tq), BF16),
                            pltpu.VMEM((2, HEAD_PAD, tq), F32)],
            compiler_params=_params(2),
            name="mla_attn",
        )(q_a, k_a, v_a)

        gates5 = gates.reshape(2, 2, MLSTM_HEADS, T // L, L)
        head_blk = lambda b, h: (b, h)
        head_blk_t = lambda b, h: (b, h, 0)
        aug = MLSTM_DH + BF16_SUBLANES
        row_scratch = pltpu.VMEM((2, nc, L), F32)
        y_mlstm_t = pl.pallas_call(
            functools.partial(_mlstm_kernel, L=L, nc=nc),
            grid=(B, MLSTM_HEADS),
            in_specs=[
                pl.BlockSpec((S, MLSTM_DH), head_blk), pl.BlockSpec((S, MLSTM_DH), head_blk),
                pl.BlockSpec((nc, MLSTM_DH, L), head_blk_t), pl.BlockSpec((nc, MLSTM_DH, L), head_blk_t),
                pl.BlockSpec((2, 2, 1, nc, L), lambda b, h: (0, 0, h, b, 0)),
                pl.BlockSpec((MLSTM_DH, 1), lambda b, h: (h, 0)),
            ],
            out_specs=pl.BlockSpec((nc, MLSTM_DH, L), head_blk_t),
            out_shape=jax.ShapeDtypeStruct((T // L, mw, L), BF16),
            scratch_shapes=[pltpu.VMEM((nc, MLSTM_DH, L), F32),
                            pltpu.VMEM((2, aug, MLSTM_DH), F32),
                            row_scratch, row_scratch, row_scratch,
                            pltpu.VMEM((2, nc, 1), F32), pltpu.VMEM((2, nc, 1), F32)],
            compiler_params=_params(2),
            name="mlstm",
        )(q_m, k_m, v_m, mo_s, gates5, mlstm_out_norm_g[l].reshape(mw, 1))

        last = l == depth - 1
        gf = norm_final_g.reshape(1, D)
        weights4 = (wgab, w_branch_mla[l].astype(BF16), w_branch_mlstm[l].astype(BF16),
                    w_out[l].astype(BF16), w_mlp_up[l].astype(BF16), w_mlp_down[l].astype(BF16))
        xf = pl.pallas_call(
            functools.partial(_merge_mlp_kernel, d=D, final_norm=last),
            grid=(T // tm4,),
            in_specs=[
                pl.BlockSpec((tm4, D), row),
                pl.BlockSpec((tm4, MLA_HEADS * MLA_V), row),
                pl.BlockSpec((1, mw, tm4), lambda i: (i, 0, 0)),
                _const_spec((1, D)),
                _resident_spec(weights4[0].shape), _resident_spec(weights4[1].shape),
                _resident_spec(weights4[2].shape), _resident_spec(weights4[3].shape),
                _const_spec((1, D)),
                _resident_spec(weights4[4].shape), _resident_spec(weights4[5].shape),
                _const_spec((1, D)),
            ],
            out_specs=pl.BlockSpec((tm4, D), row),
            out_shape=jax.ShapeDtypeStruct((T, D), F32),
            compiler_params=_params(1),
            name="merge_mlp",
        )(xf, y_attn, y_mlstm_t, norm_mix_g[l].reshape(1, D), weights4[0], weights4[1], weights4[2],
          weights4[3], norm_mlp_g[l].reshape(1, D), weights4[4], weights4[5], gf)

    return xf.reshape(B, S, D)
```

```python
import functools
import math

import jax
import jax.numpy as jnp
from jax import lax
from jax.experimental import pallas as pl
from jax.experimental.pallas import tpu as pltpu

MLA_HEADS = 8
MLA_NOPE = 64
MLA_ROPE = 32
MLA_V = 64
ROPE_THETA = 10000.0
MLSTM_HEADS = 4
MLSTM_DH = 128
CONV_WIDTH = 5
NORM_EPS = 1e-6

LANES = 128
F32_SUBLANES = 8
BF16_SUBLANES = 16
VMEM_LIMIT_BYTES = 56 * 1024 * 1024

MXU_WIDTH = 256

HEAD_PAD = LANES
MLSTM_CHUNK = MXU_WIDTH
MLSTM_POSITIONS_PER_BODY = 4
ROW_TILE = MLSTM_CHUNK
ATTN_Q_TILE = MXU_WIDTH
ATTN_KEY_STEP = 4 * MXU_WIDTH
ATTN_BODIES_PER_ITER = 8
HALO = BF16_SUBLANES
ROPE_TABLE_LANES = 16 * LANES

F32 = jnp.float32
BF16 = jnp.bfloat16


def _rms(x, g):
    return x * lax.rsqrt(jnp.mean(x * x, axis=-1, keepdims=True) + NORM_EPS) * g


def _dot(a, b):
    return jnp.dot(a, b, preferred_element_type=F32)


def _dot_nt(a, b):
    return lax.dot_general(a, b, (((1,), (1,)), ((), ())), preferred_element_type=F32)


def _dot_tn(a, b):
    return lax.dot_general(a, b, (((0,), (0,)), ((), ())), preferred_element_type=F32)


def _log_sigmoid(x):
    return jnp.minimum(x, 0.0) - jnp.log1p(jnp.exp(-jnp.abs(x)))


def _inproj_kernel(xm_ref, xp_ref, xn_ref, cos_ref, sin_ref, g_ref, w1_ref, w2t_ref, w3_ref,
                   gb_ref, cw_ref, cb_ref, qg_ref, wqa_ref, kvg_ref,
                   wuk_ref, wuvt_ref,
                   q_out, k_out, va_out, qm_out, km_out, vm_out, mo_out, gate_out,
                   *, tm, tiles_per_seq, q_lora, kv_lora, mw, q_scale, k_scale):
    i = pl.program_id(0)
    pos_in_seq = i % tiles_per_seq
    xp = jnp.where(pos_in_seq == 0, 0.0, xp_ref[...])
    xn = jnp.where(pos_in_seq == tiles_per_seq - 1, 0.0, xn_ref[...])
    xe = jnp.concatenate([xp, xm_ref[...], xn], axis=0)
    he = _rms(xe, g_ref[...]).astype(BF16)
    hm = he[HALO:HALO + tm]

    pre = _dot(he, w1_ref[...])
    rows = pre.shape[0]
    conv = cb_ref[...]
    for j in range(CONV_WIDTH):
        shifted = pre if j == CONV_WIDTH // 2 else pltpu.roll(pre, (CONV_WIDTH // 2 - j) % rows, 0)
        conv = conv + cw_ref[j:j + 1, :] * shifted[HALO:HALO + tm]
    qk = conv * jax.nn.sigmoid(conv)
    qm_out[...] = qk[:, :mw].astype(BF16)
    km_out[...] = (qk[:, mw:] * k_scale).astype(BF16)

    vo_t = _dot_nt(w2t_ref[...], hm)
    vm_out[0] = vo_t[:mw].astype(BF16)
    mo_out[0] = jax.nn.sigmoid(vo_t[mw:2 * mw]).astype(BF16)

    gt = vo_t[2 * mw:] + gb_ref[...]
    row = lax.broadcasted_iota(jnp.int32, gt.shape, 0)
    is_f = (row % (2 * MLSTM_HEADS)) >= MLSTM_HEADS
    gate_out[...] = jnp.where(is_f, _log_sigmoid(gt), gt)

    c = _dot(hm, w3_ref[...])
    cqn = _rms(c[:, :q_lora], qg_ref[...]).astype(BF16)
    ckvn = _rms(c[:, q_lora:q_lora + kv_lora], kvg_ref[...]).astype(BF16)
    cos = cos_ref[...].T
    sin = sin_ref[...].T
    half = MLA_ROPE // 2
    lane = lax.broadcasted_iota(jnp.int32, sin.shape, 1)
    sin_lo = jnp.where(lane < MLA_NOPE + half, sin, 0.0)
    sin_hi = sin - sin_lo

    def rope(x):
        return (x * cos + pltpu.roll(x, HEAD_PAD - half, 1) * sin_lo + pltpu.roll(x, half, 1) * sin_hi)

    qa = _dot(cqn, wqa_ref[...])
    q = jnp.concatenate([rope(qa[:, h * HEAD_PAD:(h + 1) * HEAD_PAD]) for h in range(MLA_HEADS)], axis=1)
    q_out[...] = (q * q_scale).astype(BF16)
    kr = rope(c[:, q_lora + kv_lora:])
    k = _dot(ckvn, wuk_ref[...]) + jnp.concatenate([kr] * MLA_HEADS, axis=1)
    k_out[...] = k.astype(BF16)
    vt = _dot_nt(wuvt_ref[...], ckvn)
    ones = jnp.ones((HEAD_PAD - MLA_V, tm), F32)
    pieces = []
    for h in range(MLA_HEADS):
        pieces += [vt[h * MLA_V:(h + 1) * MLA_V], ones]
    va_out[0] = jnp.concatenate(pieces, axis=0).astype(BF16)


def _aligned(x, m):
    return x if isinstance(x, int) else pl.multiple_of(x, m)


def _attn_kernel(q_ref, k_ref, vt_ref, o_ref, s_0, s_1, p_0, p_1, p_2, p_3, acc_scr, *,
                 tq, tk, vchunk, n_q, n_k):
    s_buf, p_buf = (s_0, s_1), (p_0, p_1, p_2, p_3)
    n_slots = len(p_buf)
    bodies = ATTN_BODIES_PER_ITER
    sub = tk // vchunk
    n_total = n_q * n_k
    assert bodies % n_slots == 0 and n_k % bodies == 0 and n_slots % 2 == 0 and n_total >= 6

    def scores(j, par):
        qoff = _aligned((j // n_k) * tq, tq)
        koff = _aligned((j % n_k) * tk, tk)
        cmax = []
        for hh in range(2):
            cols = slice(hh * HEAD_PAD, (hh + 1) * HEAD_PAD)
            s = _dot_nt(k_ref[pl.ds(koff, tk), cols], q_ref[pl.ds(qoff, tq), cols])
            s_buf[par][hh] = s
            cmax.append(jnp.max(s, axis=0, keepdims=True))
        return tuple(cmax)

    def pv_issue(j, slot):
        kb = j % n_k
        out = []
        for hh in range(2):
            vt = jnp.concatenate([vt_ref[kb * sub + c, hh * HEAD_PAD:(hh + 1) * HEAD_PAD, :]
                                  for c in range(sub)], axis=1)
            out.append(_dot(vt, p_buf[slot][hh]))
        return out

    def pv_accumulate(r, alpha):
        for hh in range(2):
            acc_scr[hh] = alpha[hh] * acc_scr[hh] + r[hh]

    def softmax(j, par, slot, m, cmax):
        first = (j % n_k) == 0
        m_out, alpha = [], []
        for hh in range(2):
            m_old = jnp.where(first, -jnp.inf, m[hh])
            m_new = jnp.maximum(m_old, cmax[hh])
            p_buf[slot][hh] = jnp.exp2(s_buf[par][hh] - m_new).astype(BF16)
            alpha.append(jnp.exp2(m_old - m_new))
            m_out.append(m_new)
        return tuple(m_out), tuple(alpha)

    def finalize(j):
        qoff = _aligned((j // n_k) * tq, tq)
        o_t = jnp.concatenate([acc_scr[hh, :MLA_V] / acc_scr[hh, MLA_V:MLA_V + 1] for hh in range(2)],
                              axis=0)
        o_ref[pl.ds(qoff, tq), :] = o_t.T.astype(BF16)

    def pv(j, alpha, r=None):
        pv_accumulate(pv_issue(j, j % n_slots) if r is None else r, alpha)
        if isinstance(j, int) and j % n_k == n_k - 1:
            finalize(j)

    def body(k, kmod, m, alpha_prev, alpha_cur, cmax):
        par = kmod % 2
        r = pv_issue(k - 1, (kmod - 1) % n_slots)
        cmax_next = scores(k + 2, par)
        m, alpha_next = softmax(k + 1, 1 - par, (kmod + 1) % n_slots, m, cmax)
        pv(k - 1, alpha_prev, r)
        return m, alpha_cur, alpha_next, cmax_next

    def loop_body(kk, carry):
        k0 = bodies * kk + 1
        for o in range(bodies):
            carry = body(k0 + o, (1 + o) % n_slots, *carry)
        last = k0 + bodies - 2
        if bodies == n_k:
            finalize(last)
        else:
            pl.when(last % n_k == n_k - 1)(functools.partial(finalize, last))
        return carry

    acc_scr[...] = jnp.zeros_like(acc_scr)
    m = tuple(jnp.full((1, tq), -jnp.inf, F32) for _ in range(2))
    cmax0 = scores(0, 0)
    cmax1 = scores(1, 1)
    m, alpha0 = softmax(0, 0, 0, m, cmax0)
    cmax2 = scores(2, 0)
    m, alpha1 = softmax(1, 1, 1, m, cmax1)
    n_iter = (n_total - 3) // bodies
    carry = lax.fori_loop(0, n_iter, loop_body, (m, alpha0, alpha1, cmax2))
    k = bodies * n_iter + 1
    while k <= n_total - 3:
        carry = body(k, k % n_slots, *carry)
        k += 1
    m, alpha_prev, alpha_cur, cmax = carry
    r = pv_issue(k - 1, (k - 1) % n_slots)
    m, alpha_last = softmax(k + 1, (k + 1) % 2, (k + 1) % n_slots, m, cmax)
    pv(k - 1, alpha_prev, r)
    pv(n_total - 2, alpha_cur)
    pv(n_total - 1, alpha_last)


def _scan_lanes(x, op, fill, reverse):
    n = x.shape[1]
    lane = lax.broadcasted_iota(jnp.int32, x.shape, 1)
    shift = 1
    while shift < n:
        if reverse:
            moved, valid = pltpu.roll(x, n - shift, 1), lane < n - shift
        else:
            moved, valid = pltpu.roll(x, shift, 1), lane >= shift
        x = op(x, jnp.where(valid, moved, fill))
        shift *= 2
    return x


class _Chain:
    pass


def _mlstm_kernel(q_ref, k_ref, vt_ref, mot_ref, gate_ref, og_ref, o_ref,
                  hacc, st_scr, b_scr, a_scr, cm_scr, bl_scr, am_scr, *, L, nc):
    dh = MLSTM_DH
    aug = st_scr.shape[1]
    s_idx = lax.broadcasted_iota(jnp.int32, (L, L), 0)
    t_idx = lax.broadcasted_iota(jnp.int32, (L, L), 1)
    eye = s_idx == t_idx
    visible = (s_idx <= t_idx, s_idx >= t_idx)
    ones_blk = (lax.broadcasted_iota(jnp.int32, (aug - dh, L), 0) == 0).astype(BF16)

    for d in range(2):
        logi = gate_ref[d, 0, 0]
        logf = gate_ref[d, 1, 0]
        b = _scan_lanes(logf, jnp.add, 0.0, reverse=d == 1)
        a = logi - b
        b_scr[d] = b
        a_scr[d] = a
        cm_scr[d] = _scan_lanes(a, jnp.maximum, -jnp.inf, reverse=d == 1)
        bl_scr[d] = jnp.sum(logf, axis=1, keepdims=True)
        am_scr[d] = jnp.max(a, axis=1, keepdims=True)
    st_scr[...] = jnp.zeros_like(st_scr)

    def prepare(d, c, m):
        x = _Chain()
        off = pl.multiple_of(c * L, L)
        x.q = q_ref[pl.ds(off, L), :]
        x.k = k_ref[pl.ds(off, L), :]
        x.vaug = jnp.concatenate([vt_ref[c], ones_blk], axis=0)
        x.b_row = b_scr[d, pl.ds(c, 1), :]
        x.a_row = a_scr[d, pl.ds(c, 1), :]
        b_last = bl_scr[d, pl.ds(c, 1), :]
        x.g_row = jnp.maximum(cm_scr[d, pl.ds(c, 1), :], m)
        x.iw = jnp.exp(m - x.g_row)
        x.m_new = jnp.maximum(b_last + m, b_last + am_scr[d, pl.ds(c, 1), :])
        x.decay = jnp.exp(b_last + m - x.m_new)
        x.w_row = jnp.exp(b_last + x.a_row - x.m_new)
        return x

    def intra(d, x):
        a_col = jnp.sum(jnp.where(eye, x.a_row, 0.0), axis=1, keepdims=True)
        e_t = jnp.exp(jnp.where(visible[d], a_col - x.g_row, -jnp.inf))
        x.r = _dot(x.vaug, (x.st * e_t).astype(BF16))

    def finish(x):
        num = x.iw * x.inter[:dh] + x.r[:dh]
        den = x.iw * x.inter[dh:dh + 1] + x.r[dh:dh + 1]
        floor = jnp.exp(-(x.b_row + x.g_row))
        return num * (1.0 / jnp.maximum(jnp.abs(den), floor))

    def run(j0, m_f, m_b):
        chains = []
        for o in range(MLSTM_POSITIONS_PER_BODY):
            j = j0 + o
            xf = prepare(0, j, m_f)
            xb = prepare(1, nc - 1 - j, m_b)
            m_f, m_b = xf.m_new, xb.m_new
            chains += [(0, j, xf), (1, nc - 1 - j, xb)]
        for d, _, x in chains:
            x.st = _dot_nt(x.k, x.q)
            x.upd = _dot((x.vaug.astype(F32) * x.w_row).astype(BF16), x.k)
        for d, _, x in chains:
            x.inter = _dot_nt(st_scr[d].astype(BF16), x.q)
            st_scr[d] = x.decay * st_scr[d] + x.upd
        for d, _, x in chains:
            intra(d, x)
        return [(c, finish(x)) for _, c, x in chains], m_f, m_b

    def first_touch(jj, carry):
        outs, m_f, m_b = run(jj * MLSTM_POSITIONS_PER_BODY, *carry)
        for c, h in outs:
            hacc[c] = h
        return m_f, m_b

    def second_touch(jj, carry):
        outs, m_f, m_b = run(jj * MLSTM_POSITIONS_PER_BODY, *carry)
        for c, h in outs:
            tot = hacc[c] + h
            y = tot * lax.rsqrt(jnp.mean(tot * tot, axis=0, keepdims=True) + NORM_EPS) * og_ref[...]
            o_ref[c] = (y * mot_ref[c].astype(F32)).astype(BF16)
        return m_f, m_b

    n_iter = nc // MLSTM_POSITIONS_PER_BODY
    m0 = jnp.zeros((1, 1), F32)
    carry = lax.fori_loop(0, n_iter // 2, first_touch, (m0, m0))
    lax.fori_loop(n_iter // 2, n_iter, second_touch, carry)


def _merge_mlp_kernel(x_ref, ya_ref, ymt_ref, g1_ref, wgab_ref, wbm_ref, wbl_ref, wout_ref,
                      g2_ref, wup_ref, wdn_ref, gf_ref, o_ref, *, d, final_norm):
    x = x_ref[...]
    hn = _rms(x, g1_ref[...]).astype(BF16)
    gates = jax.nn.sigmoid(_dot(hn, wgab_ref[...]))
    merged = (gates[:, :d] * _dot(ya_ref[...], wbm_ref[...])
              + gates[:, d:] * _dot_tn(ymt_ref[0], wbl_ref[...]))
    x1 = x + _dot(merged.astype(BF16), wout_ref[...])
    u = _dot(_rms(x1, g2_ref[...]).astype(BF16), wup_ref[...])
    r = jnp.maximum(u, 0.0)
    x2 = x1 + _dot((r * r).astype(BF16), wdn_ref[...])
    o_ref[...] = _rms(x2, gf_ref[...]) if final_norm else x2


def _const_spec(shape):
    return pl.BlockSpec(shape, lambda *_: (0,) * len(shape))


def _resident_spec(shape):
    return pl.BlockSpec(shape, lambda *_: (0,) * len(shape), pipeline_mode=pl.Buffered(1))


def _params(n_axes):
    return pltpu.CompilerParams(dimension_semantics=("arbitrary",) * n_axes,
                                vmem_limit_bytes=VMEM_LIMIT_BYTES)


def _rope_kernel(pos_ref, freq_ref, cos_out, sin_out):
    t = pos_ref.shape[1]
    ang = freq_ref[...] * pos_ref[...]
    cos, sin = jnp.cos(ang), jnp.sin(ang)
    pad = jnp.zeros((HEAD_PAD - MLA_NOPE - MLA_ROPE, t), F32)
    cos_out[...] = jnp.concatenate([jnp.ones((MLA_NOPE, t), F32), cos, cos, pad], axis=0)
    sin_out[...] = jnp.concatenate([jnp.zeros((MLA_NOPE, t), F32), -sin, sin, pad], axis=0)


def _rope_tables(positions):
    t = positions.size
    inv_freq = ROPE_THETA ** (-jnp.arange(0, MLA_ROPE, 2, dtype=F32) / MLA_ROPE)
    table = jax.ShapeDtypeStruct((HEAD_PAD, t), F32)
    tile = min(t, ROPE_TABLE_LANES)
    return pl.pallas_call(
        _rope_kernel,
        grid=(t // tile,),
        in_specs=[pl.BlockSpec((1, tile), lambda i: (0, i)), _const_spec((MLA_ROPE // 2, 1))],
        out_specs=[pl.BlockSpec((HEAD_PAD, tile), lambda i: (0, i))] * 2,
        out_shape=[table, table], compiler_params=_params(1), name="rope_table",
    )(positions.astype(F32).reshape(1, t), inv_freq.reshape(-1, 1))


def _pad_heads(w, width_in, offset_out):
    k = w.shape[0]
    w = w.reshape(k, MLA_HEADS, width_in)
    w = jnp.pad(w, ((0, 0), (0, 0), (offset_out, HEAD_PAD - width_in - offset_out)))
    return w.reshape(k, MLA_HEADS * HEAD_PAD)


def kernel(x, positions, norm_mix_g, w_in, mla_q_norm_g, mla_w_uq, mla_kv_norm_g, mla_w_ukv, mlstm_conv_w, mlstm_conv_b, mlstm_igate_b, mlstm_fgate_b, mlstm_out_norm_g, w_branch_mla, w_branch_mlstm, w_out, norm_mlp_g, w_mlp_up, w_mlp_down, norm_final_g):
    B, S, D = x.shape
    T = B * S
    depth = w_in.shape[0]
    q_lora = mla_q_norm_g.shape[1]
    kv_lora = mla_kv_norm_g.shape[1]
    mw = MLSTM_HEADS * MLSTM_DH
    n_gate = 4 * MLSTM_HEADS
    L = MLSTM_CHUNK
    nc = S // L
    tm = tm4 = ROW_TILE
    tq, tk = ATTN_Q_TILE, ATTN_KEY_STEP
    assert S % (2 * MLSTM_POSITIONS_PER_BODY * L) == 0 and MLA_HEADS % 2 == 0
    assert S % tk == 0 and S % tq == 0 and tk % tm == 0 and tm % HALO == 0

    cos_t, sin_t = _rope_tables(positions)
    xf = x.reshape(T, D)

    for l in range(depth):
        offs, o = [], 0
        for w in (q_lora, kv_lora, MLA_ROPE, mw, mw, mw, mw, n_gate, D, D):
            offs.append(o)
            o += w
        wl = w_in[l]
        w_cq = wl[:, offs[0]:offs[0] + q_lora]
        w_ckv = wl[:, offs[1]:offs[1] + kv_lora]
        w_kr = wl[:, offs[2]:offs[2] + MLA_ROPE]
        pad_kr = ((0, 0), (MLA_NOPE, HEAD_PAD - MLA_NOPE - MLA_ROPE))
        w1 = wl[:, offs[3]:offs[3] + 2 * mw].astype(BF16)
        w2t = wl[:, offs[5]:offs[5] + 2 * mw].T.astype(BF16)
        w3 = jnp.concatenate([w_cq, w_ckv, jnp.pad(w_kr, pad_kr)], axis=1).astype(BF16)
        w2t = jnp.concatenate([w2t, wl[:, offs[7]:offs[7] + n_gate].T.astype(BF16)], axis=0)
        wgab = wl[:, offs[8]:offs[8] + 2 * D].astype(BF16)
        gate_b = jnp.stack([mlstm_igate_b[l], mlstm_fgate_b[l]], axis=1).reshape(n_gate, 1)

        wqa = _pad_heads(mla_w_uq[l], MLA_NOPE + MLA_ROPE, 0).astype(BF16)
        wkv = mla_w_ukv[l].reshape(kv_lora, MLA_HEADS, MLA_NOPE + MLA_V)
        wuk = _pad_heads(wkv[:, :, :MLA_NOPE].reshape(kv_lora, -1), MLA_NOPE, 0).astype(BF16)
        wuvt = wkv[:, :, MLA_NOPE:].reshape(kv_lora, MLA_HEADS * MLA_V).T.astype(BF16)

        cw = jnp.pad(mlstm_conv_w[l].reshape(CONV_WIDTH, 2 * mw), ((0, F32_SUBLANES - CONV_WIDTH), (0, 0)))
        cb = mlstm_conv_b[l].reshape(1, 2 * mw)

        tiles_per_seq = S // tm
        hb = tm // HALO
        n_halo_blocks = T // HALO
        row = lambda i: (i, 0)
        col = lambda i: (0, i)
        chunk = lambda i: (i, 0, 0)
        kern = functools.partial(
            _inproj_kernel, tm=tm, tiles_per_seq=tiles_per_seq, q_lora=q_lora, kv_lora=kv_lora, mw=mw,
            q_scale=(MLA_NOPE + MLA_ROPE) ** -0.5 * math.log2(math.e), k_scale=MLSTM_DH ** -0.5)
        outs = pl.pallas_call(
            kern,
            grid=(T // tm,),
            in_specs=[
                pl.BlockSpec((tm, D), row),
                pl.BlockSpec((HALO, D), lambda i: (jnp.maximum(i * hb - 1, 0), 0)),
                pl.BlockSpec((HALO, D), lambda i: (jnp.minimum((i + 1) * hb, n_halo_blocks - 1), 0)),
                pl.BlockSpec((HEAD_PAD, tm), col),
                pl.BlockSpec((HEAD_PAD, tm), col),
                _const_spec((1, D)),
                _const_spec(w1.shape), _const_spec(w2t.shape), _const_spec(w3.shape),
                _const_spec(gate_b.shape),
                _const_spec(cw.shape), _const_spec(cb.shape),
                _const_spec((1, q_lora)), _const_spec(wqa.shape),
                _const_spec((1, kv_lora)), _const_spec(wuk.shape), _const_spec(wuvt.shape),
            ],
            out_specs=[
                pl.BlockSpec((tm, MLA_HEADS * HEAD_PAD), row),
                pl.BlockSpec((tm, MLA_HEADS * HEAD_PAD), row),
                pl.BlockSpec((1, MLA_HEADS * HEAD_PAD, tm), chunk),
                pl.BlockSpec((tm, mw), row), pl.BlockSpec((tm, mw), row),
                pl.BlockSpec((1, mw, tm), chunk),
                pl.BlockSpec((1, mw, tm), chunk),
                pl.BlockSpec((n_gate, tm), col),
            ],
            out_shape=[
                jax.ShapeDtypeStruct((T, MLA_HEADS * HEAD_PAD), BF16),
                jax.ShapeDtypeStruct((T, MLA_HEADS * HEAD_PAD), BF16),
                jax.ShapeDtypeStruct((T // tm, MLA_HEADS * HEAD_PAD, tm), BF16),
                jax.ShapeDtypeStruct((T, mw), BF16), jax.ShapeDtypeStruct((T, mw), BF16),
                jax.ShapeDtypeStruct((T // tm, mw, tm), BF16),
                jax.ShapeDtypeStruct((T // tm, mw, tm), BF16),
                jax.ShapeDtypeStruct((n_gate, T), F32),
            ],
            compiler_params=_params(1),
            name="inproj",
        )(xf, xf, xf, cos_t, sin_t, norm_mix_g[l].reshape(1, D), w1, w2t, w3, gate_b, cw, cb,
          mla_q_norm_g[l].reshape(1, q_lora), wqa, mla_kv_norm_g[l].reshape(1, kv_lora), wuk, wuvt)
        q_a, k_a, v_a, q_m, k_m, v_m, mo_s, gates = outs

        vchunk = tm
        y_attn = pl.pallas_call(
            functools.partial(_attn_kernel, tq=tq, tk=tk, vchunk=vchunk, n_q=S // tq, n_k=S // tk),
            grid=(B, MLA_HEADS // 2),
            in_specs=[
                pl.BlockSpec((S, 2 * HEAD_PAD), lambda b, p: (b, p)),
                pl.BlockSpec((S, 2 * HEAD_PAD), lambda b, p: (b, p)),
                pl.BlockSpec((S // vchunk, 2 * HEAD_PAD, vchunk), lambda b, p: (b, p, 0)),
            ],
            out_specs=pl.BlockSpec((S, 2 * MLA_V), lambda b, p: (b, p)),
            out_shape=jax.ShapeDtypeStruct((T, MLA_HEADS * MLA_V), BF16),
            scratch_shapes=[pltpu.VMEM((2, tk, tq), F32), pltpu.VMEM((2, tk, tq), F32),
                            pltpu.VMEM((2, tk, tq), BF16), pltpu.VMEM((2, tk, tq), BF16),
                            pltpu.VMEM((2, tk, tq), BF16), pltpu.VMEM((2, tk, tq), BF16),
                            pltpu.VMEM((2, HEAD_PAD, tq), F32)],
            compiler_params=_params(2),
            name="mla_attn",
        )(q_a, k_a, v_a)

        gates5 = gates.reshape(2, 2, MLSTM_HEADS, T // L, L)
        head_blk = lambda b, h: (b, h)
        head_blk_t = lambda b, h: (b, h, 0)
        aug = MLSTM_DH + BF16_SUBLANES
        row_scratch = pltpu.VMEM((2, nc, L), F32)
        y_mlstm_t = pl.pallas_call(
            functools.partial(_mlstm_kernel, L=L, nc=nc),
            grid=(B, MLSTM_HEADS),
            in_specs=[
                pl.BlockSpec((S, MLSTM_DH), head_blk), pl.BlockSpec((S, MLSTM_DH), head_blk),
                pl.BlockSpec((nc, MLSTM_DH, L), head_blk_t), pl.BlockSpec((nc, MLSTM_DH, L), head_blk_t),
                pl.BlockSpec((2, 2, 1, nc, L), lambda b, h: (0, 0, h, b, 0)),
                pl.BlockSpec((MLSTM_DH, 1), lambda b, h: (h, 0)),
            ],
            out_specs=pl.BlockSpec((nc, MLSTM_DH, L), head_blk_t),
            out_shape=jax.ShapeDtypeStruct((T // L, mw, L), BF16),
            scratch_shapes=[pltpu.VMEM((nc, MLSTM_DH, L), F32),
                            pltpu.VMEM((2, aug, MLSTM_DH), F32),
                            row_scratch, row_scratch, row_scratch,
                            pltpu.VMEM((2, nc, 1), F32), pltpu.VMEM((2, nc, 1), F32)],
            compiler_params=_params(2),
            name="mlstm",
        )(q_m, k_m, v_m, mo_s, gates5, mlstm_out_norm_g[l].reshape(mw, 1))

        last = l == depth - 1
        gf = norm_final_g.reshape(1, D)
        weights4 = (wgab, w_branch_mla[l].astype(BF16), w_branch_mlstm[l].astype(BF16),
                    w_out[l].astype(BF16), w_mlp_up[l].astype(BF16), w_mlp_down[l].astype(BF16))
        xf = pl.pallas_call(
            functools.partial(_merge_mlp_kernel, d=D, final_norm=last),
            grid=(T // tm4,),
            in_specs=[
                pl.BlockSpec((tm4, D), row),
                pl.BlockSpec((tm4, MLA_HEADS * MLA_V), row),
                pl.BlockSpec((1, mw, tm4), lambda i: (i, 0, 0)),
                _const_spec((1, D)),
                _resident_spec(weights4[0].shape), _resident_spec(weights4[1].shape),
                _resident_spec(weights4[2].shape), _resident_spec(weights4[3].shape),
                _const_spec((1, D)),
                _resident_spec(weights4[4].shape), _resident_spec(weights4[5].shape),
                _const_spec((1, D)),
            ],
            out_specs=pl.BlockSpec((tm4, D), row),
            out_shape=jax.ShapeDtypeStruct((T, D), F32),
            compiler_params=_params(1),
            name="merge_mlp",
        )(xf, y_attn, y_mlstm_t, norm_mix_g[l].reshape(1, D), weights4[0], weights4[1], weights4[2],
          weights4[3], norm_mlp_g[l].reshape(1, D), weights4[4], weights4[5], gf)

    return xf.reshape(B, S, D)
```

```python
import functools
import math

import jax
import jax.numpy as jnp
from jax import lax
from jax.experimental import pallas as pl
from jax.experimental.pallas import tpu as pltpu

MLA_HEADS = 8
MLA_NOPE = 64
MLA_ROPE = 32
MLA_V = 64
ROPE_THETA = 10000.0
MLSTM_HEADS = 4
MLSTM_DH = 128
CONV_WIDTH = 5
NORM_EPS = 1e-6

LANES = 128
F32_SUBLANES = 8
BF16_SUBLANES = 16
VMEM_LIMIT_BYTES = 56 * 1024 * 1024

MXU_WIDTH = 256

HEAD_PAD = LANES
MLSTM_CHUNK = MXU_WIDTH
MLSTM_POSITIONS_PER_BODY = 4
ROW_TILE = MLSTM_CHUNK
ATTN_Q_TILE = MXU_WIDTH
ATTN_KEY_STEP = 4 * MXU_WIDTH
ATTN_BODIES_PER_ITER = 16
HALO = BF16_SUBLANES
ROPE_TABLE_LANES = 16 * LANES

F32 = jnp.float32
BF16 = jnp.bfloat16


def _rms(x, g):
    return x * lax.rsqrt(jnp.mean(x * x, axis=-1, keepdims=True) + NORM_EPS) * g


def _dot(a, b):
    return jnp.dot(a, b, preferred_element_type=F32)


def _dot_nt(a, b):
    return lax.dot_general(a, b, (((1,), (1,)), ((), ())), preferred_element_type=F32)


def _dot_tn(a, b):
    return lax.dot_general(a, b, (((0,), (0,)), ((), ())), preferred_element_type=F32)


def _log_sigmoid(x):
    return jnp.minimum(x, 0.0) - jnp.log1p(jnp.exp(-jnp.abs(x)))


def _inproj_kernel(xm_ref, xp_ref, xn_ref, cos_ref, sin_ref, g_ref, w1_ref, w2t_ref, w3_ref,
                   gb_ref, cw_ref, cb_ref, qg_ref, wqa_ref, kvg_ref,
                   wuk_ref, wuvt_ref,
                   q_out, k_out, va_out, qm_out, km_out, vm_out, mo_out, gate_out,
                   *, tm, tiles_per_seq, q_lora, kv_lora, mw, q_scale, k_scale):
    i = pl.program_id(0)
    pos_in_seq = i % tiles_per_seq
    xp = jnp.where(pos_in_seq == 0, 0.0, xp_ref[...])
    xn = jnp.where(pos_in_seq == tiles_per_seq - 1, 0.0, xn_ref[...])
    xe = jnp.concatenate([xp, xm_ref[...], xn], axis=0)
    he = _rms(xe, g_ref[...]).astype(BF16)
    hm = he[HALO:HALO + tm]

    pre = _dot(he, w1_ref[...])
    rows = pre.shape[0]
    conv = cb_ref[...]
    for j in range(CONV_WIDTH):
        shifted = pre if j == CONV_WIDTH // 2 else pltpu.roll(pre, (CONV_WIDTH // 2 - j) % rows, 0)
        conv = conv + cw_ref[j:j + 1, :] * shifted[HALO:HALO + tm]
    qk = conv * jax.nn.sigmoid(conv)
    qm_out[...] = qk[:, :mw].astype(BF16)
    km_out[...] = (qk[:, mw:] * k_scale).astype(BF16)

    vo_t = _dot_nt(w2t_ref[...], hm)
    vm_out[0] = vo_t[:mw].astype(BF16)
    mo_out[0] = jax.nn.sigmoid(vo_t[mw:2 * mw]).astype(BF16)

    gt = vo_t[2 * mw:] + gb_ref[...]
    row = lax.broadcasted_iota(jnp.int32, gt.shape, 0)
    is_f = (row % (2 * MLSTM_HEADS)) >= MLSTM_HEADS
    gate_out[...] = jnp.where(is_f, _log_sigmoid(gt), gt)

    c = _dot(hm, w3_ref[...])
    cqn = _rms(c[:, :q_lora], qg_ref[...]).astype(BF16)
    ckvn = _rms(c[:, q_lora:q_lora + kv_lora], kvg_ref[...]).astype(BF16)
    cos = cos_ref[...].T
    sin = sin_ref[...].T
    half = MLA_ROPE // 2
    lane = lax.broadcasted_iota(jnp.int32, sin.shape, 1)
    sin_lo = jnp.where(lane < MLA_NOPE + half, sin, 0.0)
    sin_hi = sin - sin_lo

    def rope(x):
        return (x * cos + pltpu.roll(x, HEAD_PAD - half, 1) * sin_lo + pltpu.roll(x, half, 1) * sin_hi)

    qa = _dot(cqn, wqa_ref[...])
    q = jnp.concatenate([rope(qa[:, h * HEAD_PAD:(h + 1) * HEAD_PAD]) for h in range(MLA_HEADS)], axis=1)
    q_out[...] = (q * q_scale).astype(BF16)
    kr = rope(c[:, q_lora + kv_lora:])
    k = _dot(ckvn, wuk_ref[...]) + jnp.concatenate([kr] * MLA_HEADS, axis=1)
    k_out[...] = k.astype(BF16)
    vt = _dot_nt(wuvt_ref[...], ckvn)
    ones = jnp.ones((HEAD_PAD - MLA_V, tm), F32)
    pieces = []
    for h in range(MLA_HEADS):
        pieces += [vt[h * MLA_V:(h + 1) * MLA_V], ones]
    va_out[0] = jnp.concatenate(pieces, axis=0).astype(BF16)


def _aligned(x, m):
    return x if isinstance(x, int) else pl.multiple_of(x, m)


def _attn_kernel(q_ref, k_ref, vt_ref, o_ref, s_0, s_1, p_0, p_1, p_2, p_3, acc_scr, *,
                 tq, tk, vchunk, n_q, n_k):
    s_buf, p_buf = (s_0, s_1), (p_0, p_1, p_2, p_3)
    n_slots = len(p_buf)
    bodies = ATTN_BODIES_PER_ITER
    sub = tk // vchunk
    n_total = n_q * n_k
    assert bodies % n_slots == 0 and (n_k % bodies == 0 or bodies % n_k == 0)
    assert n_slots % 2 == 0 and n_total >= 6

    def scores(j, par):
        qoff = _aligned((j // n_k) * tq, tq)
        koff = _aligned((j % n_k) * tk, tk)
        cmax = []
        for hh in range(2):
            cols = slice(hh * HEAD_PAD, (hh + 1) * HEAD_PAD)
            s = _dot_nt(k_ref[pl.ds(koff, tk), cols], q_ref[pl.ds(qoff, tq), cols])
            s_buf[par][hh] = s
            cmax.append(jnp.max(s, axis=0, keepdims=True))
        return tuple(cmax)

    def pv_issue(j, slot):
        kb = j % n_k
        out = []
        for hh in range(2):
            vt = jnp.concatenate([vt_ref[kb * sub + c, hh * HEAD_PAD:(hh + 1) * HEAD_PAD, :]
                                  for c in range(sub)], axis=1)
            out.append(_dot(vt, p_buf[slot][hh]))
        return out

    def pv_accumulate(r, alpha):
        for hh in range(2):
            acc_scr[hh] = alpha[hh] * acc_scr[hh] + r[hh]

    def softmax(j, par, slot, m, cmax):
        first = (j % n_k) == 0
        m_out, alpha = [], []
        for hh in range(2):
            m_old = jnp.where(first, -jnp.inf, m[hh])
            m_new = jnp.maximum(m_old, cmax[hh])
            p_buf[slot][hh] = jnp.exp2(s_buf[par][hh] - m_new).astype(BF16)
            alpha.append(jnp.exp2(m_old - m_new))
            m_out.append(m_new)
        return tuple(m_out), tuple(alpha)

    def finalize(j):
        qoff = _aligned((j // n_k) * tq, tq)
        o_t = jnp.concatenate([acc_scr[hh, :MLA_V] / acc_scr[hh, MLA_V:MLA_V + 1] for hh in range(2)],
                              axis=0)
        o_ref[pl.ds(qoff, tq), :] = o_t.T.astype(BF16)

    def pv(j, alpha, r=None):
        pv_accumulate(pv_issue(j, j % n_slots) if r is None else r, alpha)
        if isinstance(j, int) and j % n_k == n_k - 1:
            finalize(j)

    def body(k, kmod, m, alpha_prev, alpha_cur, cmax):
        par = kmod % 2
        r = pv_issue(k - 1, (kmod - 1) % n_slots)
        cmax_next = scores(k + 2, par)
        m, alpha_next = softmax(k + 1, 1 - par, (kmod + 1) % n_slots, m, cmax)
        pv(k - 1, alpha_prev, r)
        return m, alpha_cur, alpha_next, cmax_next

    def loop_body(kk, carry):
        k0 = bodies * kk + 1
        for o in range(bodies):
            carry = body(k0 + o, (1 + o) % n_slots, *carry)
            if bodies % n_k == 0:
                if o % n_k == n_k - 1:
                    finalize(k0 + o - 1)
            elif o == bodies - 1:
                pl.when((k0 + o - 1) % n_k == n_k - 1)(functools.partial(finalize, k0 + o - 1))
        return carry

    acc_scr[...] = jnp.zeros_like(acc_scr)
    m = tuple(jnp.full((1, tq), -jnp.inf, F32) for _ in range(2))
    cmax0 = scores(0, 0)
    cmax1 = scores(1, 1)
    m, alpha0 = softmax(0, 0, 0, m, cmax0)
    cmax2 = scores(2, 0)
    m, alpha1 = softmax(1, 1, 1, m, cmax1)
    n_iter = (n_total - 3) // bodies
    carry = lax.fori_loop(0, n_iter, loop_body, (m, alpha0, alpha1, cmax2))
    k = bodies * n_iter + 1
    while k <= n_total - 3:
        carry = body(k, k % n_slots, *carry)
        k += 1
    m, alpha_prev, alpha_cur, cmax = carry
    r = pv_issue(k - 1, (k - 1) % n_slots)
    m, alpha_last = softmax(k + 1, (k + 1) % 2, (k + 1) % n_slots, m, cmax)
    pv(k - 1, alpha_prev, r)
    pv(n_total - 2, alpha_cur)
    pv(n_total - 1, alpha_last)


def _scan_lanes(x, op, fill, reverse):
    n = x.shape[1]
    lane = lax.broadcasted_iota(jnp.int32, x.shape, 1)
    shift = 1
    while shift < n:
        if reverse:
            moved, valid = pltpu.roll(x, n - shift, 1), lane < n - shift
        else:
            moved, valid = pltpu.roll(x, shift, 1), lane >= shift
        x = op(x, jnp.where(valid, moved, fill))
        shift *= 2
    return x


class _Chain:
    pass


def _mlstm_kernel(q_ref, k_ref, vt_ref, mot_ref, gate_ref, og_ref, o_ref,
                  hacc, st_scr, b_scr, a_scr, cm_scr, bl_scr, am_scr, *, L, nc):
    dh = MLSTM_DH
    aug = st_scr.shape[1]
    s_idx = lax.broadcasted_iota(jnp.int32, (L, L), 0)
    t_idx = lax.broadcasted_iota(jnp.int32, (L, L), 1)
    eye = s_idx == t_idx
    visible = (s_idx <= t_idx, s_idx >= t_idx)
    ones_blk = (lax.broadcasted_iota(jnp.int32, (aug - dh, L), 0) == 0).astype(BF16)

    for d in range(2):
        logi = gate_ref[d, 0, 0]
        logf = gate_ref[d, 1, 0]
        b = _scan_lanes(logf, jnp.add, 0.0, reverse=d == 1)
        a = logi - b
        b_scr[d] = b
        a_scr[d] = a
        cm_scr[d] = _scan_lanes(a, jnp.maximum, -jnp.inf, reverse=d == 1)
        bl_scr[d] = jnp.sum(logf, axis=1, keepdims=True)
        am_scr[d] = jnp.max(a, axis=1, keepdims=True)
    st_scr[...] = jnp.zeros_like(st_scr)

    def prepare(d, c, m):
        x = _Chain()
        off = pl.multiple_of(c * L, L)
        x.q = q_ref[pl.ds(off, L), :]
        x.k = k_ref[pl.ds(off, L), :]
        x.vaug = jnp.concatenate([vt_ref[c], ones_blk], axis=0)
        x.b_row = b_scr[d, pl.ds(c, 1), :]
        x.a_row = a_scr[d, pl.ds(c, 1), :]
        b_last = bl_scr[d, pl.ds(c, 1), :]
        x.g_row = jnp.maximum(cm_scr[d, pl.ds(c, 1), :], m)
        x.iw = jnp.exp(m - x.g_row)
        x.m_new = jnp.maximum(b_last + m, b_last + am_scr[d, pl.ds(c, 1), :])
        x.decay = jnp.exp(b_last + m - x.m_new)
        x.w_row = jnp.exp(b_last + x.a_row - x.m_new)
        return x

    def intra(d, x):
        a_col = jnp.sum(jnp.where(eye, x.a_row, 0.0), axis=1, keepdims=True)
        e_t = jnp.exp(jnp.where(visible[d], a_col - x.g_row, -jnp.inf))
        x.r = _dot(x.vaug, (x.st * e_t).astype(BF16))

    def finish(x):
        num = x.iw * x.inter[:dh] + x.r[:dh]
        den = x.iw * x.inter[dh:dh + 1] + x.r[dh:dh + 1]
        floor = jnp.exp(-(x.b_row + x.g_row))
        return num * (1.0 / jnp.maximum(jnp.abs(den), floor))

    def run(j0, m_f, m_b):
        chains = []
        for o in range(MLSTM_POSITIONS_PER_BODY):
            j = j0 + o
            xf = prepare(0, j, m_f)
            xb = prepare(1, nc - 1 - j, m_b)
            m_f, m_b = xf.m_new, xb.m_new
            chains += [(0, j, xf), (1, nc - 1 - j, xb)]
        for d, _, x in chains:
            x.st = _dot_nt(x.k, x.q)
            x.upd = _dot((x.vaug.astype(F32) * x.w_row).astype(BF16), x.k)
        for d, _, x in chains:
            x.inter = _dot_nt(st_scr[d].astype(BF16), x.q)
            st_scr[d] = x.decay * st_scr[d] + x.upd
        for d, _, x in chains:
            intra(d, x)
        return [(c, finish(x)) for _, c, x in chains], m_f, m_b

    def first_touch(jj, carry):
        outs, m_f, m_b = run(jj * MLSTM_POSITIONS_PER_BODY, *carry)
        for c, h in outs:
            hacc[c] = h
        return m_f, m_b

    def second_touch(jj, carry):
        outs, m_f, m_b = run(jj * MLSTM_POSITIONS_PER_BODY, *carry)
        for c, h in outs:
            tot = hacc[c] + h
            y = tot * lax.rsqrt(jnp.mean(tot * tot, axis=0, keepdims=True) + NORM_EPS) * og_ref[...]
            o_ref[c] = (y * mot_ref[c].astype(F32)).astype(BF16)
        return m_f, m_b

    n_iter = nc // MLSTM_POSITIONS_PER_BODY
    m0 = jnp.zeros((1, 1), F32)
    carry = lax.fori_loop(0, n_iter // 2, first_touch, (m0, m0))
    lax.fori_loop(n_iter // 2, n_iter, second_touch, carry)


def _merge_mlp_kernel(x_ref, ya_ref, ymt_ref, g1_ref, wgab_ref, wbm_ref, wbl_ref, wout_ref,
                      g2_ref, wup_ref, wdn_ref, gf_ref, o_ref, *, d, final_norm):
    x = x_ref[...]
    hn = _rms(x, g1_ref[...]).astype(BF16)
    gates = jax.nn.sigmoid(_dot(hn, wgab_ref[...]))
    merged = (gates[:, :d] * _dot(ya_ref[...], wbm_ref[...])
              + gates[:, d:] * _dot_tn(ymt_ref[0], wbl_ref[...]))
    x1 = x + _dot(merged.astype(BF16), wout_ref[...])
    u = _dot(_rms(x1, g2_ref[...]).astype(BF16), wup_ref[...])
    r = jnp.maximum(u, 0.0)
    x2 = x1 + _dot((r * r).astype(BF16), wdn_ref[...])
    o_ref[...] = _rms(x2, gf_ref[...]) if final_norm else x2


def _const_spec(shape):
    return pl.BlockSpec(shape, lambda *_: (0,) * len(shape))


def _resident_spec(shape):
    return pl.BlockSpec(shape, lambda *_: (0,) * len(shape), pipeline_mode=pl.Buffered(1))


def _params(n_axes):
    return pltpu.CompilerParams(dimension_semantics=("arbitrary",) * n_axes,
                                vmem_limit_bytes=VMEM_LIMIT_BYTES)


def _rope_kernel(pos_ref, freq_ref, cos_out, sin_out):
    t = pos_ref.shape[1]
    ang = freq_ref[...] * pos_ref[...]
    cos, sin = jnp.cos(ang), jnp.sin(ang)
    pad = jnp.zeros((HEAD_PAD - MLA_NOPE - MLA_ROPE, t), F32)
    cos_out[...] = jnp.concatenate([jnp.ones((MLA_NOPE, t), F32), cos, cos, pad], axis=0)
    sin_out[...] = jnp.concatenate([jnp.zeros((MLA_NOPE, t), F32), -sin, sin, pad], axis=0)


def _rope_tables(positions):
    t = positions.size
    inv_freq = ROPE_THETA ** (-jnp.arange(0, MLA_ROPE, 2, dtype=F32) / MLA_ROPE)
    table = jax.ShapeDtypeStruct((HEAD_PAD, t), F32)
    tile = min(t, ROPE_TABLE_LANES)
    return pl.pallas_call(
        _rope_kernel,
        grid=(t // tile,),
        in_specs=[pl.BlockSpec((1, tile), lambda i: (0, i)), _const_spec((MLA_ROPE // 2, 1))],
        out_specs=[pl.BlockSpec((HEAD_PAD, tile), lambda i: (0, i))] * 2,
        out_shape=[table, table], compiler_params=_params(1), name="rope_table",
    )(positions.astype(F32).reshape(1, t), inv_freq.reshape(-1, 1))


def _pad_heads(w, width_in, offset_out):
    k = w.shape[0]
    w = w.reshape(k, MLA_HEADS, width_in)
    w = jnp.pad(w, ((0, 0), (0, 0), (offset_out, HEAD_PAD - width_in - offset_out)))
    return w.reshape(k, MLA_HEADS * HEAD_PAD)


def kernel(x, positions, norm_mix_g, w_in, mla_q_norm_g, mla_w_uq, mla_kv_norm_g, mla_w_ukv, mlstm_conv_w, mlstm_conv_b, mlstm_igate_b, mlstm_fgate_b, mlstm_out_norm_g, w_branch_mla, w_branch_mlstm, w_out, norm_mlp_g, w_mlp_up, w_mlp_down, norm_final_g):
    B, S, D = x.shape
    T = B * S
    depth = w_in.shape[0]
    q_lora = mla_q_norm_g.shape[1]
    kv_lora = mla_kv_norm_g.shape[1]
    mw = MLSTM_HEADS * MLSTM_DH
    n_gate = 4 * MLSTM_HEADS
    L = MLSTM_CHUNK
    nc = S // L
    tm = tm4 = ROW_TILE
    tq, tk = ATTN_Q_TILE, ATTN_KEY_STEP
    assert S % (2 * MLSTM_POSITIONS_PER_BODY * L) == 0 and MLA_HEADS % 2 == 0
    assert S % tk == 0 and S % tq == 0 and tk % tm == 0 and tm % HALO == 0

    cos_t, sin_t = _rope_tables(positions)
    xf = x.reshape(T, D)

    for l in range(depth):
        offs, o = [], 0
        for w in (q_lora, kv_lora, MLA_ROPE, mw, mw, mw, mw, n_gate, D, D):
            offs.append(o)
            o += w
        wl = w_in[l]
        w_cq = wl[:, offs[0]:offs[0] + q_lora]
        w_ckv = wl[:, offs[1]:offs[1] + kv_lora]
        w_kr = wl[:, offs[2]:offs[2] + MLA_ROPE]
        pad_kr = ((0, 0), (MLA_NOPE, HEAD_PAD - MLA_NOPE - MLA_ROPE))
        w1 = wl[:, offs[3]:offs[3] + 2 * mw].astype(BF16)
        w2t = wl[:, offs[5]:offs[5] + 2 * mw].T.astype(BF16)
        w3 = jnp.concatenate([w_cq, w_ckv, jnp.pad(w_kr, pad_kr)], axis=1).astype(BF16)
        w2t = jnp.concatenate([w2t, wl[:, offs[7]:offs[7] + n_gate].T.astype(BF16)], axis=0)
        wgab = wl[:, offs[8]:offs[8] + 2 * D].astype(BF16)
        gate_b = jnp.stack([mlstm_igate_b[l], mlstm_fgate_b[l]], axis=1).reshape(n_gate, 1)

        wqa = _pad_heads(mla_w_uq[l], MLA_NOPE + MLA_ROPE, 0).astype(BF16)
        wkv = mla_w_ukv[l].reshape(kv_lora, MLA_HEADS, MLA_NOPE + MLA_V)
        wuk = _pad_heads(wkv[:, :, :MLA_NOPE].reshape(kv_lora, -1), MLA_NOPE, 0).astype(BF16)
        wuvt = wkv[:, :, MLA_NOPE:].reshape(kv_lora, MLA_HEADS * MLA_V).T.astype(BF16)

        cw = jnp.pad(mlstm_conv_w[l].reshape(CONV_WIDTH, 2 * mw), ((0, F32_SUBLANES - CONV_WIDTH), (0, 0)))
        cb = mlstm_conv_b[l].reshape(1, 2 * mw)

        tiles_per_seq = S // tm
        hb = tm // HALO
        n_halo_blocks = T // HALO
        row = lambda i: (i, 0)
        col = lambda i: (0, i)
        chunk = lambda i: (i, 0, 0)
        kern = functools.partial(
            _inproj_kernel, tm=tm, tiles_per_seq=tiles_per_seq, q_lora=q_lora, kv_lora=kv_lora, mw=mw,
            q_scale=(MLA_NOPE + MLA_ROPE) ** -0.5 * math.log2(math.e), k_scale=MLSTM_DH ** -0.5)
        outs = pl.pallas_call(
            kern,
            grid=(T // tm,),
            in_specs=[
                pl.BlockSpec((tm, D), row),
                pl.BlockSpec((HALO, D), lambda i: (jnp.maximum(i * hb - 1, 0), 0)),
                pl.BlockSpec((HALO, D), lambda i: (jnp.minimum((i + 1) * hb, n_halo_blocks - 1), 0)),
                pl.BlockSpec((HEAD_PAD, tm), col),
                pl.BlockSpec((HEAD_PAD, tm), col),
                _const_spec((1, D)),
                _const_spec(w1.shape), _const_spec(w2t.shape), _const_spec(w3.shape),
                _const_spec(gate_b.shape),
                _const_spec(cw.shape), _const_spec(cb.shape),
                _const_spec((1, q_lora)), _const_spec(wqa.shape),
                _const_spec((1, kv_lora)), _const_spec(wuk.shape), _const_spec(wuvt.shape),
            ],
            out_specs=[
                pl.BlockSpec((tm, MLA_HEADS * HEAD_PAD), row),
                pl.BlockSpec((tm, MLA_HEADS * HEAD_PAD), row),
                pl.BlockSpec((1, MLA_HEADS * HEAD_PAD, tm), chunk),
                pl.BlockSpec((tm, mw), row), pl.BlockSpec((tm, mw), row),
                pl.BlockSpec((1, mw, tm), chunk),
                pl.BlockSpec((1, mw, tm), chunk),
                pl.BlockSpec((n_gate, tm), col),
            ],
            out_shape=[
                jax.ShapeDtypeStruct((T, MLA_HEADS * HEAD_PAD), BF16),
                jax.ShapeDtypeStruct((T, MLA_HEADS * HEAD_PAD), BF16),
                jax.ShapeDtypeStruct((T // tm, MLA_HEADS * HEAD_PAD, tm), BF16),
                jax.ShapeDtypeStruct((T, mw), BF16), jax.ShapeDtypeStruct((T, mw), BF16),
                jax.ShapeDtypeStruct((T // tm, mw, tm), BF16),
                jax.ShapeDtypeStruct((T // tm, mw, tm), BF16),
                jax.ShapeDtypeStruct((n_gate, T), F32),
            ],
            compiler_params=_params(1),
            name="inproj",
        )(xf, xf, xf, cos_t, sin_t, norm_mix_g[l].reshape(1, D), w1, w2t, w3, gate_b, cw, cb,
          mla_q_norm_g[l].reshape(1, q_lora), wqa, mla_kv_norm_g[l].reshape(1, kv_lora), wuk, wuvt)
        q_a, k_a, v_a, q_m, k_m, v_m, mo_s, gates = outs

        vchunk = tm
        y_attn = pl.pallas_call(
            functools.partial(_attn_kernel, tq=tq, tk=tk, vchunk=vchunk, n_q=S // tq, n_k=S // tk),
            grid=(B, MLA_HEADS // 2),
            in_specs=[
                pl.BlockSpec((S, 2 * HEAD_PAD), lambda b, p: (b, p)),
                pl.BlockSpec((S, 2 * HEAD_PAD), lambda b, p: (b, p)),
                pl.BlockSpec((S // vchunk, 2 * HEAD_PAD, vchunk), lambda b, p: (b, p, 0)),
            ],
            out_specs=pl.BlockSpec((S, 2 * MLA_V), lambda b, p: (b, p)),
            out_shape=jax.ShapeDtypeStruct((T, MLA_HEADS * MLA_V), BF16),
            scratch_shapes=[pltpu.VMEM((2, tk, tq), F32), pltpu.VMEM((2, tk, tq), F32),
                            pltpu.VMEM((2, tk, tq), BF16), pltpu.VMEM((2, tk, tq), BF16),
                            pltpu.VMEM((2, tk, tq), BF16), pltpu.VMEM((2, tk, tq), BF16),
                            pltpu.VMEM((2, HEAD_PAD, tq), F32)],
            compiler_params=_params(2),
            name="mla_attn",
        )(q_a, k_a, v_a)

        gates5 = gates.reshape(2, 2, MLSTM_HEADS, T // L, L)
        head_blk = lambda b, h: (b, h)
        head_blk_t = lambda b, h: (b, h, 0)
        aug = MLSTM_DH + BF16_SUBLANES
        row_scratch = pltpu.VMEM((2, nc, L), F32)
        y_mlstm_t = pl.pallas_call(
            functools.partial(_mlstm_kernel, L=L, nc=nc),
            grid=(B, MLSTM_HEADS),
            in_specs=[
                pl.BlockSpec((S, MLSTM_DH), head_blk), pl.BlockSpec((S, MLSTM_DH), head_blk),
                pl.BlockSpec((nc, MLSTM_DH, L), head_blk_t), pl.BlockSpec((nc, MLSTM_DH, L), head_blk_t),
                pl.BlockSpec((2, 2, 1, nc, L), lambda b, h: (0, 0, h, b, 0)),
                pl.BlockSpec((MLSTM_DH, 1), lambda b, h: (h, 0)),
            ],
            out_specs=pl.BlockSpec((nc, MLSTM_DH, L), head_blk_t),
            out_shape=jax.ShapeDtypeStruct((T // L, mw, L), BF16),
            scratch_shapes=[pltpu.VMEM((nc, MLSTM_DH, L), F32),
                            pltpu.VMEM((2, aug, MLSTM_DH), F32),
                            row_scratch, row_scratch, row_scratch,
                            pltpu.VMEM((2, nc, 1), F32), pltpu.VMEM((2, nc, 1), F32)],
            compiler_params=_params(2),
            name="mlstm",
        )(q_m, k_m, v_m, mo_s, gates5, mlstm_out_norm_g[l].reshape(mw, 1))

        last = l == depth - 1
        gf = norm_final_g.reshape(1, D)
        weights4 = (wgab, w_branch_mla[l].astype(BF16), w_branch_mlstm[l].astype(BF16),
                    w_out[l].astype(BF16), w_mlp_up[l].astype(BF16), w_mlp_down[l].astype(BF16))
        xf = pl.pallas_call(
            functools.partial(_merge_mlp_kernel, d=D, final_norm=last),
            grid=(T // tm4,),
            in_specs=[
                pl.BlockSpec((tm4, D), row),
                pl.BlockSpec((tm4, MLA_HEADS * MLA_V), row),
                pl.BlockSpec((1, mw, tm4), lambda i: (i, 0, 0)),
                _const_spec((1, D)),
                _resident_spec(weights4[0].shape), _resident_spec(weights4[1].shape),
                _resident_spec(weights4[2].shape), _resident_spec(weights4[3].shape),
                _const_spec((1, D)),
                _resident_spec(weights4[4].shape), _resident_spec(weights4[5].shape),
                _const_spec((1, D)),
            ],
            out_specs=pl.BlockSpec((tm4, D), row),
            out_shape=jax.ShapeDtypeStruct((T, D), F32),
            compiler_params=_params(1),
            name="merge_mlp",
        )(xf, y_attn, y_mlstm_t, norm_mix_g[l].reshape(1, D), weights4[0], weights4[1], weights4[2],
          weights4[3], norm_mlp_g[l].reshape(1, D), weights4[4], weights4[5], gf)

    return xf.reshape(B, S, D)
```

```python
import functools
import math

import jax
import jax.numpy as jnp
from jax import lax
from jax.experimental import pallas as pl
from jax.experimental.pallas import tpu as pltpu

MLA_HEADS = 8
MLA_NOPE = 64
MLA_ROPE = 32
MLA_V = 64
ROPE_THETA = 10000.0
MLSTM_HEADS = 4
MLSTM_DH = 128
CONV_WIDTH = 5
NORM_EPS = 1e-6

LANES = 128
F32_SUBLANES = 8
BF16_SUBLANES = 16
VMEM_LIMIT_BYTES = 56 * 1024 * 1024

MXU_WIDTH = 256

HEAD_PAD = LANES
MLSTM_CHUNK = MXU_WIDTH
MLSTM_POSITIONS_PER_BODY = 4
ROW_TILE = MLSTM_CHUNK
ATTN_Q_TILE = MXU_WIDTH
ATTN_KEY_STEP = 4 * MXU_WIDTH
ATTN_BODIES_PER_ITER = 32
HALO = BF16_SUBLANES
ROPE_TABLE_LANES = 16 * LANES

F32 = jnp.float32
BF16 = jnp.bfloat16


def _rms(x, g):
    return x * lax.rsqrt(jnp.mean(x * x, axis=-1, keepdims=True) + NORM_EPS) * g


def _dot(a, b):
    return jnp.dot(a, b, preferred_element_type=F32)


def _dot_nt(a, b):
    return lax.dot_general(a, b, (((1,), (1,)), ((), ())), preferred_element_type=F32)


def _dot_tn(a, b):
    return lax.dot_general(a, b, (((0,), (0,)), ((), ())), preferred_element_type=F32)


def _log_sigmoid(x):
    return jnp.minimum(x, 0.0) - jnp.log1p(jnp.exp(-jnp.abs(x)))


def _inproj_kernel(xm_ref, xp_ref, xn_ref, cos_ref, sin_ref, g_ref, w1_ref, w2t_ref, w3_ref,
                   gb_ref, cw_ref, cb_ref, qg_ref, wqa_ref, kvg_ref,
                   wuk_ref, wuvt_ref,
                   q_out, k_out, va_out, qm_out, km_out, vm_out, mo_out, gate_out,
                   *, tm, tiles_per_seq, q_lora, kv_lora, mw, q_scale, k_scale):
    i = pl.program_id(0)
    pos_in_seq = i % tiles_per_seq
    xp = jnp.where(pos_in_seq == 0, 0.0, xp_ref[...])
    xn = jnp.where(pos_in_seq == tiles_per_seq - 1, 0.0, xn_ref[...])
    xe = jnp.concatenate([xp, xm_ref[...], xn], axis=0)
    he = _rms(xe, g_ref[...]).astype(BF16)
    hm = he[HALO:HALO + tm]

    pre = _dot(he, w1_ref[...])
    rows = pre.shape[0]
    conv = cb_ref[...]
    for j in range(CONV_WIDTH):
        shifted = pre if j == CONV_WIDTH // 2 else pltpu.roll(pre, (CONV_WIDTH // 2 - j) % rows, 0)
        conv = conv + cw_ref[j:j + 1, :] * shifted[HALO:HALO + tm]
    qk = conv * jax.nn.sigmoid(conv)
    qm_out[...] = qk[:, :mw].astype(BF16)
    km_out[...] = (qk[:, mw:] * k_scale).astype(BF16)

    vo_t = _dot_nt(w2t_ref[...], hm)
    vm_out[0] = vo_t[:mw].astype(BF16)
    mo_out[0] = jax.nn.sigmoid(vo_t[mw:2 * mw]).astype(BF16)

    gt = vo_t[2 * mw:] + gb_ref[...]
    row = lax.broadcasted_iota(jnp.int32, gt.shape, 0)
    is_f = (row % (2 * MLSTM_HEADS)) >= MLSTM_HEADS
    gate_out[...] = jnp.where(is_f, _log_sigmoid(gt), gt)

    c = _dot(hm, w3_ref[...])
    cqn = _rms(c[:, :q_lora], qg_ref[...]).astype(BF16)
    ckvn = _rms(c[:, q_lora:q_lora + kv_lora], kvg_ref[...]).astype(BF16)
    cos = cos_ref[...].T
    sin = sin_ref[...].T
    half = MLA_ROPE // 2
    lane = lax.broadcasted_iota(jnp.int32, sin.shape, 1)
    sin_lo = jnp.where(lane < MLA_NOPE + half, sin, 0.0)
    sin_hi = sin - sin_lo

    def rope(x):
        return (x * cos + pltpu.roll(x, HEAD_PAD - half, 1) * sin_lo + pltpu.roll(x, half, 1) * sin_hi)

    qa = _dot(cqn, wqa_ref[...])
    q = jnp.concatenate([rope(qa[:, h * HEAD_PAD:(h + 1) * HEAD_PAD]) for h in range(MLA_HEADS)], axis=1)
    q_out[...] = (q * q_scale).astype(BF16)
    kr = rope(c[:, q_lora + kv_lora:])
    k = _dot(ckvn, wuk_ref[...]) + jnp.concatenate([kr] * MLA_HEADS, axis=1)
    k_out[...] = k.astype(BF16)
    vt = _dot_nt(wuvt_ref[...], ckvn)
    ones = jnp.ones((HEAD_PAD - MLA_V, tm), F32)
    pieces = []
    for h in range(MLA_HEADS):
        pieces += [vt[h * MLA_V:(h + 1) * MLA_V], ones]
    va_out[0] = jnp.concatenate(pieces, axis=0).astype(BF16)


def _aligned(x, m):
    return x if isinstance(x, int) else pl.multiple_of(x, m)


def _attn_kernel(q_ref, k_ref, vt_ref, o_ref, s_0, s_1, p_0, p_1, p_2, p_3, acc_scr, *,
                 tq, tk, vchunk, n_q, n_k):
    s_buf, p_buf = (s_0, s_1), (p_0, p_1, p_2, p_3)
    n_slots = len(p_buf)
    bodies = ATTN_BODIES_PER_ITER
    sub = tk // vchunk
    n_total = n_q * n_k
    assert bodies % n_slots == 0 and (n_k % bodies == 0 or bodies % n_k == 0)
    assert n_slots % 2 == 0 and n_total >= 6

    def scores(j, par):
        qoff = _aligned((j // n_k) * tq, tq)
        koff = _aligned((j % n_k) * tk, tk)
        cmax = []
        for hh in range(2):
            cols = slice(hh * HEAD_PAD, (hh + 1) * HEAD_PAD)
            s = _dot_nt(k_ref[pl.ds(koff, tk), cols], q_ref[pl.ds(qoff, tq), cols])
            s_buf[par][hh] = s
            cmax.append(jnp.max(s, axis=0, keepdims=True))
        return tuple(cmax)

    def pv_issue(j, slot):
        kb = j % n_k
        out = []
        for hh in range(2):
            vt = jnp.concatenate([vt_ref[kb * sub + c, hh * HEAD_PAD:(hh + 1) * HEAD_PAD, :]
                                  for c in range(sub)], axis=1)
            out.append(_dot(vt, p_buf[slot][hh]))
        return out

    def pv_accumulate(r, alpha):
        for hh in range(2):
            acc_scr[hh] = alpha[hh] * acc_scr[hh] + r[hh]

    def softmax(j, par, slot, m, cmax):
        first = (j % n_k) == 0
        m_out, alpha = [], []
        for hh in range(2):
            m_old = jnp.where(first, -jnp.inf, m[hh])
            m_new = jnp.maximum(m_old, cmax[hh])
            p_buf[slot][hh] = jnp.exp2(s_buf[par][hh] - m_new).astype(BF16)
            alpha.append(jnp.exp2(m_old - m_new))
            m_out.append(m_new)
        return tuple(m_out), tuple(alpha)

    def finalize(j):
        qoff = _aligned((j // n_k) * tq, tq)
        o_t = jnp.concatenate([acc_scr[hh, :MLA_V] / acc_scr[hh, MLA_V:MLA_V + 1] for hh in range(2)],
                              axis=0)
        o_ref[pl.ds(qoff, tq), :] = o_t.T.astype(BF16)

    def pv(j, alpha, r=None):
        pv_accumulate(pv_issue(j, j % n_slots) if r is None else r, alpha)
        if isinstance(j, int) and j % n_k == n_k - 1:
            finalize(j)

    def body(k, kmod, m, alpha_prev, alpha_cur, cmax):
        par = kmod % 2
        r = pv_issue(k - 1, (kmod - 1) % n_slots)
        cmax_next = scores(k + 2, par)
        m, alpha_next = softmax(k + 1, 1 - par, (kmod + 1) % n_slots, m, cmax)
        pv(k - 1, alpha_prev, r)
        return m, alpha_cur, alpha_next, cmax_next

    def loop_body(kk, carry):
        k0 = bodies * kk + 1
        for o in range(bodies):
            carry = body(k0 + o, (1 + o) % n_slots, *carry)
            if bodies % n_k == 0:
                if o % n_k == n_k - 1:
                    finalize(k0 + o - 1)
            elif o == bodies - 1:
                pl.when((k0 + o - 1) % n_k == n_k - 1)(functools.partial(finalize, k0 + o - 1))
        return carry

    acc_scr[...] = jnp.zeros_like(acc_scr)
    m = tuple(jnp.full((1, tq), -jnp.inf, F32) for _ in range(2))
    cmax0 = scores(0, 0)
    cmax1 = scores(1, 1)
    m, alpha0 = softmax(0, 0, 0, m, cmax0)
    cmax2 = scores(2, 0)
    m, alpha1 = softmax(1, 1, 1, m, cmax1)
    n_iter = (n_total - 3) // bodies
    carry = lax.fori_loop(0, n_iter, loop_body, (m, alpha0, alpha1, cmax2))
    k = bodies * n_iter + 1
    while k <= n_total - 3:
        carry = body(k, k % n_slots, *carry)
        k += 1
    m, alpha_prev, alpha_cur, cmax = carry
    r = pv_issue(k - 1, (k - 1) % n_slots)
    m, alpha_last = softmax(k + 1, (k + 1) % 2, (k + 1) % n_slots, m, cmax)
    pv(k - 1, alpha_prev, r)
    pv(n_total - 2, alpha_cur)
    pv(n_total - 1, alpha_last)


def _scan_lanes(x, op, fill, reverse):
    n = x.shape[1]
    lane = lax.broadcasted_iota(jnp.int32, x.shape, 1)
    shift = 1
    while shift < n:
        if reverse:
            moved, valid = pltpu.roll(x, n - shift, 1), lane < n - shift
        else:
            moved, valid = pltpu.roll(x, shift, 1), lane >= shift
        x = op(x, jnp.where(valid, moved, fill))
        shift *= 2
    return x


class _Chain:
    pass


def _mlstm_kernel(q_ref, k_ref, vt_ref, mot_ref, gate_ref, og_ref, o_ref,
                  hacc, st_scr, b_scr, a_scr, cm_scr, bl_scr, am_scr, *, L, nc):
    dh = MLSTM_DH
    aug = st_scr.shape[1]
    s_idx = lax.broadcasted_iota(jnp.int32, (L, L), 0)
    t_idx = lax.broadcasted_iota(jnp.int32, (L, L), 1)
    eye = s_idx == t_idx
    visible = (s_idx <= t_idx, s_idx >= t_idx)
    ones_blk = (lax.broadcasted_iota(jnp.int32, (aug - dh, L), 0) == 0).astype(BF16)

    for d in range(2):
        logi = gate_ref[d, 0, 0]
        logf = gate_ref[d, 1, 0]
        b = _scan_lanes(logf, jnp.add, 0.0, reverse=d == 1)
        a = logi - b
        b_scr[d] = b
        a_scr[d] = a
        cm_scr[d] = _scan_lanes(a, jnp.maximum, -jnp.inf, reverse=d == 1)
        bl_scr[d] = jnp.sum(logf, axis=1, keepdims=True)
        am_scr[d] = jnp.max(a, axis=1, keepdims=True)
    st_scr[...] = jnp.zeros_like(st_scr)

    def prepare(d, c, m):
        x = _Chain()
        off = pl.multiple_of(c * L, L)
        x.q = q_ref[pl.ds(off, L), :]
        x.k = k_ref[pl.ds(off, L), :]
        x.vaug = jnp.concatenate([vt_ref[c], ones_blk], axis=0)
        x.b_row = b_scr[d, pl.ds(c, 1), :]
        x.a_row = a_scr[d, pl.ds(c, 1), :]
        b_last = bl_scr[d, pl.ds(c, 1), :]
        x.g_row = jnp.maximum(cm_scr[d, pl.ds(c, 1), :], m)
        x.iw = jnp.exp(m - x.g_row)
        x.m_new = jnp.maximum(b_last + m, b_last + am_scr[d, pl.ds(c, 1), :])
        x.decay = jnp.exp(b_last + m - x.m_new)
        x.w_row = jnp.exp(b_last + x.a_row - x.m_new)
        return x

    def intra(d, x):
        a_col = jnp.sum(jnp.where(eye, x.a_row, 0.0), axis=1, keepdims=True)
        e_t = jnp.exp(jnp.where(visible[d], a_col - x.g_row, -jnp.inf))
        x.r = _dot(x.vaug, (x.st * e_t).astype(BF16))

    def finish(x):
        num = x.iw * x.inter[:dh] + x.r[:dh]
        den = x.iw * x.inter[dh:dh + 1] + x.r[dh:dh + 1]
        floor = jnp.exp(-(x.b_row + x.g_row))
        return num * (1.0 / jnp.maximum(jnp.abs(den), floor))

    def run(j0, m_f, m_b):
        chains = []
        for o in range(MLSTM_POSITIONS_PER_BODY):
            j = j0 + o
            xf = prepare(0, j, m_f)
            xb = prepare(1, nc - 1 - j, m_b)
            m_f, m_b = xf.m_new, xb.m_new
            chains += [(0, j, xf), (1, nc - 1 - j, xb)]
        for d, _, x in chains:
            x.st = _dot_nt(x.k, x.q)
            x.upd = _dot((x.vaug.astype(F32) * x.w_row).astype(BF16), x.k)
        for d, _, x in chains:
            x.inter = _dot_nt(st_scr[d].astype(BF16), x.q)
            st_scr[d] = x.decay * st_scr[d] + x.upd
        for d, _, x in chains:
            intra(d, x)
        return [(c, finish(x)) for _, c, x in chains], m_f, m_b

    def first_touch(jj, carry):
        outs, m_f, m_b = run(jj * MLSTM_POSITIONS_PER_BODY, *carry)
        for c, h in outs:
            hacc[c] = h
        return m_f, m_b

    def second_touch(jj, carry):
        outs, m_f, m_b = run(jj * MLSTM_POSITIONS_PER_BODY, *carry)
        for c, h in outs:
            tot = hacc[c] + h
            y = tot * lax.rsqrt(jnp.mean(tot * tot, axis=0, keepdims=True) + NORM_EPS) * og_ref[...]
            o_ref[c] = (y * mot_ref[c].astype(F32)).astype(BF16)
        return m_f, m_b

    n_iter = nc // MLSTM_POSITIONS_PER_BODY
    m0 = jnp.zeros((1, 1), F32)
    carry = lax.fori_loop(0, n_iter // 2, first_touch, (m0, m0))
    lax.fori_loop(n_iter // 2, n_iter, second_touch, carry)


def _merge_mlp_kernel(x_ref, ya_ref, ymt_ref, g1_ref, wgab_ref, wbm_ref, wbl_ref, wout_ref,
                      g2_ref, wup_ref, wdn_ref, gf_ref, o_ref, *, d, final_norm):
    x = x_ref[...]
    hn = _rms(x, g1_ref[...]).astype(BF16)
    gates = jax.nn.sigmoid(_dot(hn, wgab_ref[...]))
    merged = (gates[:, :d] * _dot(ya_ref[...], wbm_ref[...])
              + gates[:, d:] * _dot_tn(ymt_ref[0], wbl_ref[...]))
    x1 = x + _dot(merged.astype(BF16), wout_ref[...])
    u = _dot(_rms(x1, g2_ref[...]).astype(BF16), wup_ref[...])
    r = jnp.maximum(u, 0.0)
    x2 = x1 + _dot((r * r).astype(BF16), wdn_ref[...])
    o_ref[...] = _rms(x2, gf_ref[...]) if final_norm else x2


def _const_spec(shape):
    return pl.BlockSpec(shape, lambda *_: (0,) * len(shape))


def _resident_spec(shape):
    return pl.BlockSpec(shape, lambda *_: (0,) * len(shape), pipeline_mode=pl.Buffered(1))


def _params(n_axes):
    return pltpu.CompilerParams(dimension_semantics=("arbitrary",) * n_axes,
                                vmem_limit_bytes=VMEM_LIMIT_BYTES)


def _rope_kernel(pos_ref, freq_ref, cos_out, sin_out):
    t = pos_ref.shape[1]
    ang = freq_ref[...] * pos_ref[...]
    cos, sin = jnp.cos(ang), jnp.sin(ang)
    pad = jnp.zeros((HEAD_PAD - MLA_NOPE - MLA_ROPE, t), F32)
    cos_out[...] = jnp.concatenate([jnp.ones((MLA_NOPE, t), F32), cos, cos, pad], axis=0)
    sin_out[...] = jnp.concatenate([jnp.zeros((MLA_NOPE, t), F32), -sin, sin, pad], axis=0)


def _rope_tables(positions):
    t = positions.size
    inv_freq = ROPE_THETA ** (-jnp.arange(0, MLA_ROPE, 2, dtype=F32) / MLA_ROPE)
    table = jax.ShapeDtypeStruct((HEAD_PAD, t), F32)
    tile = min(t, ROPE_TABLE_LANES)
    return pl.pallas_call(
        _rope_kernel,
        grid=(t // tile,),
        in_specs=[pl.BlockSpec((1, tile), lambda i: (0, i)), _const_spec((MLA_ROPE // 2, 1))],
        out_specs=[pl.BlockSpec((HEAD_PAD, tile), lambda i: (0, i))] * 2,
        out_shape=[table, table], compiler_params=_params(1), name="rope_table",
    )(positions.astype(F32).reshape(1, t), inv_freq.reshape(-1, 1))


def _pad_heads(w, width_in, offset_out):
    k = w.shape[0]
    w = w.reshape(k, MLA_HEADS, width_in)
    w = jnp.pad(w, ((0, 0), (0, 0), (offset_out, HEAD_PAD - width_in - offset_out)))
    return w.reshape(k, MLA_HEADS * HEAD_PAD)


def kernel(x, positions, norm_mix_g, w_in, mla_q_norm_g, mla_w_uq, mla_kv_norm_g, mla_w_ukv, mlstm_conv_w, mlstm_conv_b, mlstm_igate_b, mlstm_fgate_b, mlstm_out_norm_g, w_branch_mla, w_branch_mlstm, w_out, norm_mlp_g, w_mlp_up, w_mlp_down, norm_final_g):
    B, S, D = x.shape
    T = B * S
    depth = w_in.shape[0]
    q_lora = mla_q_norm_g.shape[1]
    kv_lora = mla_kv_norm_g.shape[1]
    mw = MLSTM_HEADS * MLSTM_DH
    n_gate = 4 * MLSTM_HEADS
    L = MLSTM_CHUNK
    nc = S // L
    tm = tm4 = ROW_TILE
    tq, tk = ATTN_Q_TILE, ATTN_KEY_STEP
    assert S % (2 * MLSTM_POSITIONS_PER_BODY * L) == 0 and MLA_HEADS % 2 == 0
    assert S % tk == 0 and S % tq == 0 and tk % tm == 0 and tm % HALO == 0

    cos_t, sin_t = _rope_tables(positions)
    xf = x.reshape(T, D)

    for l in range(depth):
        offs, o = [], 0
        for w in (q_lora, kv_lora, MLA_ROPE, mw, mw, mw, mw, n_gate, D, D):
            offs.append(o)
            o += w
        wl = w_in[l]
        w_cq = wl[:, offs[0]:offs[0] + q_lora]
        w_ckv = wl[:, offs[1]:offs[1] + kv_lora]
        w_kr = wl[:, offs[2]:offs[2] + MLA_ROPE]
        pad_kr = ((0, 0), (MLA_NOPE, HEAD_PAD - MLA_NOPE - MLA_ROPE))
        w1 = wl[:, offs[3]:offs[3] + 2 * mw].astype(BF16)
        w2t = wl[:, offs[5]:offs[5] + 2 * mw].T.astype(BF16)
        w3 = jnp.concatenate([w_cq, w_ckv, jnp.pad(w_kr, pad_kr)], axis=1).astype(BF16)
        w2t = jnp.concatenate([w2t, wl[:, offs[7]:offs[7] + n_gate].T.astype(BF16)], axis=0)
        wgab = wl[:, offs[8]:offs[8] + 2 * D].astype(BF16)
        gate_b = jnp.stack([mlstm_igate_b[l], mlstm_fgate_b[l]], axis=1).reshape(n_gate, 1)

        wqa = _pad_heads(mla_w_uq[l], MLA_NOPE + MLA_ROPE, 0).astype(BF16)
        wkv = mla_w_ukv[l].reshape(kv_lora, MLA_HEADS, MLA_NOPE + MLA_V)
        wuk = _pad_heads(wkv[:, :, :MLA_NOPE].reshape(kv_lora, -1), MLA_NOPE, 0).astype(BF16)
        wuvt = wkv[:, :, MLA_NOPE:].reshape(kv_lora, MLA_HEADS * MLA_V).T.astype(BF16)

        cw = jnp.pad(mlstm_conv_w[l].reshape(CONV_WIDTH, 2 * mw), ((0, F32_SUBLANES - CONV_WIDTH), (0, 0)))
        cb = mlstm_conv_b[l].reshape(1, 2 * mw)

        tiles_per_seq = S // tm
        hb = tm // HALO
        n_halo_blocks = T // HALO
        row = lambda i: (i, 0)
        col = lambda i: (0, i)
        chunk = lambda i: (i, 0, 0)
        kern = functools.partial(
            _inproj_kernel, tm=tm, tiles_per_seq=tiles_per_seq, q_lora=q_lora, kv_lora=kv_lora, mw=mw,
            q_scale=(MLA_NOPE + MLA_ROPE) ** -0.5 * math.log2(math.e), k_scale=MLSTM_DH ** -0.5)
        outs = pl.pallas_call(
            kern,
            grid=(T // tm,),
            in_specs=[
                pl.BlockSpec((tm, D), row),
                pl.BlockSpec((HALO, D), lambda i: (jnp.maximum(i * hb - 1, 0), 0)),
                pl.BlockSpec((HALO, D), lambda i: (jnp.minimum((i + 1) * hb, n_halo_blocks - 1), 0)),
                pl.BlockSpec((HEAD_PAD, tm), col),
                pl.BlockSpec((HEAD_PAD, tm), col),
                _const_spec((1, D)),
                _const_spec(w1.shape), _const_spec(w2t.shape), _const_spec(w3.shape),
                _const_spec(gate_b.shape),
                _const_spec(cw.shape), _const_spec(cb.shape),
                _const_spec((1, q_lora)), _const_spec(wqa.shape),
                _const_spec((1, kv_lora)), _const_spec(wuk.shape), _const_spec(wuvt.shape),
            ],
            out_specs=[
                pl.BlockSpec((tm, MLA_HEADS * HEAD_PAD), row),
                pl.BlockSpec((tm, MLA_HEADS * HEAD_PAD), row),
                pl.BlockSpec((1, MLA_HEADS * HEAD_PAD, tm), chunk),
                pl.BlockSpec((tm, mw), row), pl.BlockSpec((tm, mw), row),
                pl.BlockSpec((1, mw, tm), chunk),
                pl.BlockSpec((1, mw, tm), chunk),
                pl.BlockSpec((n_gate, tm), col),
            ],
            out_shape=[
                jax.ShapeDtypeStruct((T, MLA_HEADS * HEAD_PAD), BF16),
                jax.ShapeDtypeStruct((T, MLA_HEADS * HEAD_PAD), BF16),
                jax.ShapeDtypeStruct((T // tm, MLA_HEADS * HEAD_PAD, tm), BF16),
                jax.ShapeDtypeStruct((T, mw), BF16), jax.ShapeDtypeStruct((T, mw), BF16),
                jax.ShapeDtypeStruct((T // tm, mw, tm), BF16),
                jax.ShapeDtypeStruct((T // tm, mw, tm), BF16),
                jax.ShapeDtypeStruct((n_gate, T), F32),
            ],
            compiler_params=_params(1),
            name="inproj",
        )(xf, xf, xf, cos_t, sin_t, norm_mix_g[l].reshape(1, D), w1, w2t, w3, gate_b, cw, cb,
          mla_q_norm_g[l].reshape(1, q_lora), wqa, mla_kv_norm_g[l].reshape(1, kv_lora), wuk, wuvt)
        q_a, k_a, v_a, q_m, k_m, v_m, mo_s, gates = outs

        vchunk = tm
        y_attn = pl.pallas_call(
            functools.partial(_attn_kernel, tq=tq, tk=tk, vchunk=vchunk, n_q=S // tq, n_k=S // tk),
            grid=(B, MLA_HEADS // 2),
            in_specs=[
                pl.BlockSpec((S, 2 * HEAD_PAD), lambda b, p: (b, p)),
                pl.BlockSpec((S, 2 * HEAD_PAD), lambda b, p: (b, p)),
                pl.BlockSpec((S // vchunk, 2 * HEAD_PAD, vchunk), lambda b, p: (b, p, 0)),
            ],
            out_specs=pl.BlockSpec((S, 2 * MLA_V), lambda b, p: (b, p)),
            out_shape=jax.ShapeDtypeStruct((T, MLA_HEADS * MLA_V), BF16),
            scratch_shapes=[pltpu.VMEM((2, tk, tq), F32), pltpu.VMEM((2, tk, tq), F32),
                            pltpu.VMEM((2, tk, tq), BF16), pltpu.VMEM((2, tk, tq), BF16),
                            pltpu.VMEM((2, tk, tq), BF16), pltpu.VMEM((2, tk, tq), BF16),
                            pltpu.VMEM((2, HEAD_PAD, tq), F32)],
            compiler_params=_params(2),
            name="mla_attn",
        )(q_a, k_a, v_a)

        gates5 = gates.reshape(2, 2, MLSTM_HEADS, T // L, L)
        head_blk = lambda b, h: (b, h)
        head_blk_t = lambda b, h: (b, h, 0)
        aug = MLSTM_DH + BF16_SUBLANES
        row_scratch = pltpu.VMEM((2, nc, L), F32)
        y_mlstm_t = pl.pallas_call(
            functools.partial(_mlstm_kernel, L=L, nc=nc),
            grid=(B, MLSTM_HEADS),
            in_specs=[
                pl.BlockSpec((S, MLSTM_DH), head_blk), pl.BlockSpec((S, MLSTM_DH), head_blk),
                pl.BlockSpec((nc, MLSTM_DH, L), head_blk_t), pl.BlockSpec((nc, MLSTM_DH, L), head_blk_t),
                pl.BlockSpec((2, 2, 1, nc, L), lambda b, h: (0, 0, h, b, 0)),
                pl.BlockSpec((MLSTM_DH, 1), lambda b, h: (h, 0)),
            ],
            out_specs=pl.BlockSpec((nc, MLSTM_DH, L), head_blk_t),
            out_shape=jax.ShapeDtypeStruct((T // L, mw, L), BF16),
            scratch_shapes=[pltpu.VMEM((nc, MLSTM_DH, L), F32),
                            pltpu.VMEM((2, aug, MLSTM_DH), F32),
                            row_scratch, row_scratch, row_scratch,
                            pltpu.VMEM((2, nc, 1), F32), pltpu.VMEM((2, nc, 1), F32)],
            compiler_params=_params(2),
            name="mlstm",
        )(q_m, k_m, v_m, mo_s, gates5, mlstm_out_norm_g[l].reshape(mw, 1))

        last = l == depth - 1
        gf = norm_final_g.reshape(1, D)
        weights4 = (wgab, w_branch_mla[l].astype(BF16), w_branch_mlstm[l].astype(BF16),
                    w_out[l].astype(BF16), w_mlp_up[l].astype(BF16), w_mlp_down[l].astype(BF16))
        xf = pl.pallas_call(
            functools.partial(_merge_mlp_kernel, d=D, final_norm=last),
            grid=(T // tm4,),
            in_specs=[
                pl.BlockSpec((tm4, D), row),
                pl.BlockSpec((tm4, MLA_HEADS * MLA_V), row),
                pl.BlockSpec((1, mw, tm4), lambda i: (i, 0, 0)),
                _const_spec((1, D)),
                _resident_spec(weights4[0].shape), _resident_spec(weights4[1].shape),
                _resident_spec(weights4[2].shape), _resident_spec(weights4[3].shape),
                _const_spec((1, D)),
                _resident_spec(weights4[4].shape), _resident_spec(weights4[5].shape),
                _const_spec((1, D)),
            ],
            out_specs=pl.BlockSpec((tm4, D), row),
            out_shape=jax.ShapeDtypeStruct((T, D), F32),
            compiler_params=_params(1),
            name="merge_mlp",
        )(xf, y_attn, y_mlstm_t, norm_mix_g[l].reshape(1, D), weights4[0], weights4[1], weights4[2],
          weights4[3], norm_mlp_g[l].reshape(1, D), weights4[4], weights4[5], gf)

    return xf.reshape(B, S, D)
```

```python
import functools
import math

import jax
import jax.numpy as jnp
from jax import lax
from jax.experimental import pallas as pl
from jax.experimental.pallas import tpu as pltpu

MLA_HEADS = 8
MLA_NOPE = 64
MLA_ROPE = 32
MLA_V = 64
ROPE_THETA = 10000.0
MLSTM_HEADS = 4
MLSTM_DH = 128
CONV_WIDTH = 5
NORM_EPS = 1e-6

LANES = 128
F32_SUBLANES = 8
BF16_SUBLANES = 16
VMEM_LIMIT_BYTES = 56 * 1024 * 1024

MXU_WIDTH = 256

HEAD_PAD = LANES
MLSTM_CHUNK = MXU_WIDTH
MLSTM_POSITIONS_PER_BODY = 4
ROW_TILE = MLSTM_CHUNK
ATTN_Q_TILE = MXU_WIDTH
ATTN_KEY_STEP = 4 * MXU_WIDTH
ATTN_BODIES_PER_ITER = 16
HALO = BF16_SUBLANES
F32 = jnp.float32
BF16 = jnp.bfloat16


def _rms(x, g):
    return x * lax.rsqrt(jnp.mean(x * x, axis=-1, keepdims=True) + NORM_EPS) * g


def _dot(a, b):
    return jnp.dot(a, b, preferred_element_type=F32)


def _dot_nt(a, b):
    return lax.dot_general(a, b, (((1,), (1,)), ((), ())), preferred_element_type=F32)


def _dot_tn(a, b):
    return lax.dot_general(a, b, (((0,), (0,)), ((), ())), preferred_element_type=F32)


def _log_sigmoid(x):
    return jnp.minimum(x, 0.0) - jnp.log1p(jnp.exp(-jnp.abs(x)))


def _inproj_kernel(xm_ref, xp_ref, xn_ref, pos_ref, freq_ref, g_ref, w1_ref, w2t_ref, w3_ref,
                   gb_ref, cw_ref, cb_ref, qg_ref, wqa_ref, kvg_ref,
                   wuk_ref, wuvt_ref,
                   q_out, k_out, va_out, qm_out, km_out, vm_out, mo_out, gate_out,
                   *, tm, tiles_per_seq, q_lora, kv_lora, mw, q_scale, k_scale):
    i = pl.program_id(0)
    pos_in_seq = i % tiles_per_seq
    xp = jnp.where(pos_in_seq == 0, 0.0, xp_ref[...])
    xn = jnp.where(pos_in_seq == tiles_per_seq - 1, 0.0, xn_ref[...])
    xe = jnp.concatenate([xp, xm_ref[...], xn], axis=0)
    he = _rms(xe, g_ref[...]).astype(BF16)
    hm = he[HALO:HALO + tm]

    pre = _dot(he, w1_ref[...])
    rows = pre.shape[0]
    conv = cb_ref[...]
    for j in range(CONV_WIDTH):
        shifted = pre if j == CONV_WIDTH // 2 else pltpu.roll(pre, (CONV_WIDTH // 2 - j) % rows, 0)
        conv = conv + cw_ref[j:j + 1, :] * shifted[HALO:HALO + tm]
    qk = conv * jax.nn.sigmoid(conv)
    qm_out[...] = qk[:, :mw].astype(BF16)
    km_out[...] = (qk[:, mw:] * k_scale).astype(BF16)

    vo_t = _dot_nt(w2t_ref[...], hm)
    vm_out[0] = vo_t[:mw].astype(BF16)
    mo_out[0] = jax.nn.sigmoid(vo_t[mw:2 * mw]).astype(BF16)

    gt = vo_t[2 * mw:] + gb_ref[...]
    row = lax.broadcasted_iota(jnp.int32, gt.shape, 0)
    is_f = (row % (2 * MLSTM_HEADS)) >= MLSTM_HEADS
    gate_out[...] = jnp.where(is_f, _log_sigmoid(gt), gt)

    c = _dot(hm, w3_ref[...])
    cqn = _rms(c[:, :q_lora], qg_ref[...]).astype(BF16)
    ckvn = _rms(c[:, q_lora:q_lora + kv_lora], kvg_ref[...]).astype(BF16)
    ang = freq_ref[...] * pos_ref[...]
    cos_a, sin_a = jnp.cos(ang), jnp.sin(ang)
    pad = jnp.zeros((HEAD_PAD - MLA_NOPE - MLA_ROPE, tm), F32)
    cos = jnp.concatenate([jnp.ones((MLA_NOPE, tm), F32), cos_a, cos_a, pad], axis=0).T
    sin = jnp.concatenate([jnp.zeros((MLA_NOPE, tm), F32), -sin_a, sin_a, pad], axis=0).T
    half = MLA_ROPE // 2
    lane = lax.broadcasted_iota(jnp.int32, sin.shape, 1)
    sin_lo = jnp.where(lane < MLA_NOPE + half, sin, 0.0)
    sin_hi = sin - sin_lo

    def rope(x):
        return (x * cos + pltpu.roll(x, HEAD_PAD - half, 1) * sin_lo + pltpu.roll(x, half, 1) * sin_hi)

    qa = _dot(cqn, wqa_ref[...])
    q = jnp.concatenate([rope(qa[:, h * HEAD_PAD:(h + 1) * HEAD_PAD]) for h in range(MLA_HEADS)], axis=1)
    q_out[...] = (q * q_scale).astype(BF16)
    kr = rope(c[:, q_lora + kv_lora:])
    k = _dot(ckvn, wuk_ref[...]) + jnp.concatenate([kr] * MLA_HEADS, axis=1)
    k_out[...] = k.astype(BF16)
    vt = _dot_nt(wuvt_ref[...], ckvn)
    ones = jnp.ones((HEAD_PAD - MLA_V, tm), F32)
    pieces = []
    for h in range(MLA_HEADS):
        pieces += [vt[h * MLA_V:(h + 1) * MLA_V], ones]
    va_out[0] = jnp.concatenate(pieces, axis=0).astype(BF16)


def _aligned(x, m):
    return x if isinstance(x, int) else pl.multiple_of(x, m)


def _attn_kernel(q_ref, k_ref, vt_ref, o_ref, s_0, s_1, p_0, p_1, p_2, p_3, acc_scr, *,
                 tq, tk, vchunk, n_q, n_k):
    s_buf, p_buf = (s_0, s_1), (p_0, p_1, p_2, p_3)
    n_slots = len(p_buf)
    bodies = ATTN_BODIES_PER_ITER
    sub = tk // vchunk
    n_total = n_q * n_k
    assert bodies % n_slots == 0 and (n_k % bodies == 0 or bodies % n_k == 0)
    assert n_slots % 2 == 0 and n_total >= 6

    def scores(j, par):
        qoff = _aligned((j // n_k) * tq, tq)
        koff = _aligned((j % n_k) * tk, tk)
        cmax = []
        for hh in range(2):
            cols = slice(hh * HEAD_PAD, (hh + 1) * HEAD_PAD)
            s = _dot_nt(k_ref[pl.ds(koff, tk), cols], q_ref[pl.ds(qoff, tq), cols])
            s_buf[par][hh] = s
            cmax.append(jnp.max(s, axis=0, keepdims=True))
        return tuple(cmax)

    def pv_issue(j, slot):
        kb = j % n_k
        out = []
        for hh in range(2):
            vt = jnp.concatenate([vt_ref[kb * sub + c, hh * HEAD_PAD:(hh + 1) * HEAD_PAD, :]
                                  for c in range(sub)], axis=1)
            out.append(_dot(vt, p_buf[slot][hh]))
        return out

    def pv_accumulate(r, alpha):
        for hh in range(2):
            acc_scr[hh] = alpha[hh] * acc_scr[hh] + r[hh]

    def softmax(j, par, slot, m, cmax):
        first = (j % n_k) == 0
        m_out, alpha = [], []
        for hh in range(2):
            m_old = jnp.where(first, -jnp.inf, m[hh])
            m_new = jnp.maximum(m_old, cmax[hh])
            p_buf[slot][hh] = jnp.exp2(s_buf[par][hh] - m_new).astype(BF16)
            alpha.append(jnp.exp2(m_old - m_new))
            m_out.append(m_new)
        return tuple(m_out), tuple(alpha)

    def finalize(j):
        qoff = _aligned((j // n_k) * tq, tq)
        o_t = jnp.concatenate([acc_scr[hh, :MLA_V] / acc_scr[hh, MLA_V:MLA_V + 1] for hh in range(2)],
                              axis=0)
        o_ref[pl.ds(qoff, tq), :] = o_t.T.astype(BF16)

    def pv(j, alpha, r=None):
        pv_accumulate(pv_issue(j, j % n_slots) if r is None else r, alpha)
        if isinstance(j, int) and j % n_k == n_k - 1:
            finalize(j)

    def body(k, kmod, m, alpha_prev, alpha_cur, cmax):
        par = kmod % 2
        r = pv_issue(k - 1, (kmod - 1) % n_slots)
        cmax_next = scores(k + 2, par)
        m, alpha_next = softmax(k + 1, 1 - par, (kmod + 1) % n_slots, m, cmax)
        pv(k - 1, alpha_prev, r)
        return m, alpha_cur, alpha_next, cmax_next

    def loop_body(kk, carry):
        k0 = bodies * kk + 1
        for o in range(bodies):
            carry = body(k0 + o, (1 + o) % n_slots, *carry)
            if bodies % n_k == 0:
                if o % n_k == n_k - 1:
                    finalize(k0 + o - 1)
            elif o == bodies - 1:
                pl.when((k0 + o - 1) % n_k == n_k - 1)(functools.partial(finalize, k0 + o - 1))
        return carry

    acc_scr[...] = jnp.zeros_like(acc_scr)
    m = tuple(jnp.full((1, tq), -jnp.inf, F32) for _ in range(2))
    cmax0 = scores(0, 0)
    cmax1 = scores(1, 1)
    m, alpha0 = softmax(0, 0, 0, m, cmax0)
    cmax2 = scores(2, 0)
    m, alpha1 = softmax(1, 1, 1, m, cmax1)
    n_iter = (n_total - 3) // bodies
    carry = lax.fori_loop(0, n_iter, loop_body, (m, alpha0, alpha1, cmax2))
    k = bodies * n_iter + 1
    while k <= n_total - 3:
        carry = body(k, k % n_slots, *carry)
        k += 1
    m, alpha_prev, alpha_cur, cmax = carry
    r = pv_issue(k - 1, (k - 1) % n_slots)
    m, alpha_last = softmax(k + 1, (k + 1) % 2, (k + 1) % n_slots, m, cmax)
    pv(k - 1, alpha_prev, r)
    pv(n_total - 2, alpha_cur)
    pv(n_total - 1, alpha_last)


def _scan_lanes(x, op, fill, reverse):
    n = x.shape[1]
    lane = lax.broadcasted_iota(jnp.int32, x.shape, 1)
    shift = 1
    while shift < n:
        if reverse:
            moved, valid = pltpu.roll(x, n - shift, 1), lane < n - shift
        else:
            moved, valid = pltpu.roll(x, shift, 1), lane >= shift
        x = op(x, jnp.where(valid, moved, fill))
        shift *= 2
    return x


class _Chain:
    pass


def _mlstm_kernel(q_ref, k_ref, vt_ref, mot_ref, gate_ref, og_ref, o_ref,
                  hacc, st_scr, b_scr, a_scr, cm_scr, bl_scr, am_scr, *, L, nc):
    dh = MLSTM_DH
    aug = st_scr.shape[1]
    s_idx = lax.broadcasted_iota(jnp.int32, (L, L), 0)
    t_idx = lax.broadcasted_iota(jnp.int32, (L, L), 1)
    eye = s_idx == t_idx
    visible = (s_idx <= t_idx, s_idx >= t_idx)
    ones_blk = (lax.broadcasted_iota(jnp.int32, (aug - dh, L), 0) == 0).astype(BF16)

    for d in range(2):
        logi = gate_ref[d, 0, 0]
        logf = gate_ref[d, 1, 0]
        b = _scan_lanes(logf, jnp.add, 0.0, reverse=d == 1)
        a = logi - b
        b_scr[d] = b
        a_scr[d] = a
        cm_scr[d] = _scan_lanes(a, jnp.maximum, -jnp.inf, reverse=d == 1)
        bl_scr[d] = jnp.sum(logf, axis=1, keepdims=True)
        am_scr[d] = jnp.max(a, axis=1, keepdims=True)
    st_scr[...] = jnp.zeros_like(st_scr)

    def prepare(d, c, m):
        x = _Chain()
        off = pl.multiple_of(c * L, L)
        x.q = q_ref[pl.ds(off, L), :]
        x.k = k_ref[pl.ds(off, L), :]
        x.vaug = jnp.concatenate([vt_ref[c], ones_blk], axis=0)
        x.b_row = b_scr[d, pl.ds(c, 1), :]
        x.a_row = a_scr[d, pl.ds(c, 1), :]
        b_last = bl_scr[d, pl.ds(c, 1), :]
        x.g_row = jnp.maximum(cm_scr[d, pl.ds(c, 1), :], m)
        x.iw = jnp.exp(m - x.g_row)
        x.m_new = jnp.maximum(b_last + m, b_last + am_scr[d, pl.ds(c, 1), :])
        x.decay = jnp.exp(b_last + m - x.m_new)
        x.w_row = jnp.exp(b_last + x.a_row - x.m_new)
        return x

    def intra(d, x):
        a_col = jnp.sum(jnp.where(eye, x.a_row, 0.0), axis=1, keepdims=True)
        e_t = jnp.exp(jnp.where(visible[d], a_col - x.g_row, -jnp.inf))
        x.r = _dot(x.vaug, (x.st * e_t).astype(BF16))

    def finish(x):
        num = x.iw * x.inter[:dh] + x.r[:dh]
        den = x.iw * x.inter[dh:dh + 1] + x.r[dh:dh + 1]
        floor = jnp.exp(-(x.b_row + x.g_row))
        return num * (1.0 / jnp.maximum(jnp.abs(den), floor))

    def run(j0, m_f, m_b):
        chains = []
        for o in range(MLSTM_POSITIONS_PER_BODY):
            j = j0 + o
            xf = prepare(0, j, m_f)
            xb = prepare(1, nc - 1 - j, m_b)
            m_f, m_b = xf.m_new, xb.m_new
            chains += [(0, j, xf), (1, nc - 1 - j, xb)]
        for d, _, x in chains:
            x.st = _dot_nt(x.k, x.q)
            x.upd = _dot((x.vaug.astype(F32) * x.w_row).astype(BF16), x.k)
        for d, _, x in chains:
            x.inter = _dot_nt(st_scr[d].astype(BF16), x.q)
            st_scr[d] = x.decay * st_scr[d] + x.upd
        for d, _, x in chains:
            intra(d, x)
        return [(c, finish(x)) for _, c, x in chains], m_f, m_b

    def first_touch(jj, carry):
        outs, m_f, m_b = run(jj * MLSTM_POSITIONS_PER_BODY, *carry)
        for c, h in outs:
            hacc[c] = h
        return m_f, m_b

    def second_touch(jj, carry):
        outs, m_f, m_b = run(jj * MLSTM_POSITIONS_PER_BODY, *carry)
        for c, h in outs:
            tot = hacc[c] + h
            y = tot * lax.rsqrt(jnp.mean(tot * tot, axis=0, keepdims=True) + NORM_EPS) * og_ref[...]
            o_ref[c] = (y * mot_ref[c].astype(F32)).astype(BF16)
        return m_f, m_b

    n_iter = nc // MLSTM_POSITIONS_PER_BODY
    m0 = jnp.zeros((1, 1), F32)
    carry = lax.fori_loop(0, n_iter // 2, first_touch, (m0, m0))
    lax.fori_loop(n_iter // 2, n_iter, second_touch, carry)


def _merge_mlp_kernel(x_ref, ya_ref, ymt_ref, g1_ref, wgab_ref, wbm_ref, wbl_ref, wout_ref,
                      g2_ref, wup_ref, wdn_ref, gf_ref, o_ref, *, d, final_norm):
    x = x_ref[...]
    hn = _rms(x, g1_ref[...]).astype(BF16)
    gates = jax.nn.sigmoid(_dot(hn, wgab_ref[...]))
    merged = (gates[:, :d] * _dot(ya_ref[...], wbm_ref[...])
              + gates[:, d:] * _dot_tn(ymt_ref[0], wbl_ref[...]))
    x1 = x + _dot(merged.astype(BF16), wout_ref[...])
    u = _dot(_rms(x1, g2_ref[...]).astype(BF16), wup_ref[...])
    r = jnp.maximum(u, 0.0)
    x2 = x1 + _dot((r * r).astype(BF16), wdn_ref[...])
    o_ref[...] = _rms(x2, gf_ref[...]) if final_norm else x2


def _const_spec(shape):
    return pl.BlockSpec(shape, lambda *_: (0,) * len(shape))


def _resident_spec(shape):
    return pl.BlockSpec(shape, lambda *_: (0,) * len(shape), pipeline_mode=pl.Buffered(1))


def _params(n_axes):
    return pltpu.CompilerParams(dimension_semantics=("arbitrary",) * n_axes,
                                vmem_limit_bytes=VMEM_LIMIT_BYTES)


def _pad_heads(w, width_in, offset_out):
    k = w.shape[0]
    w = w.reshape(k, MLA_HEADS, width_in)
    w = jnp.pad(w, ((0, 0), (0, 0), (offset_out, HEAD_PAD - width_in - offset_out)))
    return w.reshape(k, MLA_HEADS * HEAD_PAD)


def kernel(x, positions, norm_mix_g, w_in, mla_q_norm_g, mla_w_uq, mla_kv_norm_g, mla_w_ukv, mlstm_conv_w, mlstm_conv_b, mlstm_igate_b, mlstm_fgate_b, mlstm_out_norm_g, w_branch_mla, w_branch_mlstm, w_out, norm_mlp_g, w_mlp_up, w_mlp_down, norm_final_g):
    B, S, D = x.shape
    T = B * S
    depth = w_in.shape[0]
    q_lora = mla_q_norm_g.shape[1]
    kv_lora = mla_kv_norm_g.shape[1]
    mw = MLSTM_HEADS * MLSTM_DH
    n_gate = 4 * MLSTM_HEADS
    L = MLSTM_CHUNK
    nc = S // L
    tm = tm4 = ROW_TILE
    tq, tk = ATTN_Q_TILE, ATTN_KEY_STEP
    assert S % (2 * MLSTM_POSITIONS_PER_BODY * L) == 0 and MLA_HEADS % 2 == 0
    assert S % tk == 0 and S % tq == 0 and tk % tm == 0 and tm % HALO == 0

    pos_f = positions.astype(F32).reshape(1, T)
    inv_freq = (ROPE_THETA ** (-jnp.arange(0, MLA_ROPE, 2, dtype=F32) / MLA_ROPE)).reshape(-1, 1)
    xf = x.reshape(T, D)

    for l in range(depth):
        offs, o = [], 0
        for w in (q_lora, kv_lora, MLA_ROPE, mw, mw, mw, mw, n_gate, D, D):
            offs.append(o)
            o += w
        wl = w_in[l]
        w_cq = wl[:, offs[0]:offs[0] + q_lora]
        w_ckv = wl[:, offs[1]:offs[1] + kv_lora]
        w_kr = wl[:, offs[2]:offs[2] + MLA_ROPE]
        pad_kr = ((0, 0), (MLA_NOPE, HEAD_PAD - MLA_NOPE - MLA_ROPE))
        w1 = wl[:, offs[3]:offs[3] + 2 * mw].astype(BF16)
        w2t = wl[:, offs[5]:offs[5] + 2 * mw].T.astype(BF16)
        w3 = jnp.concatenate([w_cq, w_ckv, jnp.pad(w_kr, pad_kr)], axis=1).astype(BF16)
        w2t = jnp.concatenate([w2t, wl[:, offs[7]:offs[7] + n_gate].T.astype(BF16)], axis=0)
        wgab = wl[:, offs[8]:offs[8] + 2 * D].astype(BF16)
        gate_b = jnp.stack([mlstm_igate_b[l], mlstm_fgate_b[l]], axis=1).reshape(n_gate, 1)

        wqa = _pad_heads(mla_w_uq[l], MLA_NOPE + MLA_ROPE, 0).astype(BF16)
        wkv = mla_w_ukv[l].reshape(kv_lora, MLA_HEADS, MLA_NOPE + MLA_V)
        wuk = _pad_heads(wkv[:, :, :MLA_NOPE].reshape(kv_lora, -1), MLA_NOPE, 0).astype(BF16)
        wuvt = wkv[:, :, MLA_NOPE:].reshape(kv_lora, MLA_HEADS * MLA_V).T.astype(BF16)

        cw = jnp.pad(mlstm_conv_w[l].reshape(CONV_WIDTH, 2 * mw), ((0, F32_SUBLANES - CONV_WIDTH), (0, 0)))
        cb = mlstm_conv_b[l].reshape(1, 2 * mw)

        tiles_per_seq = S // tm
        hb = tm // HALO
        n_halo_blocks = T // HALO
        row = lambda i: (i, 0)
        col = lambda i: (0, i)
        chunk = lambda i: (i, 0, 0)
        kern = functools.partial(
            _inproj_kernel, tm=tm, tiles_per_seq=tiles_per_seq, q_lora=q_lora, kv_lora=kv_lora, mw=mw,
            q_scale=(MLA_NOPE + MLA_ROPE) ** -0.5 * math.log2(math.e), k_scale=MLSTM_DH ** -0.5)
        outs = pl.pallas_call(
            kern,
            grid=(T // tm,),
            in_specs=[
                pl.BlockSpec((tm, D), row),
                pl.BlockSpec((HALO, D), lambda i: (jnp.maximum(i * hb - 1, 0), 0)),
                pl.BlockSpec((HALO, D), lambda i: (jnp.minimum((i + 1) * hb, n_halo_blocks - 1), 0)),
                pl.BlockSpec((1, tm), col),
                _const_spec((MLA_ROPE // 2, 1)),
                _const_spec((1, D)),
                _const_spec(w1.shape), _const_spec(w2t.shape), _const_spec(w3.shape),
                _const_spec(gate_b.shape),
                _const_spec(cw.shape), _const_spec(cb.shape),
                _const_spec((1, q_lora)), _const_spec(wqa.shape),
                _const_spec((1, kv_lora)), _const_spec(wuk.shape), _const_spec(wuvt.shape),
            ],
            out_specs=[
                pl.BlockSpec((tm, MLA_HEADS * HEAD_PAD), row),
                pl.BlockSpec((tm, MLA_HEADS * HEAD_PAD), row),
                pl.BlockSpec((1, MLA_HEADS * HEAD_PAD, tm), chunk),
                pl.BlockSpec((tm, mw), row), pl.BlockSpec((tm, mw), row),
                pl.BlockSpec((1, mw, tm), chunk),
                pl.BlockSpec((1, mw, tm), chunk),
                pl.BlockSpec((n_gate, tm), col),
            ],
            out_shape=[
                jax.ShapeDtypeStruct((T, MLA_HEADS * HEAD_PAD), BF16),
                jax.ShapeDtypeStruct((T, MLA_HEADS * HEAD_PAD), BF16),
                jax.ShapeDtypeStruct((T // tm, MLA_HEADS * HEAD_PAD, tm), BF16),
                jax.ShapeDtypeStruct((T, mw), BF16), jax.ShapeDtypeStruct((T, mw), BF16),
                jax.ShapeDtypeStruct((T // tm, mw, tm), BF16),
                jax.ShapeDtypeStruct((T // tm, mw, tm), BF16),
                jax.ShapeDtypeStruct((n_gate, T), F32),
            ],
            compiler_params=_params(1),
            name="inproj",
        )(xf, xf, xf, pos_f, inv_freq, norm_mix_g[l].reshape(1, D), w1, w2t, w3, gate_b, cw, cb,
          mla_q_norm_g[l].reshape(1, q_lora), wqa, mla_kv_norm_g[l].reshape(1, kv_lora), wuk, wuvt)
        q_a, k_a, v_a, q_m, k_m, v_m, mo_s, gates = outs

        vchunk = tm
        y_attn = pl.pallas_call(
            functools.partial(_attn_kernel, tq=tq, tk=tk, vchunk=vchunk, n_q=S // tq, n_k=S // tk),
            grid=(B, MLA_HEADS // 2),
            in_specs=[
                pl.BlockSpec((S, 2 * HEAD_PAD), lambda b, p: (b, p)),
                pl.BlockSpec((S, 2 * HEAD_PAD), lambda b, p: (b, p)),
                pl.BlockSpec((S // vchunk, 2 * HEAD_PAD, vchunk), lambda b, p: (b, p, 0)),
            ],
            out_specs=pl.BlockSpec((S, 2 * MLA_V), lambda b, p: (b, p)),
            out_shape=jax.ShapeDtypeStruct((T, MLA_HEADS * MLA_V), BF16),
            scratch_shapes=[pltpu.VMEM((2, tk, tq), F32), pltpu.VMEM((2, tk, tq), F32),
                            pltpu.VMEM((2, tk, tq), BF16), pltpu.VMEM((2, tk, tq), BF16),
                            pltpu.VMEM((2, tk, tq), BF16), pltpu.VMEM((2, tk, tq), BF16),
                            pltpu.VMEM((2, HEAD_PAD, tq), F32)],
            compiler_params=_params(2),
            name="mla_attn",
        )(q_a, k_a, v_a)

        gates5 = gates.reshape(2, 2, MLSTM_HEADS, T // L, L)
        head_blk = lambda b, h: (b, h)
        head_blk_t = lambda b, h: (b, h, 0)
        aug = MLSTM_DH + BF16_SUBLANES
        row_scratch = pltpu.VMEM((2, nc, L), F32)
        y_mlstm_t = pl.pallas_call(
            functools.partial(_mlstm_kernel, L=L, nc=nc),
            grid=(B, MLSTM_HEADS),
            in_specs=[
                pl.BlockSpec((S, MLSTM_DH), head_blk), pl.BlockSpec((S, MLSTM_DH), head_blk),
                pl.BlockSpec((nc, MLSTM_DH, L), head_blk_t), pl.BlockSpec((nc, MLSTM_DH, L), head_blk_t),
                pl.BlockSpec((2, 2, 1, nc, L), lambda b, h: (0, 0, h, b, 0)),
                pl.BlockSpec((MLSTM_DH, 1), lambda b, h: (h, 0)),
            ],
            out_specs=pl.BlockSpec((nc, MLSTM_DH, L), head_blk_t),
            out_shape=jax.ShapeDtypeStruct((T // L, mw, L), BF16),
            scratch_shapes=[pltpu.VMEM((nc, MLSTM_DH, L), F32),
                            pltpu.VMEM((2, aug, MLSTM_DH), F32),
                            row_scratch, row_scratch, row_scratch,
                            pltpu.VMEM((2, nc, 1), F32), pltpu.VMEM((2, nc, 1), F32)],
            compiler_params=_params(2),
            name="mlstm",
        )(q_m, k_m, v_m, mo_s, gates5, mlstm_out_norm_g[l].reshape(mw, 1))

        last = l == depth - 1
        gf = norm_final_g.reshape(1, D)
        weights4 = (wgab, w_branch_mla[l].astype(BF16), w_branch_mlstm[l].astype(BF16),
                    w_out[l].astype(BF16), w_mlp_up[l].astype(BF16), w_mlp_down[l].astype(BF16))
        xf = pl.pallas_call(
            functools.partial(_merge_mlp_kernel, d=D, final_norm=last),
            grid=(T // tm4,),
            in_specs=[
                pl.BlockSpec((tm4, D), row),
                pl.BlockSpec((tm4, MLA_HEADS * MLA_V), row),
                pl.BlockSpec((1, mw, tm4), lambda i: (i, 0, 0)),
                _const_spec((1, D)),
                _resident_spec(weights4[0].shape), _resident_spec(weights4[1].shape),
                _resident_spec(weights4[2].shape), _resident_spec(weights4[3].shape),
                _const_spec((1, D)),
                _resident_spec(weights4[4].shape), _resident_spec(weights4[5].shape),
                _const_spec((1, D)),
            ],
            out_specs=pl.BlockSpec((tm4, D), row),
            out_shape=jax.ShapeDtypeStruct((T, D), F32),
            compiler_params=_params(1),
            name="merge_mlp",
        )(xf, y_attn, y_mlstm_t, norm_mix_g[l].reshape(1, D), weights4[0], weights4[1], weights4[2],
          weights4[3], norm_mlp_g[l].reshape(1, D), weights4[4], weights4[5], gf)

    return xf.reshape(B, S, D)
```
